```python
import math
import jax
import jax.numpy as jnp
from jax import lax
import numpy as np

D_MODEL = 2048
BATCH = 1
SEQ = 8192
DEPTH = 2

EPS = 1e-6

POOL_WINDOWS = (2, 4, 8, 16)
POOL_GROUPS = len(POOL_WINDOWS)
POOL_WIDTH = D_MODEL // 2
POOL_GC = POOL_WIDTH // POOL_GROUPS

DSA_HEADS = 16
DSA_HEAD_DIM = 64
DSA_WIDTH = DSA_HEADS * DSA_HEAD_DIM
IDX_HEADS = 16
IDX_DIM = 64
DSA_MAX_TOPK = 256
Q_BLOCK = 128

SWA_Q_HEADS = 32
SWA_KV_HEADS = 4
SWA_HEAD_DIM = 64
SWA_WIDTH = SWA_Q_HEADS * SWA_HEAD_DIM
WINDOW = 128

NUM_BUCKETS = 32
MAX_DISTANCE = 1024
BIAS_HEADS = 32

EVEN_SPLITS = (POOL_WIDTH, POOL_WIDTH,
               DSA_WIDTH, DSA_HEAD_DIM, DSA_HEAD_DIM,
               DSA_WIDTH,
               IDX_HEADS * IDX_DIM, IDX_DIM, IDX_HEADS)
EVEN_IN = sum(EVEN_SPLITS)
EVEN_OUT_IN = POOL_WIDTH + DSA_WIDTH
ODD_SPLITS = (SWA_WIDTH, SWA_KV_HEADS * SWA_HEAD_DIM, SWA_KV_HEADS * SWA_HEAD_DIM, SWA_WIDTH)
ODD_IN = sum(ODD_SPLITS)

kernel_name = "hybrid_pool_dsa_swa_sink_block"


def rms_norm(x, g):
    xf = x.astype(jnp.float32)
    y = xf * lax.rsqrt(jnp.mean(xf * xf, axis=-1, keepdims=True) + EPS) * g.astype(jnp.float32)
    return y.astype(x.dtype)


def split_cols(h, sizes):
    return jnp.split(h, np.cumsum(sizes)[:-1].tolist(), axis=-1)


def t5_bucket(n):
    n = jnp.maximum(n, 0)
    max_exact = NUM_BUCKETS // 2
    nf = jnp.maximum(n, 1).astype(jnp.float32)
    large = max_exact + (jnp.log(nf / max_exact) / math.log(MAX_DISTANCE / max_exact)
                         * (NUM_BUCKETS - max_exact)).astype(jnp.int32)
    large = jnp.minimum(large, NUM_BUCKETS - 1)
    return jnp.where(n < max_exact, n, large)


def causal_multiscale_pool(a):
    B, T, G, C = a.shape
    cs = jnp.cumsum(a.astype(jnp.float32), axis=1)
    csp = jnp.concatenate([jnp.zeros((B, 1, G, C), jnp.float32), cs], axis=1)
    t = jnp.arange(T)
    outs = []
    for g, w in enumerate(POOL_WINDOWS):
        lo = jnp.maximum(t + 1 - w, 0)
        s = csp[:, t + 1, g] - csp[:, lo, g]
        cnt = (t + 1 - lo).astype(jnp.float32)
        outs.append(s / cnt[None, :, None])
    return jnp.stack(outs, axis=2).astype(a.dtype)


def dsa_attention(q, k, v, iq, ik, iw, rel_bias, topk):
    B, T, H, d = q.shape
    nb = T // Q_BLOCK
    scale = DSA_HEAD_DIM ** -0.5
    keys = jnp.arange(T)

    def to_blocks(z):
        return jnp.swapaxes(z.reshape((B, nb, Q_BLOCK) + z.shape[2:]), 0, 1)

    def block_fn(args):
        qb, iqb, iwb, start = args
        pos_q = start + jnp.arange(Q_BLOCK)
        s_idx = jnp.einsum('bqhe,bse->bqhs', iqb, ik).astype(jnp.float32) * (IDX_DIM ** -0.5)
        score = jnp.einsum('bqhs,bqh->bqs', jax.nn.relu(s_idx), iwb.astype(jnp.float32))
        score = jnp.where(keys[None, None, :] <= pos_q[None, :, None], score, -jnp.inf)
        _, idx = lax.top_k(score, topk)
        valid = idx <= pos_q[None, :, None]
        kg = jax.vmap(lambda kk, ii: kk[ii])(k, idx)
        vg = jax.vmap(lambda vv, ii: vv[ii])(v, idx)
        logits = jnp.einsum('bqhd,bqkd->bhqk', qb, kg).astype(jnp.float32) * scale
        bias = rel_bias[t5_bucket(pos_q[None, :, None] - idx)][..., :H]
        logits = logits + jnp.transpose(bias, (0, 3, 1, 2)).astype(jnp.float32)
        logits = jnp.where(valid[:, None], logits, -jnp.inf)
        p = jax.nn.softmax(logits, axis=-1).astype(vg.dtype)
        return jnp.einsum('bhqk,bqkd->bqhd', p, vg)

    starts = jnp.arange(nb) * Q_BLOCK
    out = lax.map(block_fn, (to_blocks(q), to_blocks(iq), to_blocks(iw), starts))
    return jnp.swapaxes(out, 0, 1).reshape(B, T, H, d)


def swa_sink_attention(q, k, v, sinks, rel_bias):
    B, T, HQ, d = q.shape
    HKV = k.shape[2]
    G = HQ // HKV
    W = WINDOW
    nb = T // W
    qb = q.reshape(B, nb, W, HKV, G, d)

    def with_prev(z):
        zb = z.reshape(B, nb, W, HKV, d)
        prev = jnp.concatenate([jnp.zeros_like(zb[:, :1]), zb[:, :-1]], axis=1)
        return jnp.concatenate([prev, zb], axis=2)

    kw, vw = with_prev(k), with_prev(v)
    logits = jnp.einsum('bnqhgd,bnkhd->bnhgqk', qb, kw).astype(jnp.float32) * (d ** -0.5)
    i = jnp.arange(W)
    j = jnp.arange(2 * W)
    dist = W + i[:, None] - j[None, :]
    bias = rel_bias[t5_bucket(dist)][..., :HQ]
    bias = jnp.transpose(bias, (2, 0, 1)).reshape(HKV, G, W, 2 * W).astype(jnp.float32)
    in_window = (dist >= 0) & (dist < W)
    has_prev = (jnp.arange(nb)[:, None, None] > 0) | (j[None, None, :] >= W)
    mask = in_window[None] & has_prev
    logits = jnp.where(mask[None, :, None, None], logits + bias, -jnp.inf)
    sink = jnp.broadcast_to(sinks.astype(jnp.float32).reshape(HKV, G)[None, None, :, :, None, None],
                            (B, nb, HKV, G, W, 1))
    p = jax.nn.softmax(jnp.concatenate([logits, sink], axis=-1), axis=-1)[..., :-1]
    out = jnp.einsum('bnhgqk,bnkhd->bnqhgd', p.astype(vw.dtype), vw)
    return out.reshape(B, T, HQ * d)


def even_layer(x, norm_g, w_in, pool_w, pool_scale, q_gain, k_gain, w_out, rel_bias):
    B, T, _ = x.shape
    h = rms_norm(x, norm_g) @ w_in
    p_in, p_gate, dq, dk, dv, d_gate, iq, ik, iw = split_cols(h, EVEN_SPLITS)
    pa = p_in.reshape(B, T, POOL_GROUPS, POOL_GC)
    pooled = causal_multiscale_pool(pa) - pa
    py = jnp.einsum('btgc,gcd->btgd', pooled, pool_w) * pool_scale.reshape(POOL_GROUPS, POOL_GC)
    py = jax.nn.silu(p_gate) * py.reshape(B, T, POOL_WIDTH)
    q = rms_norm(dq.reshape(B, T, DSA_HEADS, DSA_HEAD_DIM), q_gain)
    k = rms_norm(dk, k_gain)
    iq = iq.reshape(B, T, IDX_HEADS, IDX_DIM)
    iw = iw * (IDX_HEADS ** -0.5)
    topk = min(DSA_MAX_TOPK, T // 4)
    o = dsa_attention(q, k, dv, iq, ik, iw, rel_bias, topk)
    dy = jax.nn.silu(d_gate) * o.reshape(B, T, DSA_WIDTH)
    return x + jnp.concatenate([py, dy], axis=-1) @ w_out


def odd_layer(x, norm_g, w_in, q_gain, k_gain, sinks, w_out, rel_bias):
    B, T, _ = x.shape
    h = rms_norm(x, norm_g) @ w_in
    q, k, v, gate = split_cols(h, ODD_SPLITS)
    q = rms_norm(q.reshape(B, T, SWA_Q_HEADS, SWA_HEAD_DIM), q_gain)
    k = rms_norm(k.reshape(B, T, SWA_KV_HEADS, SWA_HEAD_DIM), k_gain)
    v = v.reshape(B, T, SWA_KV_HEADS, SWA_HEAD_DIM)
    o = swa_sink_attention(q, k, v, sinks, rel_bias)
    return x + (jax.nn.silu(gate) * o) @ w_out


def setup_inputs(seed: int = 0) -> dict:
    key = jax.random.key(seed)
    ks = jax.random.split(key, 16)
    n_even = (DEPTH + 1) // 2
    n_odd = DEPTH // 2
    nrm = jax.random.normal
    f32 = jnp.float32
    return {
        "x": nrm(ks[0], (BATCH, SEQ, D_MODEL), f32),
        "rel_bias": 0.5 * nrm(ks[1], (NUM_BUCKETS, BIAS_HEADS), f32),
        "even_norm": 1.0 + 0.02 * nrm(ks[2], (n_even, D_MODEL), f32),
        "even_w_in": nrm(ks[3], (n_even, D_MODEL, EVEN_IN), f32) * D_MODEL ** -0.5,
        "even_pool_w": nrm(ks[4], (n_even, POOL_GROUPS, POOL_GC, POOL_GC), f32) * POOL_GC ** -0.5,
        "even_pool_scale": 1.0 + 0.02 * nrm(ks[5], (n_even, POOL_WIDTH), f32),
        "even_q_gain": 1.0 + 0.02 * nrm(ks[6], (n_even, DSA_HEAD_DIM), f32),
        "even_k_gain": 1.0 + 0.02 * nrm(ks[7], (n_even, DSA_HEAD_DIM), f32),
        "even_w_out": nrm(ks[8], (n_even, EVEN_OUT_IN, D_MODEL), f32) * EVEN_OUT_IN ** -0.5,
        "odd_norm": 1.0 + 0.02 * nrm(ks[9], (n_odd, D_MODEL), f32),
        "odd_w_in": nrm(ks[10], (n_odd, D_MODEL, ODD_IN), f32) * D_MODEL ** -0.5,
        "odd_q_gain": 1.0 + 0.02 * nrm(ks[11], (n_odd, SWA_HEAD_DIM), f32),
        "odd_k_gain": 1.0 + 0.02 * nrm(ks[12], (n_odd, SWA_HEAD_DIM), f32),
        "odd_sinks": nrm(ks[13], (n_odd, SWA_Q_HEADS), f32),
        "odd_w_out": nrm(ks[14], (n_odd, SWA_WIDTH, D_MODEL), f32) * SWA_WIDTH ** -0.5,
    }


def reference(x, rel_bias, even_norm, even_w_in, even_pool_w, even_pool_scale, even_q_gain,
              even_k_gain, even_w_out, odd_norm, odd_w_in, odd_q_gain, odd_k_gain, odd_sinks,
              odd_w_out):
    for layer in range(DEPTH):
        i = layer // 2
        if layer % 2 == 0:
            x = even_layer(x, even_norm[i], even_w_in[i], even_pool_w[i], even_pool_scale[i],
                           even_q_gain[i], even_k_gain[i], even_w_out[i], rel_bias)
        else:
            x = odd_layer(x, odd_norm[i], odd_w_in[i], odd_q_gain[i], odd_k_gain[i],
                          odd_sinks[i], odd_w_out[i], rel_bias)
    return x
```

```python
import functools

import jax
import jax.numpy as jnp
from jax import lax
from jax.experimental import pallas as pl
from jax.experimental.pallas import tpu as pltpu

F32 = jnp.float32
BF16 = jnp.bfloat16
I32 = jnp.int32

EPS = 1e-6
HEAD_DIM = 64
QB = 128
POOL_WINDOWS = (2, 4, 8, 16)
POOL_HALO = 16
DSA_HEADS = 16
IDX_HEADS = 16
DSA_TOPK = 256
SWA_Q_HEADS = 32
SWA_KV_HEADS = 4
NUM_BUCKETS = 32
IDX_CHUNK = 512
ATT_CHUNK = 256
INT_MIN = -(2 ** 31)
MASKED = -1e30
TM = 256
VMEM_LIMIT = 56 * 1024 * 1024


def _bucket_starts():
    max_exact = NUM_BUCKETS // 2
    starts = list(range(max_exact + 1))
    n = max_exact
    for b in range(max_exact + 1, NUM_BUCKETS):
        while n ** 16 < max_exact ** 16 * 64 ** (b - max_exact):
            n += 1
        starts.append(n)
    return tuple(starts)


BUCKET_START = _bucket_starts()
FAR_DELTA = -(-(BUCKET_START[-1] + QB - 1) // QB)
N_BIAS_TILES = FAR_DELTA + 1


def _cparams(n_grid=1):
    return pltpu.CompilerParams(dimension_semantics=("arbitrary",) * n_grid,
                                vmem_limit_bytes=VMEM_LIMIT)


def _const_spec(shape):
    return pl.BlockSpec(shape, lambda i: (0,) * len(shape), pipeline_mode=pl.Buffered(1))


def _rms_rows_bf16(x, g):
    ms = jnp.mean(x * x, axis=-1, keepdims=True)
    return (x * lax.rsqrt(ms + EPS) * g).astype(BF16)


def _silu(x):
    return x / (1.0 + jnp.exp(-x))


def _bias_tiles_kernel(rb_ref, out_ref, *, n_heads, rows, base_step, base0):
    base = base0 + pl.program_id(0) * base_step
    shape = (rows, QB)
    dist = base + lax.broadcasted_iota(I32, shape, 1) - lax.broadcasted_iota(I32, shape, 0)
    at_least = [dist >= BUCKET_START[b] for b in range(1, NUM_BUCKETS)]
    for h in range(n_heads):
        val = jnp.full(shape, rb_ref[0, h], F32)
        for b in range(1, NUM_BUCKETS):
            val = jnp.where(at_least[b - 1], rb_ref[b, h], val)
        out_ref[0, h] = val


def _bias_tiles(rel_bias, n_tiles, n_heads, rows, base_step, base0):
    kern = functools.partial(_bias_tiles_kernel, n_heads=n_heads, rows=rows, base_step=base_step, base0=base0)
    return pl.pallas_call(
        kern,
        grid=(n_tiles,),
        in_specs=[pl.BlockSpec(memory_space=pltpu.SMEM)],
        out_specs=pl.BlockSpec((1, n_heads, rows, QB), lambda t: (t, 0, 0, 0)),
        out_shape=jax.ShapeDtypeStruct((n_tiles, n_heads, rows, QB), F32),
        compiler_params=_cparams(),
        name="bias_tiles",
    )(rel_bias)


def _head_t_proj_kernel(x_ref, g_ref, w_ref, gain_ref, out_ref, *, n_heads, normalize, scale):
    xn = _rms_rows_bf16(x_ref[...], g_ref[...])
    h = jnp.dot(xn, w_ref[...], preferred_element_type=F32)
    gain = gain_ref[...]
    for b in range(TM // QB):
        rows = h[b * QB:(b + 1) * QB]
        for p in range(n_heads // 2):
            pair = rows[:, p * 128:(p + 1) * 128].T
            for hh in range(2):
                t = pair[hh * HEAD_DIM:(hh + 1) * HEAD_DIM]
                if normalize:
                    ms = jnp.mean(t * t, axis=0, keepdims=True)
                    t = t * lax.rsqrt(ms + EPS) * gain * scale
                head = 2 * p + hh
                out_ref[b, :, head * QB:(head + 1) * QB] = t.astype(BF16)


def _head_t_proj(x, g, w, gain, *, normalize, scale=1.0):
    T, D = x.shape
    n_heads = w.shape[1] // HEAD_DIM
    gain_b = jnp.broadcast_to(gain.reshape(HEAD_DIM, 1), (HEAD_DIM, QB)).astype(F32)
    kern = functools.partial(_head_t_proj_kernel, n_heads=n_heads, normalize=normalize, scale=scale)
    return pl.pallas_call(
        kern,
        grid=(T // TM,),
        in_specs=[pl.BlockSpec((TM, D), lambda i: (i, 0)),
                  _const_spec((1, D)),
                  _const_spec(w.shape),
                  _const_spec((HEAD_DIM, QB))],
        out_specs=pl.BlockSpec((TM // QB, HEAD_DIM, n_heads * QB), lambda i: (i, 0, 0)),
        out_shape=jax.ShapeDtypeStruct((T // QB, HEAD_DIM, n_heads * QB), BF16),
        compiler_params=_cparams(),
        name="head_t_proj",
    )(x, g.reshape(1, D), w, gain_b)


def _pool_kernel(x_ref, g_ref, w_ref, pw_ref, ps_ref, out_ref, halo_ref):
    i = pl.program_id(0)
    width = out_ref.shape[1]
    gc = width // len(POOL_WINDOWS)
    xn = _rms_rows_bf16(x_ref[...], g_ref[...])
    h = jnp.dot(xn, w_ref[...], preferred_element_type=F32)
    pin = h[:, :width]
    gate = h[:, width:]

    @pl.when(i == 0)
    def _():
        halo_ref[...] = jnp.zeros_like(halo_ref)

    ext = jnp.concatenate([halo_ref[...], pin], axis=0)
    halo_ref[...] = pin[TM - POOL_HALO:]
    pos = i * TM + lax.broadcasted_iota(I32, (TM, gc), 0)
    outs = []
    for grp, win in enumerate(POOL_WINDOWS):
        e = ext[:, grp * gc:(grp + 1) * gc]
        s = e
        span = 1
        while span < win:
            s = s + pltpu.roll(s, span, 0)
            span *= 2
        s = s[POOL_HALO:]
        a = pin[:, grp * gc:(grp + 1) * gc]
        cnt = jnp.minimum(pos + 1, win).astype(F32)
        pooled = s / cnt - a
        y = jnp.dot(pooled.astype(BF16), pw_ref[grp], preferred_element_type=F32)
        outs.append(y * ps_ref[:, grp * gc:(grp + 1) * gc])
    py = jnp.concatenate(outs, axis=1)
    out_ref[...] = (_silu(gate) * py).astype(BF16)


def _pool_mixer(x, g, w, pool_w, pool_scale):
    T, D = x.shape
    width = w.shape[1] // 2
    return pl.pallas_call(
        _pool_kernel,
        grid=(T // TM,),
        in_specs=[pl.BlockSpec((TM, D), lambda i: (i, 0)),
                  _const_spec((1, D)),
                  _const_spec(w.shape),
                  _const_spec(pool_w.shape),
                  _const_spec((1, width))],
        out_specs=pl.BlockSpec((TM, width), lambda i: (i, 0)),
        out_shape=jax.ShapeDtypeStruct((T, width), BF16),
        scratch_shapes=[pltpu.VMEM((POOL_HALO, width), F32)],
        compiler_params=_cparams(),
        name="pool_mixer",
    )(x, g.reshape(1, D), w, pool_w, pool_scale.reshape(1, width))


def _dsa_side_kernel(x_ref, g_ref, w_ref, kg_ref, gate_ref, k_ref, ik_ref, vt_ref, iwt_ref, *, gate_w, iw_scale):
    xn = _rms_rows_bf16(x_ref[...], g_ref[...])
    h = jnp.dot(xn, w_ref[...], preferred_element_type=F32)
    gate_ref[...] = h[:, :gate_w].astype(BF16)
    k = h[:, gate_w:gate_w + HEAD_DIM]
    ms = jnp.mean(k * k, axis=-1, keepdims=True)
    k_ref[...] = (k * lax.rsqrt(ms + EPS) * kg_ref[...]).astype(BF16)
    ik_ref[...] = h[:, gate_w + 128:gate_w + 128 + HEAD_DIM].astype(BF16)
    vw = h[:, gate_w + 256:gate_w + 384].T
    vt_ref[0] = vw[:HEAD_DIM].astype(BF16)
    iwt_ref[...] = vw[HEAD_DIM:HEAD_DIM + IDX_HEADS] * iw_scale


def _dsa_side_proj(x, g, w, k_gain, gate_w, iw_scale):
    T, D = x.shape
    assert TM == ATT_CHUNK
    kern = functools.partial(_dsa_side_kernel, gate_w=gate_w, iw_scale=iw_scale)
    return pl.pallas_call(
        kern,
        grid=(T // TM,),
        in_specs=[pl.BlockSpec((TM, D), lambda i: (i, 0)),
                  _const_spec((1, D)),
                  _const_spec(w.shape),
                  _const_spec((1, HEAD_DIM))],
        out_specs=[pl.BlockSpec((TM, gate_w), lambda i: (i, 0)),
                   pl.BlockSpec((TM, HEAD_DIM), lambda i: (i, 0)),
                   pl.BlockSpec((TM, HEAD_DIM), lambda i: (i, 0)),
                   pl.BlockSpec((1, HEAD_DIM, ATT_CHUNK), lambda i: (i, 0, 0)),
                   pl.BlockSpec((IDX_HEADS, TM), lambda i: (0, i))],
        out_shape=[jax.ShapeDtypeStruct((T, gate_w), BF16),
                   jax.ShapeDtypeStruct((T, HEAD_DIM), BF16),
                   jax.ShapeDtypeStruct((T, HEAD_DIM), BF16),
                   jax.ShapeDtypeStruct((T // ATT_CHUNK, HEAD_DIM, ATT_CHUNK), BF16),
                   jax.ShapeDtypeStruct((IDX_HEADS, T), F32)],
        compiler_params=_cparams(),
        name="dsa_side_proj",
    )(x, g.reshape(1, D), w, k_gain.reshape(1, HEAD_DIM))


def _swa_side_kernel(x_ref, g_ref, w_ref, kg_ref, gate_ref, k_ref, vt_ref, *, gate_w):
    xn = _rms_rows_bf16(x_ref[...], g_ref[...])
    h = jnp.dot(xn, w_ref[...], preferred_element_type=F32)
    gate_ref[...] = h[:, :gate_w].astype(BF16)
    kv_w = SWA_KV_HEADS * HEAD_DIM
    for hd in range(SWA_KV_HEADS):
        k = h[:, gate_w + hd * HEAD_DIM:gate_w + (hd + 1) * HEAD_DIM]
        ms = jnp.mean(k * k, axis=-1, keepdims=True)
        k_ref[hd] = (k * lax.rsqrt(ms + EPS) * kg_ref[...]).astype(BF16)
    for p in range(SWA_KV_HEADS // 2):
        pair = h[:, gate_w + kv_w + p * 128:gate_w + kv_w + (p + 1) * 128].T
        vt_ref[2 * p] = pair[:HEAD_DIM].astype(BF16)
        vt_ref[2 * p + 1] = pair[HEAD_DIM:].astype(BF16)


def _swa_side_proj(x, g, w, k_gain, gate_w):
    T, D = x.shape
    kern = functools.partial(_swa_side_kernel, gate_w=gate_w)
    return pl.pallas_call(
        kern,
        grid=(T // TM,),
        in_specs=[pl.BlockSpec((TM, D), lambda i: (i, 0)),
                  _const_spec((1, D)),
                  _const_spec(w.shape),
                  _const_spec((1, HEAD_DIM))],
        out_specs=[pl.BlockSpec((TM, gate_w), lambda i: (i, 0)),
                   pl.BlockSpec((SWA_KV_HEADS, TM, HEAD_DIM), lambda i: (0, i, 0)),
                   pl.BlockSpec((SWA_KV_HEADS, HEAD_DIM, TM), lambda i: (0, 0, i))],
        out_shape=[jax.ShapeDtypeStruct((T, gate_w), BF16),
                   jax.ShapeDtypeStruct((SWA_KV_HEADS, T, HEAD_DIM), BF16),
                   jax.ShapeDtypeStruct((SWA_KV_HEADS, HEAD_DIM, T), BF16)],
        compiler_params=_cparams(),
        name="swa_side_proj",
    )(x, g.reshape(1, D), w, k_gain.reshape(1, HEAD_DIM))


def _out_proj_kernel(*refs, n_terms):
    x_ref = refs[0]
    out_ref = refs[-1]
    acc = x_ref[...]
    for t in range(n_terms):
        acc = acc + jnp.dot(refs[1 + 2 * t][...], refs[2 + 2 * t][...], preferred_element_type=F32)
    out_ref[...] = acc


def _out_proj(x, terms):
    T, D = x.shape
    args = [x]
    in_specs = [pl.BlockSpec((TM, D), lambda i: (i, 0))]
    for a, w in terms:
        args += [a, w]
        in_specs += [pl.BlockSpec((TM, a.shape[1]), lambda i: (i, 0)), _const_spec(w.shape)]
    return pl.pallas_call(
        functools.partial(_out_proj_kernel, n_terms=len(terms)),
        grid=(T // TM,),
        in_specs=in_specs,
        out_specs=pl.BlockSpec((TM, D), lambda i: (i, 0)),
        out_shape=jax.ShapeDtypeStruct((T, D), F32),
        compiler_params=_cparams(),
        name="out_proj",
    )(*args)


def _store_gated(o_t, gate_ref, out_ref, n_heads):
    for p in range(n_heads // 2):
        pair = jnp.concatenate([o_t[:, (2 * p) * QB:(2 * p + 1) * QB],
                                o_t[:, (2 * p + 1) * QB:(2 * p + 2) * QB]], axis=0)
        g = gate_ref[:, p * 128:(p + 1) * 128].astype(F32)
        out_ref[:, p * 128:(p + 1) * 128] = (_silu(g) * pair.T).astype(BF16)


def _dsa_kernel(qt_ref, iqt_ref, iwt_ref, gate_ref, k_ref, ik_ref, vt_ref, bias_ref, out_ref, keys_ref):
    i = pl.program_id(0)
    n_idx = i // (IDX_CHUNK // QB) + 1
    n_att = i // (ATT_CHUNK // QB) + 1
    lanes = DSA_HEADS * QB

    iqt = iqt_ref[0]
    iwt = iwt_ref[...]
    qpos = i * QB + lax.broadcasted_iota(I32, (IDX_CHUNK, QB), 1)
    krow = lax.broadcasted_iota(I32, (IDX_CHUNK, QB), 0)

    def idx_body(c, carry):
        off = pl.multiple_of(c * IDX_CHUNK, IDX_CHUNK)
        s = jnp.dot(ik_ref[pl.ds(off, IDX_CHUNK), :], iqt, preferred_element_type=F32)
        score = jnp.zeros((IDX_CHUNK, QB), F32)
        for h in range(IDX_HEADS):
            score = score + jnp.maximum(s[:, h * QB:(h + 1) * QB], 0.0) * iwt[h:h + 1, :]
        bits = lax.bitcast_convert_type(score, I32)
        key = jnp.where(bits < 0, bits ^ 0x7FFFFFFF, bits)
        keys_ref[pl.ds(off, IDX_CHUNK), :] = jnp.where(off + krow <= qpos, key, INT_MIN)
        return carry

    lax.fori_loop(0, n_idx, idx_body, 0)

    def count_ge(cand):
        def body(c, acc):
            off = pl.multiple_of(c * IDX_CHUNK, IDX_CHUNK)
            hit = jnp.where(keys_ref[pl.ds(off, IDX_CHUNK), :] >= cand, 1, 0).astype(I32)
            return acc + jnp.sum(hit.reshape(IDX_CHUNK // 8, 8, QB), axis=0)
        acc = lax.fori_loop(0, n_idx, body, jnp.zeros((8, QB), I32))
        return jnp.sum(acc, axis=0, keepdims=True)

    zero = jnp.zeros((1, QB), I32)
    base = jnp.where(count_ge(zero) >= DSA_TOPK, zero, INT_MIN)

    def bit_body(b, base):
        cand = base | jnp.left_shift(jnp.int32(1), 30 - b)
        return jnp.where(count_ge(cand) >= DSA_TOPK, cand, base)

    base = lax.fori_loop(0, 31, bit_body, base)
    thr = jnp.maximum(base, INT_MIN + 1)

    qt = qt_ref[0]
    sub = ATT_CHUNK // QB

    def att_body(c, carry):
        m, l, acc = carry
        off = pl.multiple_of(c * ATT_CHUNK, ATT_CHUNK)
        s = jnp.dot(k_ref[pl.ds(off, ATT_CHUNK), :], qt, preferred_element_type=F32)
        sel = keys_ref[pl.ds(off, ATT_CHUNK), :] >= thr
        blocks = []
        for sb in range(sub):
            delta = jnp.clip(i - (c * sub + sb), 0, FAR_DELTA)
            sel_sb = sel[sb * QB:(sb + 1) * QB]
            row = []
            for h in range(DSA_HEADS):
                t = s[sb * QB:(sb + 1) * QB, h * QB:(h + 1) * QB] + bias_ref[delta, h]
                row.append(jnp.where(sel_sb, t, MASKED))
            blocks.append(jnp.concatenate(row, axis=1))
        t = jnp.concatenate(blocks, axis=0)
        m_new = jnp.maximum(m, jnp.max(t, axis=0, keepdims=True))
        alpha = jnp.exp(m - m_new)
        p = jnp.exp(t - m_new)
        l_new = alpha * l + jnp.sum(p, axis=0, keepdims=True)
        pv = jnp.dot(vt_ref[c], p.astype(BF16), preferred_element_type=F32)
        return m_new, l_new, acc * alpha + pv

    m0 = jnp.full((1, lanes), MASKED, F32)
    l0 = jnp.zeros((1, lanes), F32)
    acc0 = jnp.zeros((HEAD_DIM, lanes), F32)
    _, l, acc = lax.fori_loop(0, n_att, att_body, (m0, l0, acc0))
    _store_gated(acc / l, gate_ref, out_ref, DSA_HEADS)


def _dsa_attention(qt, iqt, iwt, gate, k, ik, vt, bias):
    nb = qt.shape[0]
    T = k.shape[0]
    width = DSA_HEADS * HEAD_DIM
    return pl.pallas_call(
        _dsa_kernel,
        grid=(nb,),
        in_specs=[pl.BlockSpec((1, HEAD_DIM, DSA_HEADS * QB), lambda i: (i, 0, 0)),
                  pl.BlockSpec((1, HEAD_DIM, IDX_HEADS * QB), lambda i: (i, 0, 0)),
                  pl.BlockSpec((IDX_HEADS, QB), lambda i: (0, i)),
                  pl.BlockSpec((QB, width), lambda i: (i, 0)),
                  _const_spec(k.shape),
                  _const_spec(ik.shape),
                  _const_spec(vt.shape),
                  _const_spec(bias.shape)],
        out_specs=pl.BlockSpec((QB, width), lambda i: (i, 0)),
        out_shape=jax.ShapeDtypeStruct((T, width), BF16),
        scratch_shapes=[pltpu.VMEM((T, QB), I32)],
        compiler_params=_cparams(),
        name="dsa_attention",
    )(qt, iqt, iwt, gate, k, ik, vt, bias)


def _swa_kernel(qt_ref, kp_ref, kc_ref, vp_ref, vc_ref, sink_ref, gate_ref, bias_ref, out_ref):
    i = pl.program_id(0)
    grp = SWA_Q_HEADS // SWA_KV_HEADS
    shape = (2 * QB, QB)
    c_idx = lax.broadcasted_iota(I32, shape, 0)
    dist = QB + lax.broadcasted_iota(I32, shape, 1) - c_idx
    mask = (dist >= 0) & (dist < QB) & ((c_idx >= QB) | (i > 0))
    outs = []
    for kv in range(SWA_KV_HEADS):
        kcat = jnp.concatenate([kp_ref[kv], kc_ref[kv]], axis=0)
        vcat = jnp.concatenate([vp_ref[kv], vc_ref[kv]], axis=1)
        qg = qt_ref[0, :, kv * grp * QB:(kv + 1) * grp * QB]
        s = jnp.dot(kcat, qg, preferred_element_type=F32)
        cols = []
        for j in range(grp):
            t = s[:, j * QB:(j + 1) * QB] + bias_ref[kv * grp + j]
            cols.append(jnp.where(mask, t, MASKED))
        t = jnp.concatenate(cols, axis=1)
        sink = sink_ref[:, kv * grp * QB:(kv + 1) * grp * QB]
        m = jnp.maximum(jnp.max(t, axis=0, keepdims=True), sink)
        p = jnp.exp(t - m)
        l = jnp.sum(p, axis=0, keepdims=True) + jnp.exp(sink - m)
        pv = jnp.dot(vcat, p.astype(BF16), preferred_element_type=F32)
        outs.append(pv / l)
    _store_gated(jnp.concatenate(outs, axis=1), gate_ref, out_ref, SWA_Q_HEADS)


def _swa_attention(qt, k_hm, vt_hm, sinks, gate, bias):
    nb = qt.shape[0]
    T = k_hm.shape[1]
    width = SWA_Q_HEADS * HEAD_DIM
    sink_b = jnp.repeat(sinks.astype(F32), QB).reshape(1, SWA_Q_HEADS * QB)

    def prev(i):
        return jnp.maximum(i - 1, 0)

    return pl.pallas_call(
        _swa_kernel,
        grid=(nb,),
        in_specs=[pl.BlockSpec((1, HEAD_DIM, SWA_Q_HEADS * QB), lambda i: (i, 0, 0)),
                  pl.BlockSpec((SWA_KV_HEADS, QB, HEAD_DIM), lambda i: (0, prev(i), 0)),
                  pl.BlockSpec((SWA_KV_HEADS, QB, HEAD_DIM), lambda i: (0, i, 0)),
                  pl.BlockSpec((SWA_KV_HEADS, HEAD_DIM, QB), lambda i: (0, 0, prev(i))),
                  pl.BlockSpec((SWA_KV_HEADS, HEAD_DIM, QB), lambda i: (0, 0, i)),
                  _const_spec((1, SWA_Q_HEADS * QB)),
                  pl.BlockSpec((QB, width), lambda i: (i, 0)),
                  _const_spec(bias.shape)],
        out_specs=pl.BlockSpec((QB, width), lambda i: (i, 0)),
        out_shape=jax.ShapeDtypeStruct((T, width), BF16),
        compiler_params=_cparams(),
        name="swa_attention",
    )(qt, k_hm, k_hm, vt_hm, vt_hm, sink_b, gate, bias)


def _pad_cols(w, width):
    return jnp.pad(w, ((0, 0), (0, width - w.shape[1])))


def _even_layer(x, norm_g, w_in, pool_w, pool_scale, q_gain, k_gain, w_out, dsa_bias):
    pool_width = pool_w.shape[0] * pool_w.shape[1]
    dsa_width = DSA_HEADS * HEAD_DIM
    idx_width = IDX_HEADS * HEAD_DIM
    sizes = (pool_width, pool_width, dsa_width, HEAD_DIM, HEAD_DIM, dsa_width, idx_width, HEAD_DIM, IDX_HEADS)
    offs = [0]
    for s in sizes:
        offs.append(offs[-1] + s)
    col = lambda j: w_in[:, offs[j]:offs[j + 1]]
    w_pool = jnp.concatenate([col(0), col(1)], axis=1).astype(BF16)
    w_q = col(2).astype(BF16)
    w_iq = col(6).astype(BF16)
    w_side = jnp.concatenate([col(5), _pad_cols(col(3), 128), _pad_cols(col(7), 128),
                              _pad_cols(jnp.concatenate([col(4), col(8)], axis=1), 128)], axis=1).astype(BF16)

    py = _pool_mixer(x, norm_g, w_pool, pool_w.astype(BF16), pool_scale)
    qt = _head_t_proj(x, norm_g, w_q, q_gain, normalize=True, scale=HEAD_DIM ** -0.5)
    iqt = _head_t_proj(x, norm_g, w_iq, q_gain, normalize=False)
    gate, k, ik, vt, iwt = _dsa_side_proj(x, norm_g, w_side, k_gain, dsa_width,
                                          iw_scale=IDX_HEADS ** -0.5 * HEAD_DIM ** -0.5)
    dy = _dsa_attention(qt, iqt, iwt, gate, k, ik, vt, dsa_bias)
    w_o = w_out.astype(BF16)
    return _out_proj(x, [(py, w_o[:pool_width]), (dy, w_o[pool_width:])])


def _odd_layer(x, norm_g, w_in, q_gain, k_gain, sinks, w_out, swa_bias):
    q_width = SWA_Q_HEADS * HEAD_DIM
    kv_width = SWA_KV_HEADS * HEAD_DIM
    w_q = w_in[:, :q_width].astype(BF16)
    w_side = jnp.concatenate([w_in[:, q_width + 2 * kv_width:], w_in[:, q_width:q_width + 2 * kv_width]],
                             axis=1).astype(BF16)
    qt = _head_t_proj(x, norm_g, w_q, q_gain, normalize=True, scale=HEAD_DIM ** -0.5)
    gate, k_hm, vt_hm = _swa_side_proj(x, norm_g, w_side, k_gain, q_width)
    go = _swa_attention(qt, k_hm, vt_hm, sinks, gate, swa_bias)
    return _out_proj(x, [(go, w_out.astype(BF16))])


def kernel(x, rel_bias, even_norm, even_w_in, even_pool_w, even_pool_scale, even_q_gain, even_k_gain,
           even_w_out, odd_norm, odd_w_in, odd_q_gain, odd_k_gain, odd_sinks, odd_w_out):
    B, T, D = x.shape
    depth = even_norm.shape[0] + odd_norm.shape[0]
    dsa_bias = _bias_tiles(rel_bias, N_BIAS_TILES, DSA_HEADS, QB, QB, 0)
    swa_bias = _bias_tiles(rel_bias, 1, SWA_Q_HEADS, 2 * QB, 0, QB)[0]
    outs = []
    for b in range(B):
        h = x[b]
        for layer in range(depth):
            j = layer // 2
            if layer % 2 == 0:
                h = _even_layer(h, even_norm[j], even_w_in[j], even_pool_w[j], even_pool_scale[j],
                                even_q_gain[j], even_k_gain[j], even_w_out[j], dsa_bias)
            else:
                h = _odd_layer(h, odd_norm[j], odd_w_in[j], odd_q_gain[j], odd_k_gain[j],
                               odd_sinks[j], odd_w_out[j], swa_bias)
        outs.append(h)
    return jnp.stack(outs, axis=0)
```

```python
import functools

import jax
import jax.numpy as jnp
from jax import lax
from jax.experimental import pallas as pl
from jax.experimental.pallas import tpu as pltpu

F32 = jnp.float32
BF16 = jnp.bfloat16
I32 = jnp.int32

EPS = 1e-6
HEAD_DIM = 64
QB = 128
POOL_WINDOWS = (2, 4, 8, 16)
POOL_HALO = 16
DSA_HEADS = 16
IDX_HEADS = 16
DSA_TOPK = 256
SWA_Q_HEADS = 32
SWA_KV_HEADS = 4
NUM_BUCKETS = 32
IDX_CHUNK = 512
ATT_CHUNK = 256
INT_MIN = -(2 ** 31)
MASKED = -1e30
LOG2E = 1.4426950408889634
QT_ROWS = 128
SAFE_LOG2_RANGE = 60.0
TM = 256
VMEM_LIMIT = 56 * 1024 * 1024


def _bucket_starts():
    max_exact = NUM_BUCKETS // 2
    starts = list(range(max_exact + 1))
    n = max_exact
    for b in range(max_exact + 1, NUM_BUCKETS):
        while n ** 16 < max_exact ** 16 * 64 ** (b - max_exact):
            n += 1
        starts.append(n)
    return tuple(starts)


BUCKET_START = _bucket_starts()
FAR_DELTA = -(-(BUCKET_START[-1] + QB - 1) // QB)
N_BIAS_TILES = FAR_DELTA + 1


def _cparams(n_grid=1):
    return pltpu.CompilerParams(dimension_semantics=("arbitrary",) * n_grid,
                                vmem_limit_bytes=VMEM_LIMIT)


def _const_spec(shape):
    return pl.BlockSpec(shape, lambda i: (0,) * len(shape), pipeline_mode=pl.Buffered(1))


def _rms_rows_bf16(x, g):
    ms = jnp.mean(x * x, axis=-1, keepdims=True)
    return (x * lax.rsqrt(ms + EPS) * g).astype(BF16)


def _silu(x):
    return x / (1.0 + jnp.exp(-x))


def _bias_tiles_kernel(rb_ref, out_ref, *, n_heads, rows, base_step, base0, log2_minus_far):
    base = base0 + pl.program_id(0) * base_step
    shape = (rows, QB)
    dist = base + lax.broadcasted_iota(I32, shape, 1) - lax.broadcasted_iota(I32, shape, 0)
    at_least = [dist >= BUCKET_START[b] for b in range(1, NUM_BUCKETS)]
    for h in range(n_heads):
        val = jnp.full(shape, rb_ref[0, h], F32)
        for b in range(1, NUM_BUCKETS):
            val = jnp.where(at_least[b - 1], rb_ref[b, h], val)
        if log2_minus_far:
            val = (val - rb_ref[NUM_BUCKETS - 1, h]) * LOG2E
        out_ref[0, h] = val


def _bias_tiles(rel_bias, n_tiles, n_heads, rows, base_step, base0, log2_minus_far=False):
    kern = functools.partial(_bias_tiles_kernel, n_heads=n_heads, rows=rows, base_step=base_step, base0=base0,
                             log2_minus_far=log2_minus_far)
    return pl.pallas_call(
        kern,
        grid=(n_tiles,),
        in_specs=[pl.BlockSpec(memory_space=pltpu.SMEM)],
        out_specs=pl.BlockSpec((1, n_heads, rows, QB), lambda t: (t, 0, 0, 0)),
        out_shape=jax.ShapeDtypeStruct((n_tiles, n_heads, rows, QB), F32),
        compiler_params=_cparams(),
        name="bias_tiles",
    )(rel_bias)


def _head_t_proj_kernel(x_ref, g_ref, w_ref, gain_ref, *rest, n_heads, normalize, scale):
    out_ref = rest[-1]
    if len(rest) == 2:
        for b in range(TM // QB):
            out_ref[b, HEAD_DIM:, :] = rest[0][...]
    xn = _rms_rows_bf16(x_ref[...], g_ref[...])
    h = jnp.dot(xn, w_ref[...], preferred_element_type=F32)
    gain = gain_ref[...]
    for b in range(TM // QB):
        rows = h[b * QB:(b + 1) * QB]
        for p in range(n_heads // 2):
            pair = rows[:, p * 128:(p + 1) * 128].T
            for hh in range(2):
                t = pair[hh * HEAD_DIM:(hh + 1) * HEAD_DIM]
                if normalize:
                    ms = jnp.mean(t * t, axis=0, keepdims=True)
                    t = t * lax.rsqrt(ms + EPS) * gain * scale
                head = 2 * p + hh
                out_ref[b, :HEAD_DIM, head * QB:(head + 1) * QB] = t.astype(BF16)


def _head_t_proj(x, g, w, gain, *, normalize, scale=1.0, extra=None):
    T, D = x.shape
    n_heads = w.shape[1] // HEAD_DIM
    gain_b = jnp.broadcast_to(gain.reshape(HEAD_DIM, 1), (HEAD_DIM, QB)).astype(F32)
    kern = functools.partial(_head_t_proj_kernel, n_heads=n_heads, normalize=normalize, scale=scale)
    args = [x, g.reshape(1, D), w, gain_b]
    in_specs = [pl.BlockSpec((TM, D), lambda i: (i, 0)),
                _const_spec((1, D)),
                _const_spec(w.shape),
                _const_spec((HEAD_DIM, QB))]
    rows = HEAD_DIM
    if extra is not None:
        args.append(extra)
        in_specs.append(_const_spec(extra.shape))
        rows += extra.shape[0]
    return pl.pallas_call(
        kern,
        grid=(T // TM,),
        in_specs=in_specs,
        out_specs=pl.BlockSpec((TM // QB, rows, n_heads * QB), lambda i: (i, 0, 0)),
        out_shape=jax.ShapeDtypeStruct((T // QB, rows, n_heads * QB), BF16),
        compiler_params=_cparams(),
        name="head_t_proj",
    )(*args)


def _pool_kernel(x_ref, g_ref, w_ref, pw_ref, ps_ref, out_ref, halo_ref):
    i = pl.program_id(0)
    width = out_ref.shape[1]
    gc = width // len(POOL_WINDOWS)
    xn = _rms_rows_bf16(x_ref[...], g_ref[...])
    h = jnp.dot(xn, w_ref[...], preferred_element_type=F32)
    pin = h[:, :width]
    gate = h[:, width:]

    @pl.when(i == 0)
    def _():
        halo_ref[...] = jnp.zeros_like(halo_ref)

    ext = jnp.concatenate([halo_ref[...], pin], axis=0)
    halo_ref[...] = pin[TM - POOL_HALO:]
    pos = i * TM + lax.broadcasted_iota(I32, (TM, gc), 0)
    outs = []
    for grp, win in enumerate(POOL_WINDOWS):
        e = ext[:, grp * gc:(grp + 1) * gc]
        s = e
        span = 1
        while span < win:
            s = s + pltpu.roll(s, span, 0)
            span *= 2
        s = s[POOL_HALO:]
        a = pin[:, grp * gc:(grp + 1) * gc]
        cnt = jnp.minimum(pos + 1, win).astype(F32)
        pooled = s / cnt - a
        y = jnp.dot(pooled.astype(BF16), pw_ref[grp], preferred_element_type=F32)
        outs.append(y * ps_ref[:, grp * gc:(grp + 1) * gc])
    py = jnp.concatenate(outs, axis=1)
    out_ref[...] = (_silu(gate) * py).astype(BF16)


def _pool_mixer(x, g, w, pool_w, pool_scale):
    T, D = x.shape
    width = w.shape[1] // 2
    return pl.pallas_call(
        _pool_kernel,
        grid=(T // TM,),
        in_specs=[pl.BlockSpec((TM, D), lambda i: (i, 0)),
                  _const_spec((1, D)),
                  _const_spec(w.shape),
                  _const_spec(pool_w.shape),
                  _const_spec((1, width))],
        out_specs=pl.BlockSpec((TM, width), lambda i: (i, 0)),
        out_shape=jax.ShapeDtypeStruct((T, width), BF16),
        scratch_shapes=[pltpu.VMEM((POOL_HALO, width), F32)],
        compiler_params=_cparams(),
        name="pool_mixer",
    )(x, g.reshape(1, D), w, pool_w, pool_scale.reshape(1, width))


def _dsa_side_kernel(x_ref, g_ref, w_ref, kg_ref, gate_ref, k_ref, ik_ref, vt_ref, iwt_ref, *, gate_w, iw_scale):
    xn = _rms_rows_bf16(x_ref[...], g_ref[...])
    h = jnp.dot(xn, w_ref[...], preferred_element_type=F32)
    gate_ref[...] = h[:, :gate_w].astype(BF16)
    k = h[:, gate_w:gate_w + 128]
    ms = jnp.sum(k * k, axis=-1, keepdims=True) * (1.0 / HEAD_DIM)
    kn = k * lax.rsqrt(ms + EPS) * kg_ref[...]
    lane = lax.broadcasted_iota(I32, kn.shape, 1)
    k_ref[...] = jnp.where((lane == HEAD_DIM) | (lane == HEAD_DIM + 1), 1.0, kn).astype(BF16)
    ik_ref[...] = h[:, gate_w + 128:gate_w + 128 + HEAD_DIM].astype(BF16)
    vw = h[:, gate_w + 256:gate_w + 384].T
    vt_ref[0] = vw[:HEAD_DIM].astype(BF16)
    iwt_ref[...] = vw[HEAD_DIM:HEAD_DIM + IDX_HEADS] * iw_scale


def _dsa_side_proj(x, g, w, k_gain, gate_w, iw_scale):
    T, D = x.shape
    assert TM == ATT_CHUNK
    kern = functools.partial(_dsa_side_kernel, gate_w=gate_w, iw_scale=iw_scale)
    return pl.pallas_call(
        kern,
        grid=(T // TM,),
        in_specs=[pl.BlockSpec((TM, D), lambda i: (i, 0)),
                  _const_spec((1, D)),
                  _const_spec(w.shape),
                  _const_spec((1, QT_ROWS))],
        out_specs=[pl.BlockSpec((TM, gate_w), lambda i: (i, 0)),
                   pl.BlockSpec((TM, QT_ROWS), lambda i: (i, 0)),
                   pl.BlockSpec((TM, HEAD_DIM), lambda i: (i, 0)),
                   pl.BlockSpec((1, HEAD_DIM, ATT_CHUNK), lambda i: (i, 0, 0)),
                   pl.BlockSpec((IDX_HEADS, TM), lambda i: (0, i))],
        out_shape=[jax.ShapeDtypeStruct((T, gate_w), BF16),
                   jax.ShapeDtypeStruct((T, QT_ROWS), BF16),
                   jax.ShapeDtypeStruct((T, HEAD_DIM), BF16),
                   jax.ShapeDtypeStruct((T // ATT_CHUNK, HEAD_DIM, ATT_CHUNK), BF16),
                   jax.ShapeDtypeStruct((IDX_HEADS, T), F32)],
        compiler_params=_cparams(),
        name="dsa_side_proj",
    )(x, g.reshape(1, D), w, _pad_cols(k_gain.reshape(1, HEAD_DIM), QT_ROWS))


def _swa_side_kernel(x_ref, g_ref, w_ref, kg_ref, gate_ref, k_ref, vt_ref, *, gate_w):
    xn = _rms_rows_bf16(x_ref[...], g_ref[...])
    h = jnp.dot(xn, w_ref[...], preferred_element_type=F32)
    gate_ref[...] = h[:, :gate_w].astype(BF16)
    kv_w = SWA_KV_HEADS * HEAD_DIM
    for hd in range(SWA_KV_HEADS):
        k = h[:, gate_w + hd * HEAD_DIM:gate_w + (hd + 1) * HEAD_DIM]
        ms = jnp.mean(k * k, axis=-1, keepdims=True)
        k_ref[hd] = (k * lax.rsqrt(ms + EPS) * kg_ref[...]).astype(BF16)
    for p in range(SWA_KV_HEADS // 2):
        pair = h[:, gate_w + kv_w + p * 128:gate_w + kv_w + (p + 1) * 128].T
        vt_ref[2 * p] = pair[:HEAD_DIM].astype(BF16)
        vt_ref[2 * p + 1] = pair[HEAD_DIM:].astype(BF16)


def _swa_side_proj(x, g, w, k_gain, gate_w):
    T, D = x.shape
    kern = functools.partial(_swa_side_kernel, gate_w=gate_w)
    return pl.pallas_call(
        kern,
        grid=(T // TM,),
        in_specs=[pl.BlockSpec((TM, D), lambda i: (i, 0)),
                  _const_spec((1, D)),
                  _const_spec(w.shape),
                  _const_spec((1, HEAD_DIM))],
        out_specs=[pl.BlockSpec((TM, gate_w), lambda i: (i, 0)),
                   pl.BlockSpec((SWA_KV_HEADS, TM, HEAD_DIM), lambda i: (0, i, 0)),
                   pl.BlockSpec((SWA_KV_HEADS, HEAD_DIM, TM), lambda i: (0, 0, i))],
        out_shape=[jax.ShapeDtypeStruct((T, gate_w), BF16),
                   jax.ShapeDtypeStruct((SWA_KV_HEADS, T, HEAD_DIM), BF16),
                   jax.ShapeDtypeStruct((SWA_KV_HEADS, HEAD_DIM, T), BF16)],
        compiler_params=_cparams(),
        name="swa_side_proj",
    )(x, g.reshape(1, D), w, k_gain.reshape(1, HEAD_DIM))


def _out_proj_kernel(*refs, n_terms):
    x_ref = refs[0]
    out_ref = refs[-1]
    acc = x_ref[...]
    for t in range(n_terms):
        acc = acc + jnp.dot(refs[1 + 2 * t][...], refs[2 + 2 * t][...], preferred_element_type=F32)
    out_ref[...] = acc


def _out_proj(x, terms):
    T, D = x.shape
    args = [x]
    in_specs = [pl.BlockSpec((TM, D), lambda i: (i, 0))]
    for a, w in terms:
        args += [a, w]
        in_specs += [pl.BlockSpec((TM, a.shape[1]), lambda i: (i, 0)), _const_spec(w.shape)]
    return pl.pallas_call(
        functools.partial(_out_proj_kernel, n_terms=len(terms)),
        grid=(T // TM,),
        in_specs=in_specs,
        out_specs=pl.BlockSpec((TM, D), lambda i: (i, 0)),
        out_shape=jax.ShapeDtypeStruct((T, D), F32),
        compiler_params=_cparams(),
        name="out_proj",
    )(*args)


def _store_gated(o_t, gate_ref, out_ref, n_heads):
    for p in range(n_heads // 2):
        pair = jnp.concatenate([o_t[:, (2 * p) * QB:(2 * p + 1) * QB],
                                o_t[:, (2 * p + 1) * QB:(2 * p + 2) * QB]], axis=0)
        g = gate_ref[:, p * 128:(p + 1) * 128].astype(F32)
        out_ref[:, p * 128:(p + 1) * 128] = (_silu(g) * pair.T).astype(BF16)


def _dsa_kernel(bounded_ref, qt_ref, iqt_ref, iwt_ref, gate_ref, k_ref, ik_ref, vt_ref, bias_ref, out_ref,
                keys_ref):
    i = pl.program_id(0)
    n_idx = i // (IDX_CHUNK // QB) + 1
    n_att = i // (ATT_CHUNK // QB) + 1
    lanes = DSA_HEADS * QB

    iqt = iqt_ref[0]
    iwt = iwt_ref[...]
    qpos = i * QB + lax.broadcasted_iota(I32, (IDX_CHUNK, QB), 1)
    krow = lax.broadcasted_iota(I32, (IDX_CHUNK, QB), 0)

    def idx_body(c, carry):
        off = pl.multiple_of(c * IDX_CHUNK, IDX_CHUNK)
        s = jnp.dot(ik_ref[pl.ds(off, IDX_CHUNK), :], iqt, preferred_element_type=F32)
        score = jnp.zeros((IDX_CHUNK, QB), F32)
        for h in range(IDX_HEADS):
            score = score + jnp.maximum(s[:, h * QB:(h + 1) * QB], 0.0) * iwt[h:h + 1, :]
        bits = lax.bitcast_convert_type(score, I32)
        key = jnp.where(bits < 0, bits ^ 0x7FFFFFFF, bits)
        keys_ref[pl.ds(off, IDX_CHUNK), :] = jnp.where(off + krow <= qpos, key, INT_MIN)
        return carry

    lax.fori_loop(0, n_idx, idx_body, 0)

    def count_ge(cand):
        def body(c, acc):
            off = pl.multiple_of(c * IDX_CHUNK, IDX_CHUNK)
            hit = jnp.where(keys_ref[pl.ds(off, IDX_CHUNK), :] >= cand, 1, 0).astype(I32)
            return acc + jnp.sum(hit.reshape(IDX_CHUNK // 8, 8, QB), axis=0)
        acc = lax.fori_loop(0, n_idx, body, jnp.zeros((8, QB), I32))
        return jnp.sum(acc, axis=0, keepdims=True)

    zero = jnp.zeros((1, QB), I32)
    base = jnp.where(count_ge(zero) >= DSA_TOPK, zero, INT_MIN)

    def bit_body(b, base):
        cand = base | jnp.left_shift(jnp.int32(1), 30 - b)
        return jnp.where(count_ge(cand) >= DSA_TOPK, cand, base)

    base = lax.fori_loop(0, 31, bit_body, base)
    thr = jnp.maximum(base, INT_MIN + 1)

    qt = qt_ref[0]
    sub = ATT_CHUNK // QB

    def tiles(c, near, fn):
        off = pl.multiple_of(c * ATT_CHUNK, ATT_CHUNK)
        s = jnp.dot(k_ref[pl.ds(off, ATT_CHUNK), :], qt, preferred_element_type=F32)
        sel = keys_ref[pl.ds(off, ATT_CHUNK), :] >= thr
        blocks = []
        for sb in range(sub):
            delta = jnp.clip(i - (c * sub + sb), 0, FAR_DELTA)
            sel_sb = sel[sb * QB:(sb + 1) * QB]
            row = []
            for h in range(DSA_HEADS):
                t = s[sb * QB:(sb + 1) * QB, h * QB:(h + 1) * QB]
                if near:
                    t = t + bias_ref[delta, h]
                row.append(fn(t, sel_sb))
            blocks.append(jnp.concatenate(row, axis=1))
        return jnp.concatenate(blocks, axis=0)

    def bounded_body(near, c, carry):
        l8, acc = carry
        p = tiles(c, near, lambda t, sel: jnp.where(sel, jnp.exp2(t), 0.0))
        l8 = l8 + jnp.sum(p.reshape(ATT_CHUNK // 8, 8, lanes), axis=0)
        return l8, acc + jnp.dot(vt_ref[c], p.astype(BF16), preferred_element_type=F32)

    def bounded_path():
        n_far = jnp.maximum(i - (FAR_DELTA - 1), 0) // sub
        carry = (jnp.zeros((8, lanes), F32), jnp.zeros((HEAD_DIM, lanes), F32))
        carry = lax.fori_loop(0, n_far, functools.partial(bounded_body, False), carry)
        l8, acc = lax.fori_loop(n_far, n_att, functools.partial(bounded_body, True), carry)
        return acc / jnp.sum(l8, axis=0, keepdims=True)

    def general_body(c, carry):
        m, l, acc = carry
        t = tiles(c, True, lambda t, sel: jnp.where(sel, t, MASKED))
        m_new = jnp.maximum(m, jnp.max(t, axis=0, keepdims=True))
        alpha = jnp.exp2(m - m_new)
        p = jnp.exp2(t - m_new)
        l_new = alpha * l + jnp.sum(p, axis=0, keepdims=True)
        pv = jnp.dot(vt_ref[c], p.astype(BF16), preferred_element_type=F32)
        return m_new, l_new, acc * alpha + pv

    def general_path():
        m0 = jnp.full((1, lanes), MASKED, F32)
        l0 = jnp.zeros((1, lanes), F32)
        acc0 = jnp.zeros((HEAD_DIM, lanes), F32)
        _, l, acc = lax.fori_loop(0, n_att, general_body, (m0, l0, acc0))
        return acc / l

    o_t = lax.cond(bounded_ref[0] > 0, bounded_path, general_path)
    _store_gated(o_t, gate_ref, out_ref, DSA_HEADS)


def _dsa_attention(bounded, qt, iqt, iwt, gate, k, ik, vt, bias):
    nb = qt.shape[0]
    T = k.shape[0]
    width = DSA_HEADS * HEAD_DIM
    return pl.pallas_call(
        _dsa_kernel,
        grid=(nb,),
        in_specs=[pl.BlockSpec(memory_space=pltpu.SMEM),
                  pl.BlockSpec((1, QT_ROWS, DSA_HEADS * QB), lambda i: (i, 0, 0)),
                  pl.BlockSpec((1, HEAD_DIM, IDX_HEADS * QB), lambda i: (i, 0, 0)),
                  pl.BlockSpec((IDX_HEADS, QB), lambda i: (0, i)),
                  pl.BlockSpec((QB, width), lambda i: (i, 0)),
                  _const_spec(k.shape),
                  _const_spec(ik.shape),
                  _const_spec(vt.shape),
                  _const_spec(bias.shape)],
        out_specs=pl.BlockSpec((QB, width), lambda i: (i, 0)),
        out_shape=jax.ShapeDtypeStruct((T, width), BF16),
        scratch_shapes=[pltpu.VMEM((T, QB), I32)],
        compiler_params=_cparams(),
        name="dsa_attention",
    )(bounded, qt, iqt, iwt, gate, k, ik, vt, bias)


def _swa_kernel(qt_ref, kp_ref, kc_ref, vp_ref, vc_ref, sink_ref, gate_ref, bias_ref, out_ref):
    i = pl.program_id(0)
    grp = SWA_Q_HEADS // SWA_KV_HEADS
    shape = (2 * QB, QB)
    c_idx = lax.broadcasted_iota(I32, shape, 0)
    dist = QB + lax.broadcasted_iota(I32, shape, 1) - c_idx
    mask = (dist >= 0) & (dist < QB) & ((c_idx >= QB) | (i > 0))
    outs = []
    for kv in range(SWA_KV_HEADS):
        kcat = jnp.concatenate([kp_ref[kv], kc_ref[kv]], axis=0)
        vcat = jnp.concatenate([vp_ref[kv], vc_ref[kv]], axis=1)
        qg = qt_ref[0, :, kv * grp * QB:(kv + 1) * grp * QB]
        s = jnp.dot(kcat, qg, preferred_element_type=F32)
        cols = []
        for j in range(grp):
            t = s[:, j * QB:(j + 1) * QB] + bias_ref[kv * grp + j]
            cols.append(jnp.where(mask, t, MASKED))
        t = jnp.concatenate(cols, axis=1)
        sink = sink_ref[:, kv * grp * QB:(kv + 1) * grp * QB]
        m = jnp.maximum(jnp.max(t, axis=0, keepdims=True), sink)
        p = jnp.exp(t - m)
        l = jnp.sum(p, axis=0, keepdims=True) + jnp.exp(sink - m)
        pv = jnp.dot(vcat, p.astype(BF16), preferred_element_type=F32)
        outs.append(pv / l)
    _store_gated(jnp.concatenate(outs, axis=1), gate_ref, out_ref, SWA_Q_HEADS)


def _swa_attention(qt, k_hm, vt_hm, sinks, gate, bias):
    nb = qt.shape[0]
    T = k_hm.shape[1]
    width = SWA_Q_HEADS * HEAD_DIM
    sink_b = jnp.repeat(sinks.astype(F32), QB).reshape(1, SWA_Q_HEADS * QB)

    def prev(i):
        return jnp.maximum(i - 1, 0)

    return pl.pallas_call(
        _swa_kernel,
        grid=(nb,),
        in_specs=[pl.BlockSpec((1, HEAD_DIM, SWA_Q_HEADS * QB), lambda i: (i, 0, 0)),
                  pl.BlockSpec((SWA_KV_HEADS, QB, HEAD_DIM), lambda i: (0, prev(i), 0)),
                  pl.BlockSpec((SWA_KV_HEADS, QB, HEAD_DIM), lambda i: (0, i, 0)),
                  pl.BlockSpec((SWA_KV_HEADS, HEAD_DIM, QB), lambda i: (0, 0, prev(i))),
                  pl.BlockSpec((SWA_KV_HEADS, HEAD_DIM, QB), lambda i: (0, 0, i)),
                  _const_spec((1, SWA_Q_HEADS * QB)),
                  pl.BlockSpec((QB, width), lambda i: (i, 0)),
                  _const_spec(bias.shape)],
        out_specs=pl.BlockSpec((QB, width), lambda i: (i, 0)),
        out_shape=jax.ShapeDtypeStruct((T, width), BF16),
        compiler_params=_cparams(),
        name="swa_attention",
    )(qt, k_hm, k_hm, vt_hm, vt_hm, sink_b, gate, bias)


def _pad_cols(w, width):
    return jnp.pad(w, ((0, 0), (0, width - w.shape[1])))


def _even_layer(x, norm_g, w_in, pool_w, pool_scale, q_gain, k_gain, w_out, rel_bias, dsa_bias):
    pool_width = pool_w.shape[0] * pool_w.shape[1]
    dsa_width = DSA_HEADS * HEAD_DIM
    idx_width = IDX_HEADS * HEAD_DIM
    sizes = (pool_width, pool_width, dsa_width, HEAD_DIM, HEAD_DIM, dsa_width, idx_width, HEAD_DIM, IDX_HEADS)
    offs = [0]
    for s in sizes:
        offs.append(offs[-1] + s)
    col = lambda j: w_in[:, offs[j]:offs[j + 1]]
    w_pool = jnp.concatenate([col(0), col(1)], axis=1).astype(BF16)
    w_q = col(2).astype(BF16)
    w_iq = col(6).astype(BF16)
    w_side = jnp.concatenate([col(5), _pad_cols(col(3), 128), _pad_cols(col(7), 128),
                              _pad_cols(jnp.concatenate([col(4), col(8)], axis=1), 128)], axis=1).astype(BF16)

    q_scale = HEAD_DIM ** -0.5 * LOG2E
    far = rel_bias[NUM_BUCKETS - 1, :DSA_HEADS].astype(F32) * LOG2E
    far_hi = far.astype(BF16)
    far_lo = (far - far_hi.astype(F32)).astype(BF16)
    extra = jnp.zeros((QT_ROWS - HEAD_DIM, DSA_HEADS * QB), BF16)
    extra = extra.at[0].set(jnp.repeat(far_hi, QB)).at[1].set(jnp.repeat(far_lo, QB))
    q_norm = jnp.sqrt(HEAD_DIM * jnp.max(q_gain.astype(F32) ** 2) * q_scale ** 2 + jnp.max(far ** 2))
    k_norm = jnp.sqrt(HEAD_DIM * jnp.max(k_gain.astype(F32) ** 2) + 2.0)
    near_max = jnp.max(jnp.abs(rel_bias[:, :DSA_HEADS] - rel_bias[NUM_BUCKETS - 1, :DSA_HEADS])) * LOG2E
    bound = 1.02 * q_norm * k_norm + near_max
    bounded = (bound < SAFE_LOG2_RANGE).astype(I32).reshape(1)

    py = _pool_mixer(x, norm_g, w_pool, pool_w.astype(BF16), pool_scale)
    qt = _head_t_proj(x, norm_g, w_q, q_gain, normalize=True, scale=q_scale, extra=extra)
    iqt = _head_t_proj(x, norm_g, w_iq, q_gain, normalize=False)
    gate, k, ik, vt, iwt = _dsa_side_proj(x, norm_g, w_side, k_gain, dsa_width,
                                          iw_scale=IDX_HEADS ** -0.5 * HEAD_DIM ** -0.5)
    dy = _dsa_attention(bounded, qt, iqt, iwt, gate, k, ik, vt, dsa_bias)
    w_o = w_out.astype(BF16)
    return _out_proj(x, [(py, w_o[:pool_width]), (dy, w_o[pool_width:])])


def _odd_layer(x, norm_g, w_in, q_gain, k_gain, sinks, w_out, swa_bias):
    q_width = SWA_Q_HEADS * HEAD_DIM
    kv_width = SWA_KV_HEADS * HEAD_DIM
    w_q = w_in[:, :q_width].astype(BF16)
    w_side = jnp.concatenate([w_in[:, q_width + 2 * kv_width:], w_in[:, q_width:q_width + 2 * kv_width]],
                             axis=1).astype(BF16)
    qt = _head_t_proj(x, norm_g, w_q, q_gain, normalize=True, scale=HEAD_DIM ** -0.5)
    gate, k_hm, vt_hm = _swa_side_proj(x, norm_g, w_side, k_gain, q_width)
    go = _swa_attention(qt, k_hm, vt_hm, sinks, gate, swa_bias)
    return _out_proj(x, [(go, w_out.astype(BF16))])


def kernel(x, rel_bias, even_norm, even_w_in, even_pool_w, even_pool_scale, even_q_gain, even_k_gain,
           even_w_out, odd_norm, odd_w_in, odd_q_gain, odd_k_gain, odd_sinks, odd_w_out):
    B, T, D = x.shape
    depth = even_norm.shape[0] + odd_norm.shape[0]
    dsa_bias = _bias_tiles(rel_bias, N_BIAS_TILES, DSA_HEADS, QB, QB, 0, log2_minus_far=True)
    swa_bias = _bias_tiles(rel_bias, 1, SWA_Q_HEADS, 2 * QB, 0, QB)[0]
    outs = []
    for b in range(B):
        h = x[b]
        for layer in range(depth):
            j = layer // 2
            if layer % 2 == 0:
                h = _even_layer(h, even_norm[j], even_w_in[j], even_pool_w[j], even_pool_scale[j],
                                even_q_gain[j], even_k_gain[j], even_w_out[j], rel_bias, dsa_bias)
            else:
                h = _odd_layer(h, odd_norm[j], odd_w_in[j], odd_q_gain[j], odd_k_gain[j],
                               odd_sinks[j], odd_w_out[j], swa_bias)
        outs.append(h)
    return jnp.stack(outs, axis=0)
```

```python
import functools

import jax
import jax.numpy as jnp
from jax import lax
from jax.experimental import pallas as pl
from jax.experimental.pallas import tpu as pltpu

F32 = jnp.float32
BF16 = jnp.bfloat16
I32 = jnp.int32

EPS = 1e-6
HEAD_DIM = 64
QB = 128
POOL_WINDOWS = (2, 4, 8, 16)
POOL_HALO = 16
DSA_HEADS = 16
IDX_HEADS = 16
DSA_TOPK = 256
SWA_Q_HEADS = 32
SWA_KV_HEADS = 4
NUM_BUCKETS = 32
IDX_CHUNK = 512
ATT_CHUNK = 256
ATT_PAIR = IDX_CHUNK // ATT_CHUNK
INT_MIN = -(2 ** 31)
MASKED = -1e30
LOG2E = 1.4426950408889634
QT_ROWS = 128
VT_ROWS = 80
SAFE_LOG2_RANGE = 60.0
TM = 256
VMEM_LIMIT = 56 * 1024 * 1024


def _bucket_starts():
    max_exact = NUM_BUCKETS // 2
    starts = list(range(max_exact + 1))
    n = max_exact
    for b in range(max_exact + 1, NUM_BUCKETS):
        while n ** 16 < max_exact ** 16 * 64 ** (b - max_exact):
            n += 1
        starts.append(n)
    return tuple(starts)


BUCKET_START = _bucket_starts()
FAR_DELTA = -(-(BUCKET_START[-1] + QB - 1) // QB)
N_BIAS_TILES = FAR_DELTA + 1


def _cparams(n_grid=1):
    return pltpu.CompilerParams(dimension_semantics=("arbitrary",) * n_grid,
                                vmem_limit_bytes=VMEM_LIMIT)


def _const_spec(shape):
    return pl.BlockSpec(shape, lambda i: (0,) * len(shape), pipeline_mode=pl.Buffered(1))


def _rms_rows_bf16(x, g):
    ms = jnp.mean(x * x, axis=-1, keepdims=True)
    return (x * lax.rsqrt(ms + EPS) * g).astype(BF16)


def _silu(x):
    return x / (1.0 + jnp.exp(-x))


def _bias_tiles_kernel(rb_ref, out_ref, *, n_heads, rows, base_step, base0, log2_minus_far):
    base = base0 + pl.program_id(0) * base_step
    shape = (rows, QB)
    dist = base + lax.broadcasted_iota(I32, shape, 1) - lax.broadcasted_iota(I32, shape, 0)
    at_least = [dist >= BUCKET_START[b] for b in range(1, NUM_BUCKETS)]
    for h in range(n_heads):
        val = jnp.full(shape, rb_ref[0, h], F32)
        for b in range(1, NUM_BUCKETS):
            val = jnp.where(at_least[b - 1], rb_ref[b, h], val)
        if log2_minus_far:
            val = (val - rb_ref[NUM_BUCKETS - 1, h]) * LOG2E
        out_ref[0, h] = val


def _bias_tiles(rel_bias, n_tiles, n_heads, rows, base_step, base0, log2_minus_far=False):
    kern = functools.partial(_bias_tiles_kernel, n_heads=n_heads, rows=rows, base_step=base_step, base0=base0,
                             log2_minus_far=log2_minus_far)
    return pl.pallas_call(
        kern,
        grid=(n_tiles,),
        in_specs=[pl.BlockSpec(memory_space=pltpu.SMEM)],
        out_specs=pl.BlockSpec((1, n_heads, rows, QB), lambda t: (t, 0, 0, 0)),
        out_shape=jax.ShapeDtypeStruct((n_tiles, n_heads, rows, QB), F32),
        compiler_params=_cparams(),
        name="bias_tiles",
    )(rel_bias)


def _head_t_proj_kernel(x_ref, g_ref, w_ref, gain_ref, *rest, n_heads, normalize, scale):
    out_ref = rest[-1]
    if len(rest) == 2:
        for b in range(TM // QB):
            out_ref[b, HEAD_DIM:, :] = rest[0][...]
    xn = _rms_rows_bf16(x_ref[...], g_ref[...])
    h = jnp.dot(xn, w_ref[...], preferred_element_type=F32)
    gain = gain_ref[...]
    for b in range(TM // QB):
        rows = h[b * QB:(b + 1) * QB]
        for p in range(n_heads // 2):
            pair = rows[:, p * 128:(p + 1) * 128].T
            for hh in range(2):
                t = pair[hh * HEAD_DIM:(hh + 1) * HEAD_DIM]
                if normalize:
                    ms = jnp.mean(t * t, axis=0, keepdims=True)
                    t = t * lax.rsqrt(ms + EPS) * gain * scale
                head = 2 * p + hh
                out_ref[b, :HEAD_DIM, head * QB:(head + 1) * QB] = t.astype(BF16)


def _head_t_proj(x, g, w, gain, *, normalize, scale=1.0, extra=None):
    T, D = x.shape
    n_heads = w.shape[1] // HEAD_DIM
    gain_b = jnp.broadcast_to(gain.reshape(HEAD_DIM, 1), (HEAD_DIM, QB)).astype(F32)
    kern = functools.partial(_head_t_proj_kernel, n_heads=n_heads, normalize=normalize, scale=scale)
    args = [x, g.reshape(1, D), w, gain_b]
    in_specs = [pl.BlockSpec((TM, D), lambda i: (i, 0)),
                _const_spec((1, D)),
                _const_spec(w.shape),
                _const_spec((HEAD_DIM, QB))]
    rows = HEAD_DIM
    if extra is not None:
        args.append(extra)
        in_specs.append(_const_spec(extra.shape))
        rows += extra.shape[0]
    return pl.pallas_call(
        kern,
        grid=(T // TM,),
        in_specs=in_specs,
        out_specs=pl.BlockSpec((TM // QB, rows, n_heads * QB), lambda i: (i, 0, 0)),
        out_shape=jax.ShapeDtypeStruct((T // QB, rows, n_heads * QB), BF16),
        compiler_params=_cparams(),
        name="head_t_proj",
    )(*args)


def _pool_kernel(x_ref, g_ref, w_ref, pw_ref, ps_ref, out_ref, halo_ref):
    i = pl.program_id(0)
    width = out_ref.shape[1]
    gc = width // len(POOL_WINDOWS)
    xn = _rms_rows_bf16(x_ref[...], g_ref[...])
    h = jnp.dot(xn, w_ref[...], preferred_element_type=F32)
    pin = h[:, :width]
    gate = h[:, width:]

    @pl.when(i == 0)
    def _():
        halo_ref[...] = jnp.zeros_like(halo_ref)

    ext = jnp.concatenate([halo_ref[...], pin], axis=0)
    halo_ref[...] = pin[TM - POOL_HALO:]
    pos = i * TM + lax.broadcasted_iota(I32, (TM, gc), 0)
    outs = []
    for grp, win in enumerate(POOL_WINDOWS):
        e = ext[:, grp * gc:(grp + 1) * gc]
        s = e
        span = 1
        while span < win:
            s = s + pltpu.roll(s, span, 0)
            span *= 2
        s = s[POOL_HALO:]
        a = pin[:, grp * gc:(grp + 1) * gc]
        cnt = jnp.minimum(pos + 1, win).astype(F32)
        pooled = s / cnt - a
        y = jnp.dot(pooled.astype(BF16), pw_ref[grp], preferred_element_type=F32)
        outs.append(y * ps_ref[:, grp * gc:(grp + 1) * gc])
    py = jnp.concatenate(outs, axis=1)
    out_ref[...] = (_silu(gate) * py).astype(BF16)


def _pool_mixer(x, g, w, pool_w, pool_scale):
    T, D = x.shape
    width = w.shape[1] // 2
    return pl.pallas_call(
        _pool_kernel,
        grid=(T // TM,),
        in_specs=[pl.BlockSpec((TM, D), lambda i: (i, 0)),
                  _const_spec((1, D)),
                  _const_spec(w.shape),
                  _const_spec(pool_w.shape),
                  _const_spec((1, width))],
        out_specs=pl.BlockSpec((TM, width), lambda i: (i, 0)),
        out_shape=jax.ShapeDtypeStruct((T, width), BF16),
        scratch_shapes=[pltpu.VMEM((POOL_HALO, width), F32)],
        compiler_params=_cparams(),
        name="pool_mixer",
    )(x, g.reshape(1, D), w, pool_w, pool_scale.reshape(1, width))


def _dsa_side_kernel(x_ref, g_ref, w_ref, kg_ref, gate_ref, k_ref, ik_ref, vt_ref, iwt_ref, *, gate_w, iw_scale):
    xn = _rms_rows_bf16(x_ref[...], g_ref[...])
    h = jnp.dot(xn, w_ref[...], preferred_element_type=F32)
    gate_ref[...] = h[:, :gate_w].astype(BF16)
    k = h[:, gate_w:gate_w + 128]
    ms = jnp.sum(k * k, axis=-1, keepdims=True) * (1.0 / HEAD_DIM)
    kn = k * lax.rsqrt(ms + EPS) * kg_ref[...]
    lane = lax.broadcasted_iota(I32, kn.shape, 1)
    k_ref[...] = jnp.where((lane == HEAD_DIM) | (lane == HEAD_DIM + 1), 1.0, kn).astype(BF16)
    ik_ref[...] = h[:, gate_w + 128:gate_w + 128 + HEAD_DIM].astype(BF16)
    vw = h[:, gate_w + 256:gate_w + 384].T
    row = lax.broadcasted_iota(I32, (VT_ROWS - HEAD_DIM, TM), 0)
    vt_ref[0] = jnp.concatenate([vw[:HEAD_DIM], jnp.where(row == 0, 1.0, 0.0)], axis=0).astype(BF16)
    iwt_ref[...] = vw[HEAD_DIM:HEAD_DIM + IDX_HEADS] * iw_scale


def _dsa_side_proj(x, g, w, k_gain, gate_w, iw_scale):
    T, D = x.shape
    assert TM == ATT_CHUNK
    kern = functools.partial(_dsa_side_kernel, gate_w=gate_w, iw_scale=iw_scale)
    return pl.pallas_call(
        kern,
        grid=(T // TM,),
        in_specs=[pl.BlockSpec((TM, D), lambda i: (i, 0)),
                  _const_spec((1, D)),
                  _const_spec(w.shape),
                  _const_spec((1, QT_ROWS))],
        out_specs=[pl.BlockSpec((TM, gate_w), lambda i: (i, 0)),
                   pl.BlockSpec((TM, QT_ROWS), lambda i: (i, 0)),
                   pl.BlockSpec((TM, HEAD_DIM), lambda i: (i, 0)),
                   pl.BlockSpec((1, VT_ROWS, ATT_CHUNK), lambda i: (i, 0, 0)),
                   pl.BlockSpec((IDX_HEADS, TM), lambda i: (0, i))],
        out_shape=[jax.ShapeDtypeStruct((T, gate_w), BF16),
                   jax.ShapeDtypeStruct((T, QT_ROWS), BF16),
                   jax.ShapeDtypeStruct((T, HEAD_DIM), BF16),
                   jax.ShapeDtypeStruct((T // ATT_CHUNK, VT_ROWS, ATT_CHUNK), BF16),
                   jax.ShapeDtypeStruct((IDX_HEADS, T), F32)],
        compiler_params=_cparams(),
        name="dsa_side_proj",
    )(x, g.reshape(1, D), w, _pad_cols(k_gain.reshape(1, HEAD_DIM), QT_ROWS))


def _swa_side_kernel(x_ref, g_ref, w_ref, kg_ref, gate_ref, k_ref, vt_ref, *, gate_w):
    xn = _rms_rows_bf16(x_ref[...], g_ref[...])
    h = jnp.dot(xn, w_ref[...], preferred_element_type=F32)
    gate_ref[...] = h[:, :gate_w].astype(BF16)
    kv_w = SWA_KV_HEADS * HEAD_DIM
    for hd in range(SWA_KV_HEADS):
        k = h[:, gate_w + hd * HEAD_DIM:gate_w + (hd + 1) * HEAD_DIM]
        ms = jnp.mean(k * k, axis=-1, keepdims=True)
        k_ref[hd] = (k * lax.rsqrt(ms + EPS) * kg_ref[...]).astype(BF16)
    for p in range(SWA_KV_HEADS // 2):
        pair = h[:, gate_w + kv_w + p * 128:gate_w + kv_w + (p + 1) * 128].T
        vt_ref[2 * p] = pair[:HEAD_DIM].astype(BF16)
        vt_ref[2 * p + 1] = pair[HEAD_DIM:].astype(BF16)


def _swa_side_proj(x, g, w, k_gain, gate_w):
    T, D = x.shape
    kern = functools.partial(_swa_side_kernel, gate_w=gate_w)
    return pl.pallas_call(
        kern,
        grid=(T // TM,),
        in_specs=[pl.BlockSpec((TM, D), lambda i: (i, 0)),
                  _const_spec((1, D)),
                  _const_spec(w.shape),
                  _const_spec((1, HEAD_DIM))],
        out_specs=[pl.BlockSpec((TM, gate_w), lambda i: (i, 0)),
                   pl.BlockSpec((SWA_KV_HEADS, TM, HEAD_DIM), lambda i: (0, i, 0)),
                   pl.BlockSpec((SWA_KV_HEADS, HEAD_DIM, TM), lambda i: (0, 0, i))],
        out_shape=[jax.ShapeDtypeStruct((T, gate_w), BF16),
                   jax.ShapeDtypeStruct((SWA_KV_HEADS, T, HEAD_DIM), BF16),
                   jax.ShapeDtypeStruct((SWA_KV_HEADS, HEAD_DIM, T), BF16)],
        compiler_params=_cparams(),
        name="swa_side_proj",
    )(x, g.reshape(1, D), w, k_gain.reshape(1, HEAD_DIM))


def _out_proj_kernel(*refs, n_terms):
    x_ref = refs[0]
    out_ref = refs[-1]
    acc = x_ref[...]
    for t in range(n_terms):
        acc = acc + jnp.dot(refs[1 + 2 * t][...], refs[2 + 2 * t][...], preferred_element_type=F32)
    out_ref[...] = acc


def _out_proj(x, terms):
    T, D = x.shape
    args = [x]
    in_specs = [pl.BlockSpec((TM, D), lambda i: (i, 0))]
    for a, w in terms:
        args += [a, w]
        in_specs += [pl.BlockSpec((TM, a.shape[1]), lambda i: (i, 0)), _const_spec(w.shape)]
    return pl.pallas_call(
        functools.partial(_out_proj_kernel, n_terms=len(terms)),
        grid=(T // TM,),
        in_specs=in_specs,
        out_specs=pl.BlockSpec((TM, D), lambda i: (i, 0)),
        out_shape=jax.ShapeDtypeStruct((T, D), F32),
        compiler_params=_cparams(),
        name="out_proj",
    )(*args)


def _store_gated(o_t, gate_ref, out_ref, n_heads):
    for p in range(n_heads // 2):
        pair = jnp.concatenate([o_t[:, (2 * p) * QB:(2 * p + 1) * QB],
                                o_t[:, (2 * p + 1) * QB:(2 * p + 2) * QB]], axis=0)
        g = gate_ref[:, p * 128:(p + 1) * 128].astype(F32)
        out_ref[:, p * 128:(p + 1) * 128] = (_silu(g) * pair.T).astype(BF16)


def _dsa_kernel(bounded_ref, qt_ref, iqt_ref, iwt_ref, gate_ref, k_ref, ik_ref, vt_ref, bias_ref, out_ref,
                keys_ref):
    i = pl.program_id(0)
    n_idx = i // (IDX_CHUNK // QB) + 1
    lanes = DSA_HEADS * QB

    iqt = iqt_ref[0]
    iwt = iwt_ref[...]
    qpos = i * QB + lax.broadcasted_iota(I32, (IDX_CHUNK, QB), 1)
    krow = lax.broadcasted_iota(I32, (IDX_CHUNK, QB), 0)

    def idx_body(c, carry):
        off = pl.multiple_of(c * IDX_CHUNK, IDX_CHUNK)
        s = jnp.dot(ik_ref[pl.ds(off, IDX_CHUNK), :], iqt, preferred_element_type=F32)
        score = jnp.zeros((IDX_CHUNK, QB), F32)
        for h in range(IDX_HEADS):
            score = score + jnp.maximum(s[:, h * QB:(h + 1) * QB], 0.0) * iwt[h:h + 1, :]
        bits = lax.bitcast_convert_type(score, I32)
        key = jnp.where(bits < 0, bits ^ 0x7FFFFFFF, bits)
        keys_ref[pl.ds(off, IDX_CHUNK), :] = jnp.where(off + krow <= qpos, key, INT_MIN)
        return carry

    lax.fori_loop(0, n_idx, idx_body, 0)

    def count_ge(cand):
        def body(c, acc):
            off = pl.multiple_of(c * IDX_CHUNK, IDX_CHUNK)
            hit = jnp.where(keys_ref[pl.ds(off, IDX_CHUNK), :] >= cand, 1, 0).astype(I32)
            return acc + jnp.sum(hit.reshape(IDX_CHUNK // 8, 8, QB), axis=0)
        acc = lax.fori_loop(0, n_idx, body, jnp.zeros((8, QB), I32))
        return jnp.sum(acc, axis=0, keepdims=True)

    zero = jnp.zeros((1, QB), I32)
    base = jnp.where(count_ge(zero) >= DSA_TOPK, zero, INT_MIN)

    def bit_body(b, base):
        cand = base | jnp.left_shift(jnp.int32(1), 30 - b)
        return jnp.where(count_ge(cand) >= DSA_TOPK, cand, base)

    base = lax.fori_loop(0, 31, bit_body, base)
    thr = jnp.maximum(base, INT_MIN + 1)

    qt = qt_ref[0]
    sub = ATT_CHUNK // QB

    def tiles(c, near, fn):
        off = pl.multiple_of(c * ATT_CHUNK, ATT_CHUNK)
        s = jnp.dot(k_ref[pl.ds(off, ATT_CHUNK), :], qt, preferred_element_type=F32)
        sel = keys_ref[pl.ds(off, ATT_CHUNK), :] >= thr
        blocks = []
        for sb in range(sub):
            delta = jnp.clip(i - (c * sub + sb), 0, FAR_DELTA)
            sel_sb = sel[sb * QB:(sb + 1) * QB]
            row = []
            for h in range(DSA_HEADS):
                t = s[sb * QB:(sb + 1) * QB, h * QB:(h + 1) * QB]
                if near:
                    t = t + bias_ref[delta, h]
                row.append(fn(t, sel_sb))
            blocks.append(jnp.concatenate(row, axis=1))
        return jnp.concatenate(blocks, axis=0)

    def bounded_body(near, pair, acc):
        pvs = []
        for half in range(ATT_PAIR):
            c = pair * ATT_PAIR + half
            p = tiles(c, near, lambda t, sel: jnp.where(sel, jnp.exp2(t), 0.0))
            pvs.append(jnp.dot(vt_ref[c], p.astype(BF16), preferred_element_type=F32))
        return acc + sum(pvs[1:], pvs[0])

    def bounded_path():
        n_far = jnp.maximum(i - (FAR_DELTA - 1), 0) // (sub * ATT_PAIR)
        acc = jnp.zeros((VT_ROWS, lanes), F32)
        acc = lax.fori_loop(0, n_far, functools.partial(bounded_body, False), acc)
        acc = lax.fori_loop(n_far, n_idx, functools.partial(bounded_body, True), acc)
        return acc[:HEAD_DIM] / acc[HEAD_DIM:HEAD_DIM + 1]

    def general_body(c, carry):
        m, l, acc = carry
        t = tiles(c, True, lambda t, sel: jnp.where(sel, t, MASKED))
        m_new = jnp.maximum(m, jnp.max(t, axis=0, keepdims=True))
        alpha = jnp.exp2(m - m_new)
        p = jnp.exp2(t - m_new)
        l_new = alpha * l + jnp.sum(p, axis=0, keepdims=True)
        pv = jnp.dot(vt_ref[c], p.astype(BF16), preferred_element_type=F32)
        return m_new, l_new, acc * alpha + pv

    def general_path():
        m0 = jnp.full((1, lanes), MASKED, F32)
        l0 = jnp.zeros((1, lanes), F32)
        acc0 = jnp.zeros((VT_ROWS, lanes), F32)
        _, l, acc = lax.fori_loop(0, n_idx * ATT_PAIR, general_body, (m0, l0, acc0))
        return acc[:HEAD_DIM] / l

    o_t = lax.cond(bounded_ref[0] > 0, bounded_path, general_path)
    _store_gated(o_t, gate_ref, out_ref, DSA_HEADS)


def _dsa_attention(bounded, qt, iqt, iwt, gate, k, ik, vt, bias):
    nb = qt.shape[0]
    T = k.shape[0]
    width = DSA_HEADS * HEAD_DIM
    return pl.pallas_call(
        _dsa_kernel,
        grid=(nb,),
        in_specs=[pl.BlockSpec(memory_space=pltpu.SMEM),
                  pl.BlockSpec((1, QT_ROWS, DSA_HEADS * QB), lambda i: (i, 0, 0)),
                  pl.BlockSpec((1, HEAD_DIM, IDX_HEADS * QB), lambda i: (i, 0, 0)),
                  pl.BlockSpec((IDX_HEADS, QB), lambda i: (0, i)),
                  pl.BlockSpec((QB, width), lambda i: (i, 0)),
                  _const_spec(k.shape),
                  _const_spec(ik.shape),
                  _const_spec(vt.shape),
                  _const_spec(bias.shape)],
        out_specs=pl.BlockSpec((QB, width), lambda i: (i, 0)),
        out_shape=jax.ShapeDtypeStruct((T, width), BF16),
        scratch_shapes=[pltpu.VMEM((T, QB), I32)],
        compiler_params=_cparams(),
        name="dsa_attention",
    )(bounded, qt, iqt, iwt, gate, k, ik, vt, bias)


def _swa_kernel(qt_ref, kp_ref, kc_ref, vp_ref, vc_ref, sink_ref, gate_ref, bias_ref, out_ref):
    i = pl.program_id(0)
    grp = SWA_Q_HEADS // SWA_KV_HEADS
    shape = (2 * QB, QB)
    c_idx = lax.broadcasted_iota(I32, shape, 0)
    dist = QB + lax.broadcasted_iota(I32, shape, 1) - c_idx
    mask = (dist >= 0) & (dist < QB) & ((c_idx >= QB) | (i > 0))
    outs = []
    for kv in range(SWA_KV_HEADS):
        kcat = jnp.concatenate([kp_ref[kv], kc_ref[kv]], axis=0)
        vcat = jnp.concatenate([vp_ref[kv], vc_ref[kv]], axis=1)
        qg = qt_ref[0, :, kv * grp * QB:(kv + 1) * grp * QB]
        s = jnp.dot(kcat, qg, preferred_element_type=F32)
        cols = []
        for j in range(grp):
            t = s[:, j * QB:(j + 1) * QB] + bias_ref[kv * grp + j]
            cols.append(jnp.where(mask, t, MASKED))
        t = jnp.concatenate(cols, axis=1)
        sink = sink_ref[:, kv * grp * QB:(kv + 1) * grp * QB]
        m = jnp.maximum(jnp.max(t, axis=0, keepdims=True), sink)
        p = jnp.exp(t - m)
        l = jnp.sum(p, axis=0, keepdims=True) + jnp.exp(sink - m)
        pv = jnp.dot(vcat, p.astype(BF16), preferred_element_type=F32)
        outs.append(pv / l)
    _store_gated(jnp.concatenate(outs, axis=1), gate_ref, out_ref, SWA_Q_HEADS)


def _swa_attention(qt, k_hm, vt_hm, sinks, gate, bias):
    nb = qt.shape[0]
    T = k_hm.shape[1]
    width = SWA_Q_HEADS * HEAD_DIM
    sink_b = jnp.repeat(sinks.astype(F32), QB).reshape(1, SWA_Q_HEADS * QB)

    def prev(i):
        return jnp.maximum(i - 1, 0)

    return pl.pallas_call(
        _swa_kernel,
        grid=(nb,),
        in_specs=[pl.BlockSpec((1, HEAD_DIM, SWA_Q_HEADS * QB), lambda i: (i, 0, 0)),
                  pl.BlockSpec((SWA_KV_HEADS, QB, HEAD_DIM), lambda i: (0, prev(i), 0)),
                  pl.BlockSpec((SWA_KV_HEADS, QB, HEAD_DIM), lambda i: (0, i, 0)),
                  pl.BlockSpec((SWA_KV_HEADS, HEAD_DIM, QB), lambda i: (0, 0, prev(i))),
                  pl.BlockSpec((SWA_KV_HEADS, HEAD_DIM, QB), lambda i: (0, 0, i)),
                  _const_spec((1, SWA_Q_HEADS * QB)),
                  pl.BlockSpec((QB, width), lambda i: (i, 0)),
                  _const_spec(bias.shape)],
        out_specs=pl.BlockSpec((QB, width), lambda i: (i, 0)),
        out_shape=jax.ShapeDtypeStruct((T, width), BF16),
        compiler_params=_cparams(),
        name="swa_attention",
    )(qt, k_hm, k_hm, vt_hm, vt_hm, sink_b, gate, bias)


def _pad_cols(w, width):
    return jnp.pad(w, ((0, 0), (0, width - w.shape[1])))


def _even_layer(x, norm_g, w_in, pool_w, pool_scale, q_gain, k_gain, w_out, rel_bias, dsa_bias):
    pool_width = pool_w.shape[0] * pool_w.shape[1]
    dsa_width = DSA_HEADS * HEAD_DIM
    idx_width = IDX_HEADS * HEAD_DIM
    sizes = (pool_width, pool_width, dsa_width, HEAD_DIM, HEAD_DIM, dsa_width, idx_width, HEAD_DIM, IDX_HEADS)
    offs = [0]
    for s in sizes:
        offs.append(offs[-1] + s)
    col = lambda j: w_in[:, offs[j]:offs[j + 1]]
    w_pool = jnp.concatenate([col(0), col(1)], axis=1).astype(BF16)
    w_q = col(2).astype(BF16)
    w_iq = col(6).astype(BF16)
    w_side = jnp.concatenate([col(5), _pad_cols(col(3), 128), _pad_cols(col(7), 128),
                              _pad_cols(jnp.concatenate([col(4), col(8)], axis=1), 128)], axis=1).astype(BF16)

    q_scale = HEAD_DIM ** -0.5 * LOG2E
    far = rel_bias[NUM_BUCKETS - 1, :DSA_HEADS].astype(F32) * LOG2E
    far_hi = far.astype(BF16)
    far_lo = (far - far_hi.astype(F32)).astype(BF16)
    extra = jnp.zeros((QT_ROWS - HEAD_DIM, DSA_HEADS * QB), BF16)
    extra = extra.at[0].set(jnp.repeat(far_hi, QB)).at[1].set(jnp.repeat(far_lo, QB))
    q_norm = jnp.sqrt(HEAD_DIM * jnp.max(q_gain.astype(F32) ** 2) * q_scale ** 2 + jnp.max(far ** 2))
    k_norm = jnp.sqrt(HEAD_DIM * jnp.max(k_gain.astype(F32) ** 2) + 2.0)
    near_max = jnp.max(jnp.abs(rel_bias[:, :DSA_HEADS] - rel_bias[NUM_BUCKETS - 1, :DSA_HEADS])) * LOG2E
    bound = 1.02 * q_norm * k_norm + near_max
    bounded = (bound < SAFE_LOG2_RANGE).astype(I32).reshape(1)

    py = _pool_mixer(x, norm_g, w_pool, pool_w.astype(BF16), pool_scale)
    qt = _head_t_proj(x, norm_g, w_q, q_gain, normalize=True, scale=q_scale, extra=extra)
    iqt = _head_t_proj(x, norm_g, w_iq, q_gain, normalize=False)
    gate, k, ik, vt, iwt = _dsa_side_proj(x, norm_g, w_side, k_gain, dsa_width,
                                          iw_scale=IDX_HEADS ** -0.5 * HEAD_DIM ** -0.5)
    dy = _dsa_attention(bounded, qt, iqt, iwt, gate, k, ik, vt, dsa_bias)
    w_o = w_out.astype(BF16)
    return _out_proj(x, [(py, w_o[:pool_width]), (dy, w_o[pool_width:])])


def _odd_layer(x, norm_g, w_in, q_gain, k_gain, sinks, w_out, swa_bias):
    q_width = SWA_Q_HEADS * HEAD_DIM
    kv_width = SWA_KV_HEADS * HEAD_DIM
    w_q = w_in[:, :q_width].astype(BF16)
    w_side = jnp.concatenate([w_in[:, q_width + 2 * kv_width:], w_in[:, q_width:q_width + 2 * kv_width]],
                             axis=1).astype(BF16)
    qt = _head_t_proj(x, norm_g, w_q, q_gain, normalize=True, scale=HEAD_DIM ** -0.5)
    gate, k_hm, vt_hm = _swa_side_proj(x, norm_g, w_side, k_gain, q_width)
    go = _swa_attention(qt, k_hm, vt_hm, sinks, gate, swa_bias)
    return _out_proj(x, [(go, w_out.astype(BF16))])


def kernel(x, rel_bias, even_norm, even_w_in, even_pool_w, even_pool_scale, even_q_gain, even_k_gain,
           even_w_out, odd_norm, odd_w_in, odd_q_gain, odd_k_gain, odd_sinks, odd_w_out):
    B, T, D = x.shape
    depth = even_norm.shape[0] + odd_norm.shape[0]
    dsa_bias = _bias_tiles(rel_bias, N_BIAS_TILES, DSA_HEADS, QB, QB, 0, log2_minus_far=True)
    swa_bias = _bias_tiles(rel_bias, 1, SWA_Q_HEADS, 2 * QB, 0, QB)[0]
    outs = []
    for b in range(B):
        h = x[b]
        for layer in range(depth):
            j = layer // 2
            if layer % 2 == 0:
                h = _even_layer(h, even_norm[j], even_w_in[j], even_pool_w[j], even_pool_scale[j],
                                even_q_gain[j], even_k_gain[j], even_w_out[j], rel_bias, dsa_bias)
            else:
                h = _odd_layer(h, odd_norm[j], odd_w_in[j], odd_q_gain[j], odd_k_gain[j],
                               odd_sinks[j], odd_w_out[j], swa_bias)
        outs.append(h)
    return jnp.stack(outs, axis=0)
```

```python
import functools

import jax
import jax.numpy as jnp
from jax import lax
from jax.experimental import pallas as pl
from jax.experimental.pallas import tpu as pltpu

F32 = jnp.float32
BF16 = jnp.bfloat16
I32 = jnp.int32
I16 = jnp.int16

EPS = 1e-6
HEAD_DIM = 64
QB = 128
POOL_WINDOWS = (2, 4, 8, 16)
POOL_HALO = 16
DSA_HEADS = 16
IDX_HEADS = 16
DSA_TOPK = 256
SWA_Q_HEADS = 32
SWA_KV_HEADS = 4
NUM_BUCKETS = 32
IDX_CHUNK = 512
ATT_CHUNK = 256
ATT_PAIR = IDX_CHUNK // ATT_CHUNK
INT_MIN = -(2 ** 31)
HALF_OFFSET = 2 ** 15
COUNT_ACCS = 4
MASKED = -1e30
LOG2E = 1.4426950408889634
QT_ROWS = 128
VT_ROWS = 80
SAFE_LOG2_RANGE = 60.0
TM = 256
VMEM_LIMIT = 56 * 1024 * 1024


def _bucket_starts():
    max_exact = NUM_BUCKETS // 2
    starts = list(range(max_exact + 1))
    n = max_exact
    for b in range(max_exact + 1, NUM_BUCKETS):
        while n ** 16 < max_exact ** 16 * 64 ** (b - max_exact):
            n += 1
        starts.append(n)
    return tuple(starts)


BUCKET_START = _bucket_starts()
FAR_DELTA = -(-(BUCKET_START[-1] + QB - 1) // QB)
N_BIAS_TILES = FAR_DELTA + 1


def _cparams(n_grid=1):
    return pltpu.CompilerParams(dimension_semantics=("arbitrary",) * n_grid,
                                vmem_limit_bytes=VMEM_LIMIT)


def _const_spec(shape):
    return pl.BlockSpec(shape, lambda i: (0,) * len(shape), pipeline_mode=pl.Buffered(1))


def _rms_rows_bf16(x, g):
    ms = jnp.mean(x * x, axis=-1, keepdims=True)
    return (x * lax.rsqrt(ms + EPS) * g).astype(BF16)


def _silu(x):
    return x / (1.0 + jnp.exp(-x))


def _bias_tiles_kernel(rb_ref, out_ref, *, n_heads, rows, base_step, base0, log2_minus_far):
    base = base0 + pl.program_id(0) * base_step
    shape = (rows, QB)
    dist = base + lax.broadcasted_iota(I32, shape, 1) - lax.broadcasted_iota(I32, shape, 0)
    at_least = [dist >= BUCKET_START[b] for b in range(1, NUM_BUCKETS)]
    for h in range(n_heads):
        val = jnp.full(shape, rb_ref[0, h], F32)
        for b in range(1, NUM_BUCKETS):
            val = jnp.where(at_least[b - 1], rb_ref[b, h], val)
        if log2_minus_far:
            val = (val - rb_ref[NUM_BUCKETS - 1, h]) * LOG2E
        out_ref[0, h] = val


def _bias_tiles(rel_bias, n_tiles, n_heads, rows, base_step, base0, log2_minus_far=False):
    kern = functools.partial(_bias_tiles_kernel, n_heads=n_heads, rows=rows, base_step=base_step, base0=base0,
                             log2_minus_far=log2_minus_far)
    return pl.pallas_call(
        kern,
        grid=(n_tiles,),
        in_specs=[pl.BlockSpec(memory_space=pltpu.SMEM)],
        out_specs=pl.BlockSpec((1, n_heads, rows, QB), lambda t: (t, 0, 0, 0)),
        out_shape=jax.ShapeDtypeStruct((n_tiles, n_heads, rows, QB), F32),
        compiler_params=_cparams(),
        name="bias_tiles",
    )(rel_bias)


def _head_t_proj_kernel(x_ref, g_ref, w_ref, gain_ref, *rest, n_heads, normalize, scale):
    out_ref = rest[-1]
    if len(rest) == 2:
        for b in range(TM // QB):
            out_ref[b, HEAD_DIM:, :] = rest[0][...]
    xn = _rms_rows_bf16(x_ref[...], g_ref[...])
    h = jnp.dot(xn, w_ref[...], preferred_element_type=F32)
    gain = gain_ref[...]
    for b in range(TM // QB):
        rows = h[b * QB:(b + 1) * QB]
        for p in range(n_heads // 2):
            pair = rows[:, p * 128:(p + 1) * 128].T
            for hh in range(2):
                t = pair[hh * HEAD_DIM:(hh + 1) * HEAD_DIM]
                if normalize:
                    ms = jnp.mean(t * t, axis=0, keepdims=True)
                    t = t * lax.rsqrt(ms + EPS) * gain * scale
                head = 2 * p + hh
                out_ref[b, :HEAD_DIM, head * QB:(head + 1) * QB] = t.astype(BF16)


def _head_t_proj(x, g, w, gain, *, normalize, scale=1.0, extra=None):
    T, D = x.shape
    n_heads = w.shape[1] // HEAD_DIM
    gain_b = jnp.broadcast_to(gain.reshape(HEAD_DIM, 1), (HEAD_DIM, QB)).astype(F32)
    kern = functools.partial(_head_t_proj_kernel, n_heads=n_heads, normalize=normalize, scale=scale)
    args = [x, g.reshape(1, D), w, gain_b]
    in_specs = [pl.BlockSpec((TM, D), lambda i: (i, 0)),
                _const_spec((1, D)),
                _const_spec(w.shape),
                _const_spec((HEAD_DIM, QB))]
    rows = HEAD_DIM
    if extra is not None:
        args.append(extra)
        in_specs.append(_const_spec(extra.shape))
        rows += extra.shape[0]
    return pl.pallas_call(
        kern,
        grid=(T // TM,),
        in_specs=in_specs,
        out_specs=pl.BlockSpec((TM // QB, rows, n_heads * QB), lambda i: (i, 0, 0)),
        out_shape=jax.ShapeDtypeStruct((T // QB, rows, n_heads * QB), BF16),
        compiler_params=_cparams(),
        name="head_t_proj",
    )(*args)


def _pool_kernel(x_ref, g_ref, w_ref, pw_ref, ps_ref, out_ref, halo_ref):
    i = pl.program_id(0)
    width = out_ref.shape[1]
    gc = width // len(POOL_WINDOWS)
    xn = _rms_rows_bf16(x_ref[...], g_ref[...])
    h = jnp.dot(xn, w_ref[...], preferred_element_type=F32)
    pin = h[:, :width]
    gate = h[:, width:]

    @pl.when(i == 0)
    def _():
        halo_ref[...] = jnp.zeros_like(halo_ref)

    ext = jnp.concatenate([halo_ref[...], pin], axis=0)
    halo_ref[...] = pin[TM - POOL_HALO:]
    pos = i * TM + lax.broadcasted_iota(I32, (TM, gc), 0)
    outs = []
    for grp, win in enumerate(POOL_WINDOWS):
        e = ext[:, grp * gc:(grp + 1) * gc]
        s = e
        span = 1
        while span < win:
            s = s + pltpu.roll(s, span, 0)
            span *= 2
        s = s[POOL_HALO:]
        a = pin[:, grp * gc:(grp + 1) * gc]
        cnt = jnp.minimum(pos + 1, win).astype(F32)
        pooled = s / cnt - a
        y = jnp.dot(pooled.astype(BF16), pw_ref[grp], preferred_element_type=F32)
        outs.append(y * ps_ref[:, grp * gc:(grp + 1) * gc])
    py = jnp.concatenate(outs, axis=1)
    out_ref[...] = (_silu(gate) * py).astype(BF16)


def _pool_mixer(x, g, w, pool_w, pool_scale):
    T, D = x.shape
    width = w.shape[1] // 2
    return pl.pallas_call(
        _pool_kernel,
        grid=(T // TM,),
        in_specs=[pl.BlockSpec((TM, D), lambda i: (i, 0)),
                  _const_spec((1, D)),
                  _const_spec(w.shape),
                  _const_spec(pool_w.shape),
                  _const_spec((1, width))],
        out_specs=pl.BlockSpec((TM, width), lambda i: (i, 0)),
        out_shape=jax.ShapeDtypeStruct((T, width), BF16),
        scratch_shapes=[pltpu.VMEM((POOL_HALO, width), F32)],
        compiler_params=_cparams(),
        name="pool_mixer",
    )(x, g.reshape(1, D), w, pool_w, pool_scale.reshape(1, width))


def _dsa_side_kernel(x_ref, g_ref, w_ref, kg_ref, gate_ref, k_ref, ik_ref, vt_ref, iwt_ref, *, gate_w, iw_scale):
    xn = _rms_rows_bf16(x_ref[...], g_ref[...])
    h = jnp.dot(xn, w_ref[...], preferred_element_type=F32)
    gate_ref[...] = h[:, :gate_w].astype(BF16)
    k = h[:, gate_w:gate_w + 128]
    ms = jnp.sum(k * k, axis=-1, keepdims=True) * (1.0 / HEAD_DIM)
    kn = k * lax.rsqrt(ms + EPS) * kg_ref[...]
    lane = lax.broadcasted_iota(I32, kn.shape, 1)
    k_ref[...] = jnp.where((lane == HEAD_DIM) | (lane == HEAD_DIM + 1), 1.0, kn).astype(BF16)
    ik_ref[...] = h[:, gate_w + 128:gate_w + 128 + HEAD_DIM].astype(BF16)
    vw = h[:, gate_w + 256:gate_w + 384].T
    row = lax.broadcasted_iota(I32, (VT_ROWS - HEAD_DIM, TM), 0)
    vt_ref[0] = jnp.concatenate([vw[:HEAD_DIM], jnp.where(row == 0, 1.0, 0.0)], axis=0).astype(BF16)
    iwt_ref[...] = vw[HEAD_DIM:HEAD_DIM + IDX_HEADS] * iw_scale


def _dsa_side_proj(x, g, w, k_gain, gate_w, iw_scale):
    T, D = x.shape
    assert TM == ATT_CHUNK
    kern = functools.partial(_dsa_side_kernel, gate_w=gate_w, iw_scale=iw_scale)
    return pl.pallas_call(
        kern,
        grid=(T // TM,),
        in_specs=[pl.BlockSpec((TM, D), lambda i: (i, 0)),
                  _const_spec((1, D)),
                  _const_spec(w.shape),
                  _const_spec((1, QT_ROWS))],
        out_specs=[pl.BlockSpec((TM, gate_w), lambda i: (i, 0)),
                   pl.BlockSpec((TM, QT_ROWS), lambda i: (i, 0)),
                   pl.BlockSpec((TM, HEAD_DIM), lambda i: (i, 0)),
                   pl.BlockSpec((1, VT_ROWS, ATT_CHUNK), lambda i: (i, 0, 0)),
                   pl.BlockSpec((IDX_HEADS, TM), lambda i: (0, i))],
        out_shape=[jax.ShapeDtypeStruct((T, gate_w), BF16),
                   jax.ShapeDtypeStruct((T, QT_ROWS), BF16),
                   jax.ShapeDtypeStruct((T, HEAD_DIM), BF16),
                   jax.ShapeDtypeStruct((T // ATT_CHUNK, VT_ROWS, ATT_CHUNK), BF16),
                   jax.ShapeDtypeStruct((IDX_HEADS, T), F32)],
        compiler_params=_cparams(),
        name="dsa_side_proj",
    )(x, g.reshape(1, D), w, _pad_cols(k_gain.reshape(1, HEAD_DIM), QT_ROWS))


def _swa_side_kernel(x_ref, g_ref, w_ref, kg_ref, gate_ref, k_ref, vt_ref, *, gate_w):
    xn = _rms_rows_bf16(x_ref[...], g_ref[...])
    h = jnp.dot(xn, w_ref[...], preferred_element_type=F32)
    gate_ref[...] = h[:, :gate_w].astype(BF16)
    kv_w = SWA_KV_HEADS * HEAD_DIM
    for hd in range(SWA_KV_HEADS):
        k = h[:, gate_w + hd * HEAD_DIM:gate_w + (hd + 1) * HEAD_DIM]
        ms = jnp.mean(k * k, axis=-1, keepdims=True)
        k_ref[hd] = (k * lax.rsqrt(ms + EPS) * kg_ref[...]).astype(BF16)
    for p in range(SWA_KV_HEADS // 2):
        pair = h[:, gate_w + kv_w + p * 128:gate_w + kv_w + (p + 1) * 128].T
        vt_ref[2 * p] = pair[:HEAD_DIM].astype(BF16)
        vt_ref[2 * p + 1] = pair[HEAD_DIM:].astype(BF16)


def _swa_side_proj(x, g, w, k_gain, gate_w):
    T, D = x.shape
    kern = functools.partial(_swa_side_kernel, gate_w=gate_w)
    return pl.pallas_call(
        kern,
        grid=(T // TM,),
        in_specs=[pl.BlockSpec((TM, D), lambda i: (i, 0)),
                  _const_spec((1, D)),
                  _const_spec(w.shape),
                  _const_spec((1, HEAD_DIM))],
        out_specs=[pl.BlockSpec((TM, gate_w), lambda i: (i, 0)),
                   pl.BlockSpec((SWA_KV_HEADS, TM, HEAD_DIM), lambda i: (0, i, 0)),
                   pl.BlockSpec((SWA_KV_HEADS, HEAD_DIM, TM), lambda i: (0, 0, i))],
        out_shape=[jax.ShapeDtypeStruct((T, gate_w), BF16),
                   jax.ShapeDtypeStruct((SWA_KV_HEADS, T, HEAD_DIM), BF16),
                   jax.ShapeDtypeStruct((SWA_KV_HEADS, HEAD_DIM, T), BF16)],
        compiler_params=_cparams(),
        name="swa_side_proj",
    )(x, g.reshape(1, D), w, k_gain.reshape(1, HEAD_DIM))


def _out_proj_kernel(*refs, n_terms):
    x_ref = refs[0]
    out_ref = refs[-1]
    acc = x_ref[...]
    for t in range(n_terms):
        acc = acc + jnp.dot(refs[1 + 2 * t][...], refs[2 + 2 * t][...], preferred_element_type=F32)
    out_ref[...] = acc


def _out_proj(x, terms):
    T, D = x.shape
    args = [x]
    in_specs = [pl.BlockSpec((TM, D), lambda i: (i, 0))]
    for a, w in terms:
        args += [a, w]
        in_specs += [pl.BlockSpec((TM, a.shape[1]), lambda i: (i, 0)), _const_spec(w.shape)]
    return pl.pallas_call(
        functools.partial(_out_proj_kernel, n_terms=len(terms)),
        grid=(T // TM,),
        in_specs=in_specs,
        out_specs=pl.BlockSpec((TM, D), lambda i: (i, 0)),
        out_shape=jax.ShapeDtypeStruct((T, D), F32),
        compiler_params=_cparams(),
        name="out_proj",
    )(*args)


def _store_gated(o_t, gate_ref, out_ref, n_heads):
    for p in range(n_heads // 2):
        pair = jnp.concatenate([o_t[:, (2 * p) * QB:(2 * p + 1) * QB],
                                o_t[:, (2 * p + 1) * QB:(2 * p + 2) * QB]], axis=0)
        g = gate_ref[:, p * 128:(p + 1) * 128].astype(F32)
        out_ref[:, p * 128:(p + 1) * 128] = (_silu(g) * pair.T).astype(BF16)


def _dsa_kernel(bounded_ref, qt_ref, iqt_ref, iwt_ref, gate_ref, k_ref, ik_ref, vt_ref, bias_ref, out_ref,
                keys_ref, hi_ref, lo_ref):
    i = pl.program_id(0)
    n_idx = i // (IDX_CHUNK // QB) + 1
    lanes = DSA_HEADS * QB

    iqt = iqt_ref[0]
    iwt = iwt_ref[...]
    qpos = i * QB + lax.broadcasted_iota(I32, (IDX_CHUNK, QB), 1)
    krow = lax.broadcasted_iota(I32, (IDX_CHUNK, QB), 0)

    def idx_body(c, carry):
        off = pl.multiple_of(c * IDX_CHUNK, IDX_CHUNK)
        s = jnp.dot(ik_ref[pl.ds(off, IDX_CHUNK), :], iqt, preferred_element_type=F32)
        score = jnp.zeros((IDX_CHUNK, QB), F32)
        for h in range(IDX_HEADS):
            score = score + jnp.maximum(s[:, h * QB:(h + 1) * QB], 0.0) * iwt[h:h + 1, :]
        bits = lax.bitcast_convert_type(score, I32)
        key = jnp.where(bits < 0, bits ^ 0x7FFFFFFF, bits)
        key = jnp.where(off + krow <= qpos, key, INT_MIN)
        keys_ref[pl.ds(off, IDX_CHUNK), :] = key
        hi_ref[pl.ds(off, IDX_CHUNK), :] = (key >> 16).astype(I16)
        lo_ref[pl.ds(off, IDX_CHUNK), :] = ((key & 0xFFFF) - HALF_OFFSET).astype(I16)
        return carry

    lax.fori_loop(0, n_idx, idx_body, 0)

    one16 = jnp.ones((16, QB), I16)
    zero16 = jnp.zeros((16, QB), I16)

    def count16(ref, cand, strict=False):
        cand16 = jnp.broadcast_to(cand.astype(I16), (16, QB))

        def body(c, accs):
            off = pl.multiple_of(c * IDX_CHUNK, IDX_CHUNK)
            blk = ref[pl.ds(off, IDX_CHUNK), :]
            accs = list(accs)
            for j in range(IDX_CHUNK // 16):
                rows = blk[j * 16:(j + 1) * 16]
                hit = rows > cand16 if strict else rows >= cand16
                accs[j % COUNT_ACCS] = accs[j % COUNT_ACCS] + jnp.where(hit, one16, zero16)
            return tuple(accs)

        accs = lax.fori_loop(0, n_idx, body, (zero16,) * COUNT_ACCS)
        total = accs[0].astype(I32)
        for a in accs[1:]:
            total = total + a.astype(I32)
        return jnp.sum(total, axis=0, keepdims=True)

    def search16(ref, need):
        zero = jnp.zeros((1, QB), I32)
        base = jnp.where(count16(ref, zero) >= need, zero, -HALF_OFFSET)

        def bit_body(b, base):
            cand = base | jnp.left_shift(jnp.int32(1), 14 - b)
            return jnp.where(count16(ref, cand) >= need, cand, base)

        return lax.fori_loop(0, 15, bit_body, base)

    t_hi = search16(hi_ref, DSA_TOPK)
    need_lo = DSA_TOPK - count16(hi_ref, t_hi, strict=True)
    t_hi16 = jnp.broadcast_to(t_hi.astype(I16), (IDX_CHUNK, QB))

    def bucket_body(c, carry):
        off = pl.multiple_of(c * IDX_CHUNK, IDX_CHUNK)
        in_bucket = hi_ref[pl.ds(off, IDX_CHUNK), :] == t_hi16
        lo_ref[pl.ds(off, IDX_CHUNK), :] = jnp.where(in_bucket, lo_ref[pl.ds(off, IDX_CHUNK), :],
                                                     jnp.int16(-HALF_OFFSET))
        return carry

    lax.fori_loop(0, n_idx, bucket_body, 0)
    t_lo = search16(lo_ref, need_lo)
    base = (t_hi << 16) | (t_lo + HALF_OFFSET)
    thr = jnp.maximum(base, INT_MIN + 1)

    qt = qt_ref[0]
    sub = ATT_CHUNK // QB

    def tiles(c, near, fn):
        off = pl.multiple_of(c * ATT_CHUNK, ATT_CHUNK)
        s = jnp.dot(k_ref[pl.ds(off, ATT_CHUNK), :], qt, preferred_element_type=F32)
        sel = keys_ref[pl.ds(off, ATT_CHUNK), :] >= thr
        blocks = []
        for sb in range(sub):
            delta = jnp.clip(i - (c * sub + sb), 0, FAR_DELTA)
            sel_sb = sel[sb * QB:(sb + 1) * QB]
            row = []
            for h in range(DSA_HEADS):
                t = s[sb * QB:(sb + 1) * QB, h * QB:(h + 1) * QB]
                if near:
                    t = t + bias_ref[delta, h]
                row.append(fn(t, sel_sb))
            blocks.append(jnp.concatenate(row, axis=1))
        return jnp.concatenate(blocks, axis=0)

    def bounded_body(near, pair, acc):
        pvs = []
        for half in range(ATT_PAIR):
            c = pair * ATT_PAIR + half
            p = tiles(c, near, lambda t, sel: jnp.where(sel, jnp.exp2(t), 0.0))
            pvs.append(jnp.dot(vt_ref[c], p.astype(BF16), preferred_element_type=F32))
        return acc + sum(pvs[1:], pvs[0])

    def bounded_path():
        n_far = jnp.maximum(i - (FAR_DELTA - 1), 0) // (sub * ATT_PAIR)
        acc = jnp.zeros((VT_ROWS, lanes), F32)
        acc = lax.fori_loop(0, n_far, functools.partial(bounded_body, False), acc)
        acc = lax.fori_loop(n_far, n_idx, functools.partial(bounded_body, True), acc)
        return acc[:HEAD_DIM] / acc[HEAD_DIM:HEAD_DIM + 1]

    def general_body(c, carry):
        m, l, acc = carry
        t = tiles(c, True, lambda t, sel: jnp.where(sel, t, MASKED))
        m_new = jnp.maximum(m, jnp.max(t, axis=0, keepdims=True))
        alpha = jnp.exp2(m - m_new)
        p = jnp.exp2(t - m_new)
        l_new = alpha * l + jnp.sum(p, axis=0, keepdims=True)
        pv = jnp.dot(vt_ref[c], p.astype(BF16), preferred_element_type=F32)
        return m_new, l_new, acc * alpha + pv

    def general_path():
        m0 = jnp.full((1, lanes), MASKED, F32)
        l0 = jnp.zeros((1, lanes), F32)
        acc0 = jnp.zeros((VT_ROWS, lanes), F32)
        _, l, acc = lax.fori_loop(0, n_idx * ATT_PAIR, general_body, (m0, l0, acc0))
        return acc[:HEAD_DIM] / l

    o_t = lax.cond(bounded_ref[0] > 0, bounded_path, general_path)
    _store_gated(o_t, gate_ref, out_ref, DSA_HEADS)


def _dsa_attention(bounded, qt, iqt, iwt, gate, k, ik, vt, bias):
    nb = qt.shape[0]
    T = k.shape[0]
    width = DSA_HEADS * HEAD_DIM
    return pl.pallas_call(
        _dsa_kernel,
        grid=(nb,),
        in_specs=[pl.BlockSpec(memory_space=pltpu.SMEM),
                  pl.BlockSpec((1, QT_ROWS, DSA_HEADS * QB), lambda i: (i, 0, 0)),
                  pl.BlockSpec((1, HEAD_DIM, IDX_HEADS * QB), lambda i: (i, 0, 0)),
                  pl.BlockSpec((IDX_HEADS, QB), lambda i: (0, i)),
                  pl.BlockSpec((QB, width), lambda i: (i, 0)),
                  _const_spec(k.shape),
                  _const_spec(ik.shape),
                  _const_spec(vt.shape),
                  _const_spec(bias.shape)],
        out_specs=pl.BlockSpec((QB, width), lambda i: (i, 0)),
        out_shape=jax.ShapeDtypeStruct((T, width), BF16),
        scratch_shapes=[pltpu.VMEM((T, QB), I32), pltpu.VMEM((T, QB), I16), pltpu.VMEM((T, QB), I16)],
        compiler_params=_cparams(),
        name="dsa_attention",
    )(bounded, qt, iqt, iwt, gate, k, ik, vt, bias)


def _swa_kernel(qt_ref, kp_ref, kc_ref, vp_ref, vc_ref, sink_ref, gate_ref, bias_ref, out_ref):
    i = pl.program_id(0)
    grp = SWA_Q_HEADS // SWA_KV_HEADS
    shape = (2 * QB, QB)
    c_idx = lax.broadcasted_iota(I32, shape, 0)
    dist = QB + lax.broadcasted_iota(I32, shape, 1) - c_idx
    mask = (dist >= 0) & (dist < QB) & ((c_idx >= QB) | (i > 0))
    outs = []
    for kv in range(SWA_KV_HEADS):
        kcat = jnp.concatenate([kp_ref[kv], kc_ref[kv]], axis=0)
        vcat = jnp.concatenate([vp_ref[kv], vc_ref[kv]], axis=1)
        qg = qt_ref[0, :, kv * grp * QB:(kv + 1) * grp * QB]
        s = jnp.dot(kcat, qg, preferred_element_type=F32)
        cols = []
        for j in range(grp):
            t = s[:, j * QB:(j + 1) * QB] + bias_ref[kv * grp + j]
            cols.append(jnp.where(mask, t, MASKED))
        t = jnp.concatenate(cols, axis=1)
        sink = sink_ref[:, kv * grp * QB:(kv + 1) * grp * QB]
        m = jnp.maximum(jnp.max(t, axis=0, keepdims=True), sink)
        p = jnp.exp(t - m)
        l = jnp.sum(p, axis=0, keepdims=True) + jnp.exp(sink - m)
        pv = jnp.dot(vcat, p.astype(BF16), preferred_element_type=F32)
        outs.append(pv / l)
    _store_gated(jnp.concatenate(outs, axis=1), gate_ref, out_ref, SWA_Q_HEADS)


def _swa_attention(qt, k_hm, vt_hm, sinks, gate, bias):
    nb = qt.shape[0]
    T = k_hm.shape[1]
    width = SWA_Q_HEADS * HEAD_DIM
    sink_b = jnp.repeat(sinks.astype(F32), QB).reshape(1, SWA_Q_HEADS * QB)

    def prev(i):
        return jnp.maximum(i - 1, 0)

    return pl.pallas_call(
        _swa_kernel,
        grid=(nb,),
        in_specs=[pl.BlockSpec((1, HEAD_DIM, SWA_Q_HEADS * QB), lambda i: (i, 0, 0)),
                  pl.BlockSpec((SWA_KV_HEADS, QB, HEAD_DIM), lambda i: (0, prev(i), 0)),
                  pl.BlockSpec((SWA_KV_HEADS, QB, HEAD_DIM), lambda i: (0, i, 0)),
                  pl.BlockSpec((SWA_KV_HEADS, HEAD_DIM, QB), lambda i: (0, 0, prev(i))),
                  pl.BlockSpec((SWA_KV_HEADS, HEAD_DIM, QB), lambda i: (0, 0, i)),
                  _const_spec((1, SWA_Q_HEADS * QB)),
                  pl.BlockSpec((QB, width), lambda i: (i, 0)),
                  _const_spec(bias.shape)],
        out_specs=pl.BlockSpec((QB, width), lambda i: (i, 0)),
        out_shape=jax.ShapeDtypeStruct((T, width), BF16),
        compiler_params=_cparams(),
        name="swa_attention",
    )(qt, k_hm, k_hm, vt_hm, vt_hm, sink_b, gate, bias)


def _pad_cols(w, width):
    return jnp.pad(w, ((0, 0), (0, width - w.shape[1])))


def _even_layer(x, norm_g, w_in, pool_w, pool_scale, q_gain, k_gain, w_out, rel_bias, dsa_bias):
    pool_width = pool_w.shape[0] * pool_w.shape[1]
    dsa_width = DSA_HEADS * HEAD_DIM
    idx_width = IDX_HEADS * HEAD_DIM
    sizes = (pool_width, pool_width, dsa_width, HEAD_DIM, HEAD_DIM, dsa_width, idx_width, HEAD_DIM, IDX_HEADS)
    offs = [0]
    for s in sizes:
        offs.append(offs[-1] + s)
    col = lambda j: w_in[:, offs[j]:offs[j + 1]]
    w_pool = jnp.concatenate([col(0), col(1)], axis=1).astype(BF16)
    w_q = col(2).astype(BF16)
    w_iq = col(6).astype(BF16)
    w_side = jnp.concatenate([col(5), _pad_cols(col(3), 128), _pad_cols(col(7), 128),
                              _pad_cols(jnp.concatenate([col(4), col(8)], axis=1), 128)], axis=1).astype(BF16)

    q_scale = HEAD_DIM ** -0.5 * LOG2E
    far = rel_bias[NUM_BUCKETS - 1, :DSA_HEADS].astype(F32) * LOG2E
    far_hi = far.astype(BF16)
    far_lo = (far - far_hi.astype(F32)).astype(BF16)
    extra = jnp.zeros((QT_ROWS - HEAD_DIM, DSA_HEADS * QB), BF16)
    extra = extra.at[0].set(jnp.repeat(far_hi, QB)).at[1].set(jnp.repeat(far_lo, QB))
    q_norm = jnp.sqrt(HEAD_DIM * jnp.max(q_gain.astype(F32) ** 2) * q_scale ** 2 + jnp.max(far ** 2))
    k_norm = jnp.sqrt(HEAD_DIM * jnp.max(k_gain.astype(F32) ** 2) + 2.0)
    near_max = jnp.max(jnp.abs(rel_bias[:, :DSA_HEADS] - rel_bias[NUM_BUCKETS - 1, :DSA_HEADS])) * LOG2E
    bound = 1.02 * q_norm * k_norm + near_max
    bounded = (bound < SAFE_LOG2_RANGE).astype(I32).reshape(1)

    py = _pool_mixer(x, norm_g, w_pool, pool_w.astype(BF16), pool_scale)
    qt = _head_t_proj(x, norm_g, w_q, q_gain, normalize=True, scale=q_scale, extra=extra)
    iqt = _head_t_proj(x, norm_g, w_iq, q_gain, normalize=False)
    gate, k, ik, vt, iwt = _dsa_side_proj(x, norm_g, w_side, k_gain, dsa_width,
                                          iw_scale=IDX_HEADS ** -0.5 * HEAD_DIM ** -0.5)
    dy = _dsa_attention(bounded, qt, iqt, iwt, gate, k, ik, vt, dsa_bias)
    w_o = w_out.astype(BF16)
    return _out_proj(x, [(py, w_o[:pool_width]), (dy, w_o[pool_width:])])


def _odd_layer(x, norm_g, w_in, q_gain, k_gain, sinks, w_out, swa_bias):
    q_width = SWA_Q_HEADS * HEAD_DIM
    kv_width = SWA_KV_HEADS * HEAD_DIM
    w_q = w_in[:, :q_width].astype(BF16)
    w_side = jnp.concatenate([w_in[:, q_width + 2 * kv_width:], w_in[:, q_width:q_width + 2 * kv_width]],
                             axis=1).astype(BF16)
    qt = _head_t_proj(x, norm_g, w_q, q_gain, normalize=True, scale=HEAD_DIM ** -0.5)
    gate, k_hm, vt_hm = _swa_side_proj(x, norm_g, w_side, k_gain, q_width)
    go = _swa_attention(qt, k_hm, vt_hm, sinks, gate, swa_bias)
    return _out_proj(x, [(go, w_out.astype(BF16))])


def kernel(x, rel_bias, even_norm, even_w_in, even_pool_w, even_pool_scale, even_q_gain, even_k_gain,
           even_w_out, odd_norm, odd_w_in, odd_q_gain, odd_k_gain, odd_sinks, odd_w_out):
    B, T, D = x.shape
    depth = even_norm.shape[0] + odd_norm.shape[0]
    dsa_bias = _bias_tiles(rel_bias, N_BIAS_TILES, DSA_HEADS, QB, QB, 0, log2_minus_far=True)
    swa_bias = _bias_tiles(rel_bias, 1, SWA_Q_HEADS, 2 * QB, 0, QB)[0]
    outs = []
    for b in range(B):
        h = x[b]
        for layer in range(depth):
            j = layer // 2
            if layer % 2 == 0:
                h = _even_layer(h, even_norm[j], even_w_in[j], even_pool_w[j], even_pool_scale[j],
                                even_q_gain[j], even_k_gain[j], even_w_out[j], rel_bias, dsa_bias)
            else:
                h = _odd_layer(h, odd_norm[j], odd_w_in[j], odd_q_gain[j], odd_k_gain[j],
                               odd_sinks[j], odd_w_out[j], swa_bias)
        outs.append(h)
    return jnp.stack(outs, axis=0)
```

```python
import functools

import jax
import jax.numpy as jnp
from jax import lax
from jax.experimental import pallas as pl
from jax.experimental.pallas import tpu as pltpu

F32 = jnp.float32
BF16 = jnp.bfloat16
I32 = jnp.int32

EPS = 1e-6
HEAD_DIM = 64
QB = 128
POOL_WINDOWS = (2, 4, 8, 16)
POOL_HALO = 16
DSA_HEADS = 16
IDX_HEADS = 16
DSA_TOPK = 256
SWA_Q_HEADS = 32
SWA_KV_HEADS = 4
NUM_BUCKETS = 32
IDX_CHUNK = 512
ATT_CHUNK = 256
ATT_PAIR = IDX_CHUNK // ATT_CHUNK
INT_MIN = -(2 ** 31)
FINITE_MIN_KEY = -0x7F800000
MASKED = -1e30
LOG2E = 1.4426950408889634
QT_ROWS = 128
VT_ROWS = 80
SAFE_LOG2_RANGE = 60.0
TM = 256
VMEM_LIMIT = 56 * 1024 * 1024


def _bucket_starts():
    max_exact = NUM_BUCKETS // 2
    starts = list(range(max_exact + 1))
    n = max_exact
    for b in range(max_exact + 1, NUM_BUCKETS):
        while n ** 16 < max_exact ** 16 * 64 ** (b - max_exact):
            n += 1
        starts.append(n)
    return tuple(starts)


BUCKET_START = _bucket_starts()
FAR_DELTA = -(-(BUCKET_START[-1] + QB - 1) // QB)
N_BIAS_TILES = FAR_DELTA + 1


def _cparams(n_grid=1):
    return pltpu.CompilerParams(dimension_semantics=("arbitrary",) * n_grid,
                                vmem_limit_bytes=VMEM_LIMIT)


def _const_spec(shape):
    return pl.BlockSpec(shape, lambda i: (0,) * len(shape), pipeline_mode=pl.Buffered(1))


def _rms_rows_bf16(x, g):
    ms = jnp.mean(x * x, axis=-1, keepdims=True)
    return (x * lax.rsqrt(ms + EPS) * g).astype(BF16)


def _silu(x):
    return x / (1.0 + jnp.exp(-x))


def _bias_tiles_kernel(rb_ref, out_ref, *, n_heads, rows, base_step, base0, log2_minus_far):
    base = base0 + pl.program_id(0) * base_step
    shape = (rows, QB)
    dist = base + lax.broadcasted_iota(I32, shape, 1) - lax.broadcasted_iota(I32, shape, 0)
    at_least = [dist >= BUCKET_START[b] for b in range(1, NUM_BUCKETS)]
    for h in range(n_heads):
        val = jnp.full(shape, rb_ref[0, h], F32)
        for b in range(1, NUM_BUCKETS):
            val = jnp.where(at_least[b - 1], rb_ref[b, h], val)
        if log2_minus_far:
            val = (val - rb_ref[NUM_BUCKETS - 1, h]) * LOG2E
        out_ref[0, h] = val


def _bias_tiles(rel_bias, n_tiles, n_heads, rows, base_step, base0, log2_minus_far=False):
    kern = functools.partial(_bias_tiles_kernel, n_heads=n_heads, rows=rows, base_step=base_step, base0=base0,
                             log2_minus_far=log2_minus_far)
    return pl.pallas_call(
        kern,
        grid=(n_tiles,),
        in_specs=[pl.BlockSpec(memory_space=pltpu.SMEM)],
        out_specs=pl.BlockSpec((1, n_heads, rows, QB), lambda t: (t, 0, 0, 0)),
        out_shape=jax.ShapeDtypeStruct((n_tiles, n_heads, rows, QB), F32),
        compiler_params=_cparams(),
        name="bias_tiles",
    )(rel_bias)


def _head_t_proj_kernel(x_ref, g_ref, w_ref, gain_ref, *rest, n_heads, normalize, scale):
    out_ref = rest[-1]
    if len(rest) == 2:
        for b in range(TM // QB):
            out_ref[b, HEAD_DIM:, :] = rest[0][...]
    xn = _rms_rows_bf16(x_ref[...], g_ref[...])
    h = jnp.dot(xn, w_ref[...], preferred_element_type=F32)
    gain = gain_ref[...]
    for b in range(TM // QB):
        rows = h[b * QB:(b + 1) * QB]
        for p in range(n_heads // 2):
            pair = rows[:, p * 128:(p + 1) * 128].T
            for hh in range(2):
                t = pair[hh * HEAD_DIM:(hh + 1) * HEAD_DIM]
                if normalize:
                    ms = jnp.mean(t * t, axis=0, keepdims=True)
                    t = t * lax.rsqrt(ms + EPS) * gain * scale
                head = 2 * p + hh
                out_ref[b, :HEAD_DIM, head * QB:(head + 1) * QB] = t.astype(BF16)


def _head_t_proj(x, g, w, gain, *, normalize, scale=1.0, extra=None):
    T, D = x.shape
    n_heads = w.shape[1] // HEAD_DIM
    gain_b = jnp.broadcast_to(gain.reshape(HEAD_DIM, 1), (HEAD_DIM, QB)).astype(F32)
    kern = functools.partial(_head_t_proj_kernel, n_heads=n_heads, normalize=normalize, scale=scale)
    args = [x, g.reshape(1, D), w, gain_b]
    in_specs = [pl.BlockSpec((TM, D), lambda i: (i, 0)),
                _const_spec((1, D)),
                _const_spec(w.shape),
                _const_spec((HEAD_DIM, QB))]
    rows = HEAD_DIM
    if extra is not None:
        args.append(extra)
        in_specs.append(_const_spec(extra.shape))
        rows += extra.shape[0]
    return pl.pallas_call(
        kern,
        grid=(T // TM,),
        in_specs=in_specs,
        out_specs=pl.BlockSpec((TM // QB, rows, n_heads * QB), lambda i: (i, 0, 0)),
        out_shape=jax.ShapeDtypeStruct((T // QB, rows, n_heads * QB), BF16),
        compiler_params=_cparams(),
        name="head_t_proj",
    )(*args)


def _pool_kernel(x_ref, g_ref, w_ref, pw_ref, ps_ref, out_ref, halo_ref):
    i = pl.program_id(0)
    width = out_ref.shape[1]
    gc = width // len(POOL_WINDOWS)
    xn = _rms_rows_bf16(x_ref[...], g_ref[...])
    h = jnp.dot(xn, w_ref[...], preferred_element_type=F32)
    pin = h[:, :width]
    gate = h[:, width:]

    @pl.when(i == 0)
    def _():
        halo_ref[...] = jnp.zeros_like(halo_ref)

    ext = jnp.concatenate([halo_ref[...], pin], axis=0)
    halo_ref[...] = pin[TM - POOL_HALO:]
    pos = i * TM + lax.broadcasted_iota(I32, (TM, gc), 0)
    outs = []
    for grp, win in enumerate(POOL_WINDOWS):
        e = ext[:, grp * gc:(grp + 1) * gc]
        s = e
        span = 1
        while span < win:
            s = s + pltpu.roll(s, span, 0)
            span *= 2
        s = s[POOL_HALO:]
        a = pin[:, grp * gc:(grp + 1) * gc]
        cnt = jnp.minimum(pos + 1, win).astype(F32)
        pooled = s / cnt - a
        y = jnp.dot(pooled.astype(BF16), pw_ref[grp], preferred_element_type=F32)
        outs.append(y * ps_ref[:, grp * gc:(grp + 1) * gc])
    py = jnp.concatenate(outs, axis=1)
    out_ref[...] = (_silu(gate) * py).astype(BF16)


def _pool_mixer(x, g, w, pool_w, pool_scale):
    T, D = x.shape
    width = w.shape[1] // 2
    return pl.pallas_call(
        _pool_kernel,
        grid=(T // TM,),
        in_specs=[pl.BlockSpec((TM, D), lambda i: (i, 0)),
                  _const_spec((1, D)),
                  _const_spec(w.shape),
                  _const_spec(pool_w.shape),
                  _const_spec((1, width))],
        out_specs=pl.BlockSpec((TM, width), lambda i: (i, 0)),
        out_shape=jax.ShapeDtypeStruct((T, width), BF16),
        scratch_shapes=[pltpu.VMEM((POOL_HALO, width), F32)],
        compiler_params=_cparams(),
        name="pool_mixer",
    )(x, g.reshape(1, D), w, pool_w, pool_scale.reshape(1, width))


def _dsa_side_kernel(x_ref, g_ref, w_ref, kg_ref, gate_ref, k_ref, ik_ref, vt_ref, iwt_ref, *, gate_w, iw_scale):
    xn = _rms_rows_bf16(x_ref[...], g_ref[...])
    h = jnp.dot(xn, w_ref[...], preferred_element_type=F32)
    gate_ref[...] = h[:, :gate_w].astype(BF16)
    k = h[:, gate_w:gate_w + 128]
    ms = jnp.sum(k * k, axis=-1, keepdims=True) * (1.0 / HEAD_DIM)
    kn = k * lax.rsqrt(ms + EPS) * kg_ref[...]
    lane = lax.broadcasted_iota(I32, kn.shape, 1)
    k_ref[...] = jnp.where((lane == HEAD_DIM) | (lane == HEAD_DIM + 1), 1.0, kn).astype(BF16)
    ik_ref[...] = h[:, gate_w + 128:gate_w + 128 + HEAD_DIM].astype(BF16)
    vw = h[:, gate_w + 256:gate_w + 384].T
    row = lax.broadcasted_iota(I32, (VT_ROWS - HEAD_DIM, TM), 0)
    vt_ref[0] = jnp.concatenate([vw[:HEAD_DIM], jnp.where(row == 0, 1.0, 0.0)], axis=0).astype(BF16)
    iwt_ref[...] = vw[HEAD_DIM:HEAD_DIM + IDX_HEADS] * iw_scale


def _dsa_side_proj(x, g, w, k_gain, gate_w, iw_scale):
    T, D = x.shape
    assert TM == ATT_CHUNK
    kern = functools.partial(_dsa_side_kernel, gate_w=gate_w, iw_scale=iw_scale)
    return pl.pallas_call(
        kern,
        grid=(T // TM,),
        in_specs=[pl.BlockSpec((TM, D), lambda i: (i, 0)),
                  _const_spec((1, D)),
                  _const_spec(w.shape),
                  _const_spec((1, QT_ROWS))],
        out_specs=[pl.BlockSpec((TM, gate_w), lambda i: (i, 0)),
                   pl.BlockSpec((TM, QT_ROWS), lambda i: (i, 0)),
                   pl.BlockSpec((TM, HEAD_DIM), lambda i: (i, 0)),
                   pl.BlockSpec((1, VT_ROWS, ATT_CHUNK), lambda i: (i, 0, 0)),
                   pl.BlockSpec((IDX_HEADS, TM), lambda i: (0, i))],
        out_shape=[jax.ShapeDtypeStruct((T, gate_w), BF16),
                   jax.ShapeDtypeStruct((T, QT_ROWS), BF16),
                   jax.ShapeDtypeStruct((T, HEAD_DIM), BF16),
                   jax.ShapeDtypeStruct((T // ATT_CHUNK, VT_ROWS, ATT_CHUNK), BF16),
                   jax.ShapeDtypeStruct((IDX_HEADS, T), F32)],
        compiler_params=_cparams(),
        name="dsa_side_proj",
    )(x, g.reshape(1, D), w, _pad_cols(k_gain.reshape(1, HEAD_DIM), QT_ROWS))


def _swa_side_kernel(x_ref, g_ref, w_ref, kg_ref, gate_ref, k_ref, vt_ref, *, gate_w):
    xn = _rms_rows_bf16(x_ref[...], g_ref[...])
    h = jnp.dot(xn, w_ref[...], preferred_element_type=F32)
    gate_ref[...] = h[:, :gate_w].astype(BF16)
    kv_w = SWA_KV_HEADS * HEAD_DIM
    for hd in range(SWA_KV_HEADS):
        k = h[:, gate_w + hd * HEAD_DIM:gate_w + (hd + 1) * HEAD_DIM]
        ms = jnp.mean(k * k, axis=-1, keepdims=True)
        k_ref[hd] = (k * lax.rsqrt(ms + EPS) * kg_ref[...]).astype(BF16)
    for p in range(SWA_KV_HEADS // 2):
        pair = h[:, gate_w + kv_w + p * 128:gate_w + kv_w + (p + 1) * 128].T
        vt_ref[2 * p] = pair[:HEAD_DIM].astype(BF16)
        vt_ref[2 * p + 1] = pair[HEAD_DIM:].astype(BF16)


def _swa_side_proj(x, g, w, k_gain, gate_w):
    T, D = x.shape
    kern = functools.partial(_swa_side_kernel, gate_w=gate_w)
    return pl.pallas_call(
        kern,
        grid=(T // TM,),
        in_specs=[pl.BlockSpec((TM, D), lambda i: (i, 0)),
                  _const_spec((1, D)),
                  _const_spec(w.shape),
                  _const_spec((1, HEAD_DIM))],
        out_specs=[pl.BlockSpec((TM, gate_w), lambda i: (i, 0)),
                   pl.BlockSpec((SWA_KV_HEADS, TM, HEAD_DIM), lambda i: (0, i, 0)),
                   pl.BlockSpec((SWA_KV_HEADS, HEAD_DIM, TM), lambda i: (0, 0, i))],
        out_shape=[jax.ShapeDtypeStruct((T, gate_w), BF16),
                   jax.ShapeDtypeStruct((SWA_KV_HEADS, T, HEAD_DIM), BF16),
                   jax.ShapeDtypeStruct((SWA_KV_HEADS, HEAD_DIM, T), BF16)],
        compiler_params=_cparams(),
        name="swa_side_proj",
    )(x, g.reshape(1, D), w, k_gain.reshape(1, HEAD_DIM))


def _out_proj_kernel(*refs, n_terms):
    x_ref = refs[0]
    out_ref = refs[-1]
    acc = x_ref[...]
    for t in range(n_terms):
        acc = acc + jnp.dot(refs[1 + 2 * t][...], refs[2 + 2 * t][...], preferred_element_type=F32)
    out_ref[...] = acc


def _out_proj(x, terms):
    T, D = x.shape
    args = [x]
    in_specs = [pl.BlockSpec((TM, D), lambda i: (i, 0))]
    for a, w in terms:
        args += [a, w]
        in_specs += [pl.BlockSpec((TM, a.shape[1]), lambda i: (i, 0)), _const_spec(w.shape)]
    return pl.pallas_call(
        functools.partial(_out_proj_kernel, n_terms=len(terms)),
        grid=(T // TM,),
        in_specs=in_specs,
        out_specs=pl.BlockSpec((TM, D), lambda i: (i, 0)),
        out_shape=jax.ShapeDtypeStruct((T, D), F32),
        compiler_params=_cparams(),
        name="out_proj",
    )(*args)


def _store_gated(o_t, gate_ref, out_ref, n_heads):
    for p in range(n_heads // 2):
        pair = jnp.concatenate([o_t[:, (2 * p) * QB:(2 * p + 1) * QB],
                                o_t[:, (2 * p + 1) * QB:(2 * p + 2) * QB]], axis=0)
        g = gate_ref[:, p * 128:(p + 1) * 128].astype(F32)
        out_ref[:, p * 128:(p + 1) * 128] = (_silu(g) * pair.T).astype(BF16)


def _dsa_kernel(bounded_ref, qt_ref, iqt_ref, iwt_ref, gate_ref, k_ref, ik_ref, vt_ref, bias_ref, out_ref,
                score_ref):
    i = pl.program_id(0)
    n_idx = i // (IDX_CHUNK // QB) + 1
    lanes = DSA_HEADS * QB

    iqt = iqt_ref[0]
    iwt = iwt_ref[...]
    qpos = i * QB + lax.broadcasted_iota(I32, (IDX_CHUNK, QB), 1)
    krow = lax.broadcasted_iota(I32, (IDX_CHUNK, QB), 0)

    def idx_body(c, carry):
        off = pl.multiple_of(c * IDX_CHUNK, IDX_CHUNK)
        s = jnp.dot(ik_ref[pl.ds(off, IDX_CHUNK), :], iqt, preferred_element_type=F32)
        score = jnp.zeros((IDX_CHUNK, QB), F32)
        for h in range(IDX_HEADS):
            score = score + jnp.maximum(s[:, h * QB:(h + 1) * QB], 0.0) * iwt[h:h + 1, :]
        score_ref[pl.ds(off, IDX_CHUNK), :] = jnp.where(off + krow <= qpos, score, -jnp.inf)
        return carry

    lax.fori_loop(0, n_idx, idx_body, 0)

    def key_to_float(key):
        return lax.bitcast_convert_type(jnp.where(key < 0, key ^ 0x7FFFFFFF, key), F32)

    def count_ge(key):
        cand = key_to_float(key)

        def body(c, acc):
            off = pl.multiple_of(c * IDX_CHUNK, IDX_CHUNK)
            hit = jnp.where(score_ref[pl.ds(off, IDX_CHUNK), :] >= cand, 1, 0).astype(I32)
            return acc + jnp.sum(hit.reshape(IDX_CHUNK // 8, 8, QB), axis=0)

        acc = lax.fori_loop(0, n_idx, body, jnp.zeros((8, QB), I32))
        return jnp.sum(acc, axis=0, keepdims=True)

    zero = jnp.zeros((1, QB), I32)
    base = jnp.where(count_ge(zero) >= DSA_TOPK, zero, INT_MIN)

    def bit_body(b, base):
        cand = base | jnp.left_shift(jnp.int32(1), 30 - b)
        return jnp.where(count_ge(cand) >= DSA_TOPK, cand, base)

    base = lax.fori_loop(0, 31, bit_body, base)
    thr = key_to_float(jnp.maximum(base, FINITE_MIN_KEY))

    qt = qt_ref[0]
    sub = ATT_CHUNK // QB

    def tiles(c, near, fn):
        off = pl.multiple_of(c * ATT_CHUNK, ATT_CHUNK)
        s = jnp.dot(k_ref[pl.ds(off, ATT_CHUNK), :], qt, preferred_element_type=F32)
        sel = score_ref[pl.ds(off, ATT_CHUNK), :] >= thr
        blocks = []
        for sb in range(sub):
            delta = jnp.clip(i - (c * sub + sb), 0, FAR_DELTA)
            sel_sb = sel[sb * QB:(sb + 1) * QB]
            row = []
            for h in range(DSA_HEADS):
                t = s[sb * QB:(sb + 1) * QB, h * QB:(h + 1) * QB]
                if near:
                    t = t + bias_ref[delta, h]
                row.append(fn(t, sel_sb))
            blocks.append(jnp.concatenate(row, axis=1))
        return jnp.concatenate(blocks, axis=0)

    def bounded_body(near, pair, acc):
        pvs = []
        for half in range(ATT_PAIR):
            c = pair * ATT_PAIR + half
            p = tiles(c, near, lambda t, sel: jnp.where(sel, jnp.exp2(t), 0.0))
            pvs.append(jnp.dot(vt_ref[c], p.astype(BF16), preferred_element_type=F32))
        return acc + sum(pvs[1:], pvs[0])

    def bounded_path():
        n_far = jnp.maximum(i - (FAR_DELTA - 1), 0) // (sub * ATT_PAIR)
        acc = jnp.zeros((VT_ROWS, lanes), F32)
        acc = lax.fori_loop(0, n_far, functools.partial(bounded_body, False), acc)
        acc = lax.fori_loop(n_far, n_idx, functools.partial(bounded_body, True), acc)
        return acc[:HEAD_DIM] / acc[HEAD_DIM:HEAD_DIM + 1]

    def general_body(c, carry):
        m, l, acc = carry
        t = tiles(c, True, lambda t, sel: jnp.where(sel, t, MASKED))
        m_new = jnp.maximum(m, jnp.max(t, axis=0, keepdims=True))
        alpha = jnp.exp2(m - m_new)
        p = jnp.exp2(t - m_new)
        l_new = alpha * l + jnp.sum(p, axis=0, keepdims=True)
        pv = jnp.dot(vt_ref[c], p.astype(BF16), preferred_element_type=F32)
        return m_new, l_new, acc * alpha + pv

    def general_path():
        m0 = jnp.full((1, lanes), MASKED, F32)
        l0 = jnp.zeros((1, lanes), F32)
        acc0 = jnp.zeros((VT_ROWS, lanes), F32)
        _, l, acc = lax.fori_loop(0, n_idx * ATT_PAIR, general_body, (m0, l0, acc0))
        return acc[:HEAD_DIM] / l

    o_t = lax.cond(bounded_ref[0] > 0, bounded_path, general_path)
    _store_gated(o_t, gate_ref, out_ref, DSA_HEADS)


def _dsa_attention(bounded, qt, iqt, iwt, gate, k, ik, vt, bias):
    nb = qt.shape[0]
    T = k.shape[0]
    width = DSA_HEADS * HEAD_DIM
    return pl.pallas_call(
        _dsa_kernel,
        grid=(nb,),
        in_specs=[pl.BlockSpec(memory_space=pltpu.SMEM),
                  pl.BlockSpec((1, QT_ROWS, DSA_HEADS * QB), lambda i: (i, 0, 0)),
                  pl.BlockSpec((1, HEAD_DIM, IDX_HEADS * QB), lambda i: (i, 0, 0)),
                  pl.BlockSpec((IDX_HEADS, QB), lambda i: (0, i)),
                  pl.BlockSpec((QB, width), lambda i: (i, 0)),
                  _const_spec(k.shape),
                  _const_spec(ik.shape),
                  _const_spec(vt.shape),
                  _const_spec(bias.shape)],
        out_specs=pl.BlockSpec((QB, width), lambda i: (i, 0)),
        out_shape=jax.ShapeDtypeStruct((T, width), BF16),
        scratch_shapes=[pltpu.VMEM((T, QB), F32)],
        compiler_params=_cparams(),
        name="dsa_attention",
    )(bounded, qt, iqt, iwt, gate, k, ik, vt, bias)


def _swa_kernel(qt_ref, kp_ref, kc_ref, vp_ref, vc_ref, sink_ref, gate_ref, bias_ref, out_ref):
    i = pl.program_id(0)
    grp = SWA_Q_HEADS // SWA_KV_HEADS
    shape = (2 * QB, QB)
    c_idx = lax.broadcasted_iota(I32, shape, 0)
    dist = QB + lax.broadcasted_iota(I32, shape, 1) - c_idx
    mask = (dist >= 0) & (dist < QB) & ((c_idx >= QB) | (i > 0))
    outs = []
    for kv in range(SWA_KV_HEADS):
        kcat = jnp.concatenate([kp_ref[kv], kc_ref[kv]], axis=0)
        vcat = jnp.concatenate([vp_ref[kv], vc_ref[kv]], axis=1)
        qg = qt_ref[0, :, kv * grp * QB:(kv + 1) * grp * QB]
        s = jnp.dot(kcat, qg, preferred_element_type=F32)
        cols = []
        for j in range(grp):
            t = s[:, j * QB:(j + 1) * QB] + bias_ref[kv * grp + j]
            cols.append(jnp.where(mask, t, MASKED))
        t = jnp.concatenate(cols, axis=1)
        sink = sink_ref[:, kv * grp * QB:(kv + 1) * grp * QB]
        m = jnp.maximum(jnp.max(t, axis=0, keepdims=True), sink)
        p = jnp.exp(t - m)
        l = jnp.sum(p, axis=0, keepdims=True) + jnp.exp(sink - m)
        pv = jnp.dot(vcat, p.astype(BF16), preferred_element_type=F32)
        outs.append(pv / l)
    _store_gated(jnp.concatenate(outs, axis=1), gate_ref, out_ref, SWA_Q_HEADS)


def _swa_attention(qt, k_hm, vt_hm, sinks, gate, bias):
    nb = qt.shape[0]
    T = k_hm.shape[1]
    width = SWA_Q_HEADS * HEAD_DIM
    sink_b = jnp.repeat(sinks.astype(F32), QB).reshape(1, SWA_Q_HEADS * QB)

    def prev(i):
        return jnp.maximum(i - 1, 0)

    return pl.pallas_call(
        _swa_kernel,
        grid=(nb,),
        in_specs=[pl.BlockSpec((1, HEAD_DIM, SWA_Q_HEADS * QB), lambda i: (i, 0, 0)),
                  pl.BlockSpec((SWA_KV_HEADS, QB, HEAD_DIM), lambda i: (0, prev(i), 0)),
                  pl.BlockSpec((SWA_KV_HEADS, QB, HEAD_DIM), lambda i: (0, i, 0)),
                  pl.BlockSpec((SWA_KV_HEADS, HEAD_DIM, QB), lambda i: (0, 0, prev(i))),
                  pl.BlockSpec((SWA_KV_HEADS, HEAD_DIM, QB), lambda i: (0, 0, i)),
                  _const_spec((1, SWA_Q_HEADS * QB)),
                  pl.BlockSpec((QB, width), lambda i: (i, 0)),
                  _const_spec(bias.shape)],
        out_specs=pl.BlockSpec((QB, width), lambda i: (i, 0)),
        out_shape=jax.ShapeDtypeStruct((T, width), BF16),
        compiler_params=_cparams(),
        name="swa_attention",
    )(qt, k_hm, k_hm, vt_hm, vt_hm, sink_b, gate, bias)


def _pad_cols(w, width):
    return jnp.pad(w, ((0, 0), (0, width - w.shape[1])))


def _even_layer(x, norm_g, w_in, pool_w, pool_scale, q_gain, k_gain, w_out, rel_bias, dsa_bias):
    pool_width = pool_w.shape[0] * pool_w.shape[1]
    dsa_width = DSA_HEADS * HEAD_DIM
    idx_width = IDX_HEADS * HEAD_DIM
    sizes = (pool_width, pool_width, dsa_width, HEAD_DIM, HEAD_DIM, dsa_width, idx_width, HEAD_DIM, IDX_HEADS)
    offs = [0]
    for s in sizes:
        offs.append(offs[-1] + s)
    col = lambda j: w_in[:, offs[j]:offs[j + 1]]
    w_pool = jnp.concatenate([col(0), col(1)], axis=1).astype(BF16)
    w_q = col(2).astype(BF16)
    w_iq = col(6).astype(BF16)
    w_side = jnp.concatenate([col(5), _pad_cols(col(3), 128), _pad_cols(col(7), 128),
                              _pad_cols(jnp.concatenate([col(4), col(8)], axis=1), 128)], axis=1).astype(BF16)

    q_scale = HEAD_DIM ** -0.5 * LOG2E
    far = rel_bias[NUM_BUCKETS - 1, :DSA_HEADS].astype(F32) * LOG2E
    far_hi = far.astype(BF16)
    far_lo = (far - far_hi.astype(F32)).astype(BF16)
    extra = jnp.zeros((QT_ROWS - HEAD_DIM, DSA_HEADS * QB), BF16)
    extra = extra.at[0].set(jnp.repeat(far_hi, QB)).at[1].set(jnp.repeat(far_lo, QB))
    q_norm = jnp.sqrt(HEAD_DIM * jnp.max(q_gain.astype(F32) ** 2) * q_scale ** 2 + jnp.max(far ** 2))
    k_norm = jnp.sqrt(HEAD_DIM * jnp.max(k_gain.astype(F32) ** 2) + 2.0)
    near_max = jnp.max(jnp.abs(rel_bias[:, :DSA_HEADS] - rel_bias[NUM_BUCKETS - 1, :DSA_HEADS])) * LOG2E
    bound = 1.02 * q_norm * k_norm + near_max
    bounded = (bound < SAFE_LOG2_RANGE).astype(I32).reshape(1)

    py = _pool_mixer(x, norm_g, w_pool, pool_w.astype(BF16), pool_scale)
    qt = _head_t_proj(x, norm_g, w_q, q_gain, normalize=True, scale=q_scale, extra=extra)
    iqt = _head_t_proj(x, norm_g, w_iq, q_gain, normalize=False)
    gate, k, ik, vt, iwt = _dsa_side_proj(x, norm_g, w_side, k_gain, dsa_width,
                                          iw_scale=IDX_HEADS ** -0.5 * HEAD_DIM ** -0.5)
    dy = _dsa_attention(bounded, qt, iqt, iwt, gate, k, ik, vt, dsa_bias)
    w_o = w_out.astype(BF16)
    return _out_proj(x, [(py, w_o[:pool_width]), (dy, w_o[pool_width:])])


def _odd_layer(x, norm_g, w_in, q_gain, k_gain, sinks, w_out, swa_bias):
    q_width = SWA_Q_HEADS * HEAD_DIM
    kv_width = SWA_KV_HEADS * HEAD_DIM
    w_q = w_in[:, :q_width].astype(BF16)
    w_side = jnp.concatenate([w_in[:, q_width + 2 * kv_width:], w_in[:, q_width:q_width + 2 * kv_width]],
                             axis=1).astype(BF16)
    qt = _head_t_proj(x, norm_g, w_q, q_gain, normalize=True, scale=HEAD_DIM ** -0.5)
    gate, k_hm, vt_hm = _swa_side_proj(x, norm_g, w_side, k_gain, q_width)
    go = _swa_attention(qt, k_hm, vt_hm, sinks, gate, swa_bias)
    return _out_proj(x, [(go, w_out.astype(BF16))])


def kernel(x, rel_bias, even_norm, even_w_in, even_pool_w, even_pool_scale, even_q_gain, even_k_gain,
           even_w_out, odd_norm, odd_w_in, odd_q_gain, odd_k_gain, odd_sinks, odd_w_out):
    B, T, D = x.shape
    depth = even_norm.shape[0] + odd_norm.shape[0]
    dsa_bias = _bias_tiles(rel_bias, N_BIAS_TILES, DSA_HEADS, QB, QB, 0, log2_minus_far=True)
    swa_bias = _bias_tiles(rel_bias, 1, SWA_Q_HEADS, 2 * QB, 0, QB)[0]
    outs = []
    for b in range(B):
        h = x[b]
        for layer in range(depth):
            j = layer // 2
            if layer % 2 == 0:
                h = _even_layer(h, even_norm[j], even_w_in[j], even_pool_w[j], even_pool_scale[j],
                                even_q_gain[j], even_k_gain[j], even_w_out[j], rel_bias, dsa_bias)
            else:
                h = _odd_layer(h, odd_norm[j], odd_w_in[j], odd_q_gain[j], odd_k_gain[j],
                               odd_sinks[j], odd_w_out[j], swa_bias)
        outs.append(h)
    return jnp.stack(outs, axis=0)
```

```python
import functools

import jax
import jax.numpy as jnp
from jax import lax
from jax.experimental import pallas as pl
from jax.experimental.pallas import tpu as pltpu

F32 = jnp.float32
BF16 = jnp.bfloat16
I32 = jnp.int32

EPS = 1e-6
HEAD_DIM = 64
QB = 128
POOL_WINDOWS = (2, 4, 8, 16)
POOL_HALO = 16
DSA_HEADS = 16
IDX_HEADS = 16
DSA_TOPK = 256
SWA_Q_HEADS = 32
SWA_KV_HEADS = 4
NUM_BUCKETS = 32
IDX_CHUNK = 512
ATT_CHUNK = 256
ATT_PAIR = IDX_CHUNK // ATT_CHUNK
INT_MIN = -(2 ** 31)
FINITE_MIN_KEY = -0x7F800000
MASKED = -1e30
LOG2E = 1.4426950408889634
QT_ROWS = 128
VT_ROWS = 80
SAFE_LOG2_RANGE = 60.0
TM = 256
VMEM_LIMIT = 56 * 1024 * 1024


def _bucket_starts():
    max_exact = NUM_BUCKETS // 2
    starts = list(range(max_exact + 1))
    n = max_exact
    for b in range(max_exact + 1, NUM_BUCKETS):
        while n ** 16 < max_exact ** 16 * 64 ** (b - max_exact):
            n += 1
        starts.append(n)
    return tuple(starts)


BUCKET_START = _bucket_starts()
FAR_DELTA = -(-(BUCKET_START[-1] + QB - 1) // QB)
N_BIAS_TILES = FAR_DELTA + 1


def _cparams(n_grid=1):
    return pltpu.CompilerParams(dimension_semantics=("arbitrary",) * n_grid,
                                vmem_limit_bytes=VMEM_LIMIT)


def _const_spec(shape):
    return pl.BlockSpec(shape, lambda i: (0,) * len(shape), pipeline_mode=pl.Buffered(1))


def _rms_rows_bf16(x, g):
    ms = jnp.mean(x * x, axis=-1, keepdims=True)
    return (x * lax.rsqrt(ms + EPS) * g).astype(BF16)


def _silu(x):
    return x / (1.0 + jnp.exp(-x))


def _bias_tiles_kernel(rb_ref, out_ref, *, n_heads, rows, base_step, base0, minus_far):
    base = base0 + pl.program_id(0) * base_step
    shape = (rows, QB)
    dist = base + lax.broadcasted_iota(I32, shape, 1) - lax.broadcasted_iota(I32, shape, 0)
    at_least = [dist >= BUCKET_START[b] for b in range(1, NUM_BUCKETS)]
    for h in range(n_heads):
        val = jnp.full(shape, rb_ref[0, h], F32)
        for b in range(1, NUM_BUCKETS):
            val = jnp.where(at_least[b - 1], rb_ref[b, h], val)
        if minus_far:
            val = val - rb_ref[NUM_BUCKETS - 1, h]
        out_ref[0, h] = val * LOG2E


def _bias_tiles(rel_bias, n_tiles, n_heads, rows, base_step, base0, minus_far=False):
    kern = functools.partial(_bias_tiles_kernel, n_heads=n_heads, rows=rows, base_step=base_step, base0=base0,
                             minus_far=minus_far)
    return pl.pallas_call(
        kern,
        grid=(n_tiles,),
        in_specs=[pl.BlockSpec(memory_space=pltpu.SMEM)],
        out_specs=pl.BlockSpec((1, n_heads, rows, QB), lambda t: (t, 0, 0, 0)),
        out_shape=jax.ShapeDtypeStruct((n_tiles, n_heads, rows, QB), F32),
        compiler_params=_cparams(),
        name="bias_tiles",
    )(rel_bias)


def _head_t_proj_kernel(x_ref, g_ref, w_ref, gain_ref, *rest, n_heads, normalize, scale):
    out_ref = rest[-1]
    if len(rest) == 2:
        for b in range(TM // QB):
            out_ref[b, HEAD_DIM:, :] = rest[0][...]
    xn = _rms_rows_bf16(x_ref[...], g_ref[...])
    h = jnp.dot(xn, w_ref[...], preferred_element_type=F32)
    gain = gain_ref[...]
    for b in range(TM // QB):
        rows = h[b * QB:(b + 1) * QB]
        for p in range(n_heads // 2):
            pair = rows[:, p * 128:(p + 1) * 128].T
            for hh in range(2):
                t = pair[hh * HEAD_DIM:(hh + 1) * HEAD_DIM]
                if normalize:
                    ms = jnp.mean(t * t, axis=0, keepdims=True)
                    t = t * lax.rsqrt(ms + EPS) * gain * scale
                head = 2 * p + hh
                out_ref[b, :HEAD_DIM, head * QB:(head + 1) * QB] = t.astype(BF16)


def _head_t_proj(x, g, w, gain, *, normalize, scale=1.0, extra=None):
    T, D = x.shape
    n_heads = w.shape[1] // HEAD_DIM
    gain_b = jnp.broadcast_to(gain.reshape(HEAD_DIM, 1), (HEAD_DIM, QB)).astype(F32)
    kern = functools.partial(_head_t_proj_kernel, n_heads=n_heads, normalize=normalize, scale=scale)
    args = [x, g.reshape(1, D), w, gain_b]
    in_specs = [pl.BlockSpec((TM, D), lambda i: (i, 0)),
                _const_spec((1, D)),
                _const_spec(w.shape),
                _const_spec((HEAD_DIM, QB))]
    rows = HEAD_DIM
    if extra is not None:
        args.append(extra)
        in_specs.append(_const_spec(extra.shape))
        rows += extra.shape[0]
    return pl.pallas_call(
        kern,
        grid=(T // TM,),
        in_specs=in_specs,
        out_specs=pl.BlockSpec((TM // QB, rows, n_heads * QB), lambda i: (i, 0, 0)),
        out_shape=jax.ShapeDtypeStruct((T // QB, rows, n_heads * QB), BF16),
        compiler_params=_cparams(),
        name="head_t_proj",
    )(*args)


def _pool_kernel(x_ref, g_ref, w_ref, pw_ref, ps_ref, out_ref, halo_ref):
    i = pl.program_id(0)
    width = out_ref.shape[1]
    gc = width // len(POOL_WINDOWS)
    xn = _rms_rows_bf16(x_ref[...], g_ref[...])
    h = jnp.dot(xn, w_ref[...], preferred_element_type=F32)
    pin = h[:, :width]
    gate = h[:, width:]

    @pl.when(i == 0)
    def _():
        halo_ref[...] = jnp.zeros_like(halo_ref)

    ext = jnp.concatenate([halo_ref[...], pin], axis=0)
    halo_ref[...] = pin[TM - POOL_HALO:]
    pos = i * TM + lax.broadcasted_iota(I32, (TM, gc), 0)
    outs = []
    for grp, win in enumerate(POOL_WINDOWS):
        e = ext[:, grp * gc:(grp + 1) * gc]
        s = e
        span = 1
        while span < win:
            s = s + pltpu.roll(s, span, 0)
            span *= 2
        s = s[POOL_HALO:]
        a = pin[:, grp * gc:(grp + 1) * gc]
        cnt = jnp.minimum(pos + 1, win).astype(F32)
        pooled = s / cnt - a
        y = jnp.dot(pooled.astype(BF16), pw_ref[grp], preferred_element_type=F32)
        outs.append(y * ps_ref[:, grp * gc:(grp + 1) * gc])
    py = jnp.concatenate(outs, axis=1)
    out_ref[...] = (_silu(gate) * py).astype(BF16)


def _pool_mixer(x, g, w, pool_w, pool_scale):
    T, D = x.shape
    width = w.shape[1] // 2
    return pl.pallas_call(
        _pool_kernel,
        grid=(T // TM,),
        in_specs=[pl.BlockSpec((TM, D), lambda i: (i, 0)),
                  _const_spec((1, D)),
                  _const_spec(w.shape),
                  _const_spec(pool_w.shape),
                  _const_spec((1, width))],
        out_specs=pl.BlockSpec((TM, width), lambda i: (i, 0)),
        out_shape=jax.ShapeDtypeStruct((T, width), BF16),
        scratch_shapes=[pltpu.VMEM((POOL_HALO, width), F32)],
        compiler_params=_cparams(),
        name="pool_mixer",
    )(x, g.reshape(1, D), w, pool_w, pool_scale.reshape(1, width))


def _dsa_side_kernel(x_ref, g_ref, w_ref, kg_ref, gate_ref, k_ref, ik_ref, vt_ref, iwt_ref, *, gate_w, iw_scale):
    xn = _rms_rows_bf16(x_ref[...], g_ref[...])
    h = jnp.dot(xn, w_ref[...], preferred_element_type=F32)
    gate_ref[...] = _silu(h[:, :gate_w]).astype(BF16)
    k = h[:, gate_w:gate_w + 128]
    ms = jnp.sum(k * k, axis=-1, keepdims=True) * (1.0 / HEAD_DIM)
    kn = k * lax.rsqrt(ms + EPS) * kg_ref[...]
    lane = lax.broadcasted_iota(I32, kn.shape, 1)
    k_ref[...] = jnp.where((lane == HEAD_DIM) | (lane == HEAD_DIM + 1), 1.0, kn).astype(BF16)
    ik_ref[...] = h[:, gate_w + 128:gate_w + 128 + HEAD_DIM].astype(BF16)
    vw = h[:, gate_w + 256:gate_w + 384].T
    row = lax.broadcasted_iota(I32, (VT_ROWS - HEAD_DIM, TM), 0)
    vt_ref[0] = jnp.concatenate([vw[:HEAD_DIM], jnp.where(row == 0, 1.0, 0.0)], axis=0).astype(BF16)
    iwt_ref[...] = vw[HEAD_DIM:HEAD_DIM + IDX_HEADS] * iw_scale


def _dsa_side_proj(x, g, w, k_gain, gate_w, iw_scale):
    T, D = x.shape
    assert TM == ATT_CHUNK
    kern = functools.partial(_dsa_side_kernel, gate_w=gate_w, iw_scale=iw_scale)
    return pl.pallas_call(
        kern,
        grid=(T // TM,),
        in_specs=[pl.BlockSpec((TM, D), lambda i: (i, 0)),
                  _const_spec((1, D)),
                  _const_spec(w.shape),
                  _const_spec((1, QT_ROWS))],
        out_specs=[pl.BlockSpec((TM, gate_w), lambda i: (i, 0)),
                   pl.BlockSpec((TM, QT_ROWS), lambda i: (i, 0)),
                   pl.BlockSpec((TM, HEAD_DIM), lambda i: (i, 0)),
                   pl.BlockSpec((1, VT_ROWS, ATT_CHUNK), lambda i: (i, 0, 0)),
                   pl.BlockSpec((IDX_HEADS, TM), lambda i: (0, i))],
        out_shape=[jax.ShapeDtypeStruct((T, gate_w), BF16),
                   jax.ShapeDtypeStruct((T, QT_ROWS), BF16),
                   jax.ShapeDtypeStruct((T, HEAD_DIM), BF16),
                   jax.ShapeDtypeStruct((T // ATT_CHUNK, VT_ROWS, ATT_CHUNK), BF16),
                   jax.ShapeDtypeStruct((IDX_HEADS, T), F32)],
        compiler_params=_cparams(),
        name="dsa_side_proj",
    )(x, g.reshape(1, D), w, _pad_cols(k_gain.reshape(1, HEAD_DIM), QT_ROWS))


def _swa_side_kernel(x_ref, g_ref, w_ref, kg_ref, gate_ref, k_ref, vt_ref, *, gate_w):
    xn = _rms_rows_bf16(x_ref[...], g_ref[...])
    h = jnp.dot(xn, w_ref[...], preferred_element_type=F32)
    gate_ref[...] = _silu(h[:, :gate_w]).astype(BF16)
    kv_w = SWA_KV_HEADS * HEAD_DIM
    for hd in range(SWA_KV_HEADS):
        k = h[:, gate_w + hd * HEAD_DIM:gate_w + (hd + 1) * HEAD_DIM]
        ms = jnp.mean(k * k, axis=-1, keepdims=True)
        k_ref[hd] = (k * lax.rsqrt(ms + EPS) * kg_ref[...]).astype(BF16)
    row = lax.broadcasted_iota(I32, (VT_ROWS - HEAD_DIM, TM), 0)
    ones_rows = jnp.where(row == 0, 1.0, 0.0)
    for p in range(SWA_KV_HEADS // 2):
        pair = h[:, gate_w + kv_w + p * 128:gate_w + kv_w + (p + 1) * 128].T
        vt_ref[2 * p] = jnp.concatenate([pair[:HEAD_DIM], ones_rows], axis=0).astype(BF16)
        vt_ref[2 * p + 1] = jnp.concatenate([pair[HEAD_DIM:], ones_rows], axis=0).astype(BF16)


def _swa_side_proj(x, g, w, k_gain, gate_w):
    T, D = x.shape
    kern = functools.partial(_swa_side_kernel, gate_w=gate_w)
    return pl.pallas_call(
        kern,
        grid=(T // TM,),
        in_specs=[pl.BlockSpec((TM, D), lambda i: (i, 0)),
                  _const_spec((1, D)),
                  _const_spec(w.shape),
                  _const_spec((1, HEAD_DIM))],
        out_specs=[pl.BlockSpec((TM, gate_w), lambda i: (i, 0)),
                   pl.BlockSpec((SWA_KV_HEADS, TM, HEAD_DIM), lambda i: (0, i, 0)),
                   pl.BlockSpec((SWA_KV_HEADS, VT_ROWS, TM), lambda i: (0, 0, i))],
        out_shape=[jax.ShapeDtypeStruct((T, gate_w), BF16),
                   jax.ShapeDtypeStruct((SWA_KV_HEADS, T, HEAD_DIM), BF16),
                   jax.ShapeDtypeStruct((SWA_KV_HEADS, VT_ROWS, T), BF16)],
        compiler_params=_cparams(),
        name="swa_side_proj",
    )(x, g.reshape(1, D), w, k_gain.reshape(1, HEAD_DIM))


def _out_proj_kernel(*refs, n_terms):
    x_ref = refs[0]
    out_ref = refs[-1]
    acc = x_ref[...]
    for t in range(n_terms):
        acc = acc + jnp.dot(refs[1 + 2 * t][...], refs[2 + 2 * t][...], preferred_element_type=F32)
    out_ref[...] = acc


def _out_proj(x, terms):
    T, D = x.shape
    args = [x]
    in_specs = [pl.BlockSpec((TM, D), lambda i: (i, 0))]
    for a, w in terms:
        args += [a, w]
        in_specs += [pl.BlockSpec((TM, a.shape[1]), lambda i: (i, 0)), _const_spec(w.shape)]
    return pl.pallas_call(
        functools.partial(_out_proj_kernel, n_terms=len(terms)),
        grid=(T // TM,),
        in_specs=in_specs,
        out_specs=pl.BlockSpec((TM, D), lambda i: (i, 0)),
        out_shape=jax.ShapeDtypeStruct((T, D), F32),
        compiler_params=_cparams(),
        name="out_proj",
    )(*args)


def _store_gated(o_t, gate_ref, out_ref, n_heads):
    for p in range(n_heads // 2):
        pair = jnp.concatenate([o_t[:, (2 * p) * QB:(2 * p + 1) * QB],
                                o_t[:, (2 * p + 1) * QB:(2 * p + 2) * QB]], axis=0)
        g = gate_ref[:, p * 128:(p + 1) * 128].astype(F32)
        out_ref[:, p * 128:(p + 1) * 128] = (g * pair.T).astype(BF16)


def _dsa_kernel(bounded_ref, qt_ref, iqt_ref, iwt_ref, gate_ref, k_ref, ik_ref, vt_ref, bias_ref, out_ref,
                score_ref):
    i = pl.program_id(0)
    n_idx = i // (IDX_CHUNK // QB) + 1
    lanes = DSA_HEADS * QB

    iqt = iqt_ref[0]
    iwt = iwt_ref[...]
    qpos = i * QB + lax.broadcasted_iota(I32, (IDX_CHUNK, QB), 1)
    krow = lax.broadcasted_iota(I32, (IDX_CHUNK, QB), 0)

    def idx_body(c, carry):
        off = pl.multiple_of(c * IDX_CHUNK, IDX_CHUNK)
        s = jnp.dot(ik_ref[pl.ds(off, IDX_CHUNK), :], iqt, preferred_element_type=F32)
        score = jnp.zeros((IDX_CHUNK, QB), F32)
        for h in range(IDX_HEADS):
            score = score + jnp.maximum(s[:, h * QB:(h + 1) * QB], 0.0) * iwt[h:h + 1, :]
        score_ref[pl.ds(off, IDX_CHUNK), :] = jnp.where(off + krow <= qpos, score, -jnp.inf)
        return carry

    lax.fori_loop(0, n_idx, idx_body, 0)

    def key_to_float(key):
        return lax.bitcast_convert_type(jnp.where(key < 0, key ^ 0x7FFFFFFF, key), F32)

    def count_ge(key):
        cand = key_to_float(key)

        def body(c, acc):
            off = pl.multiple_of(c * IDX_CHUNK, IDX_CHUNK)
            hit = jnp.where(score_ref[pl.ds(off, IDX_CHUNK), :] >= cand, 1, 0).astype(I32)
            return acc + jnp.sum(hit.reshape(IDX_CHUNK // 8, 8, QB), axis=0)

        acc = lax.fori_loop(0, n_idx, body, jnp.zeros((8, QB), I32))
        return jnp.sum(acc, axis=0, keepdims=True)

    zero = jnp.zeros((1, QB), I32)
    base = jnp.where(count_ge(zero) >= DSA_TOPK, zero, INT_MIN)

    def bit_body(b, base):
        cand = base | jnp.left_shift(jnp.int32(1), 30 - b)
        return jnp.where(count_ge(cand) >= DSA_TOPK, cand, base)

    base = lax.fori_loop(0, 31, bit_body, base)
    thr = key_to_float(jnp.maximum(base, FINITE_MIN_KEY))

    qt = qt_ref[0]
    sub = ATT_CHUNK // QB

    def tiles(c, near, fn):
        off = pl.multiple_of(c * ATT_CHUNK, ATT_CHUNK)
        s = jnp.dot(k_ref[pl.ds(off, ATT_CHUNK), :], qt, preferred_element_type=F32)
        sel = score_ref[pl.ds(off, ATT_CHUNK), :] >= thr
        blocks = []
        for sb in range(sub):
            delta = jnp.clip(i - (c * sub + sb), 0, FAR_DELTA)
            sel_sb = sel[sb * QB:(sb + 1) * QB]
            row = []
            for h in range(DSA_HEADS):
                t = s[sb * QB:(sb + 1) * QB, h * QB:(h + 1) * QB]
                if near:
                    t = t + bias_ref[delta, h]
                row.append(fn(t, sel_sb))
            blocks.append(jnp.concatenate(row, axis=1))
        return jnp.concatenate(blocks, axis=0)

    def bounded_body(near, pair, acc):
        pvs = []
        for half in range(ATT_PAIR):
            c = pair * ATT_PAIR + half
            p = tiles(c, near, lambda t, sel: jnp.where(sel, jnp.exp2(t), 0.0))
            pvs.append(jnp.dot(vt_ref[c], p.astype(BF16), preferred_element_type=F32))
        return acc + sum(pvs[1:], pvs[0])

    def bounded_path():
        n_far = jnp.maximum(i - (FAR_DELTA - 1), 0) // (sub * ATT_PAIR)
        acc = jnp.zeros((VT_ROWS, lanes), F32)
        acc = lax.fori_loop(0, n_far, functools.partial(bounded_body, False), acc)
        acc = lax.fori_loop(n_far, n_idx, functools.partial(bounded_body, True), acc)
        return acc[:HEAD_DIM] / acc[HEAD_DIM:HEAD_DIM + 1]

    def general_body(c, carry):
        m, l, acc = carry
        t = tiles(c, True, lambda t, sel: jnp.where(sel, t, MASKED))
        m_new = jnp.maximum(m, jnp.max(t, axis=0, keepdims=True))
        alpha = jnp.exp2(m - m_new)
        p = jnp.exp2(t - m_new)
        l_new = alpha * l + jnp.sum(p, axis=0, keepdims=True)
        pv = jnp.dot(vt_ref[c], p.astype(BF16), preferred_element_type=F32)
        return m_new, l_new, acc * alpha + pv

    def general_path():
        m0 = jnp.full((1, lanes), MASKED, F32)
        l0 = jnp.zeros((1, lanes), F32)
        acc0 = jnp.zeros((VT_ROWS, lanes), F32)
        _, l, acc = lax.fori_loop(0, n_idx * ATT_PAIR, general_body, (m0, l0, acc0))
        return acc[:HEAD_DIM] / l

    o_t = lax.cond(bounded_ref[0] > 0, bounded_path, general_path)
    _store_gated(o_t, gate_ref, out_ref, DSA_HEADS)


def _dsa_attention(bounded, qt, iqt, iwt, gate, k, ik, vt, bias):
    nb = qt.shape[0]
    T = k.shape[0]
    width = DSA_HEADS * HEAD_DIM
    return pl.pallas_call(
        _dsa_kernel,
        grid=(nb,),
        in_specs=[pl.BlockSpec(memory_space=pltpu.SMEM),
                  pl.BlockSpec((1, QT_ROWS, DSA_HEADS * QB), lambda i: (i, 0, 0)),
                  pl.BlockSpec((1, HEAD_DIM, IDX_HEADS * QB), lambda i: (i, 0, 0)),
                  pl.BlockSpec((IDX_HEADS, QB), lambda i: (0, i)),
                  pl.BlockSpec((QB, width), lambda i: (i, 0)),
                  _const_spec(k.shape),
                  _const_spec(ik.shape),
                  _const_spec(vt.shape),
                  _const_spec(bias.shape)],
        out_specs=pl.BlockSpec((QB, width), lambda i: (i, 0)),
        out_shape=jax.ShapeDtypeStruct((T, width), BF16),
        scratch_shapes=[pltpu.VMEM((T, QB), F32)],
        compiler_params=_cparams(),
        name="dsa_attention",
    )(bounded, qt, iqt, iwt, gate, k, ik, vt, bias)


def _swa_kernel(bounded_ref, qt_ref, kp_ref, kc_ref, vp_ref, vc_ref, sink_ref, gate_ref, bias_ref, out_ref):
    i = pl.program_id(0)
    grp = SWA_Q_HEADS // SWA_KV_HEADS
    shape = (2 * QB, QB)
    c_idx = lax.broadcasted_iota(I32, shape, 0)
    dist = QB + lax.broadcasted_iota(I32, shape, 1) - c_idx
    mask = (dist >= 0) & (dist < QB) & ((c_idx >= QB) | (i > 0))

    def group(kv, bounded):
        kcat = jnp.concatenate([kp_ref[kv], kc_ref[kv]], axis=0)
        vcat = jnp.concatenate([vp_ref[kv], vc_ref[kv]], axis=1)
        qg = qt_ref[0, :, kv * grp * QB:(kv + 1) * grp * QB]
        s = jnp.dot(kcat, qg, preferred_element_type=F32)
        sink = sink_ref[:, kv * grp * QB:(kv + 1) * grp * QB]
        logits = [s[:, j * QB:(j + 1) * QB] + bias_ref[kv * grp + j] for j in range(grp)]
        if bounded:
            p = jnp.concatenate([jnp.where(mask, jnp.exp2(t), 0.0) for t in logits], axis=1)
            pv = jnp.dot(vcat, p.astype(BF16), preferred_element_type=F32)
            return pv[:HEAD_DIM] / (pv[HEAD_DIM:HEAD_DIM + 1] + jnp.exp2(sink))
        t = jnp.concatenate([jnp.where(mask, t, MASKED) for t in logits], axis=1)
        m = jnp.maximum(jnp.max(t, axis=0, keepdims=True), sink)
        p = jnp.exp2(t - m)
        l = jnp.sum(p, axis=0, keepdims=True) + jnp.exp2(sink - m)
        pv = jnp.dot(vcat, p.astype(BF16), preferred_element_type=F32)
        return pv[:HEAD_DIM] / l

    def path(bounded):
        return jnp.concatenate([group(kv, bounded) for kv in range(SWA_KV_HEADS)], axis=1)

    o_t = lax.cond(bounded_ref[0] > 0, functools.partial(path, True), functools.partial(path, False))
    _store_gated(o_t, gate_ref, out_ref, SWA_Q_HEADS)


def _swa_attention(bounded, qt, k_hm, vt_hm, sink_b, gate, bias):
    nb = qt.shape[0]
    T = k_hm.shape[1]
    width = SWA_Q_HEADS * HEAD_DIM

    def prev(i):
        return jnp.maximum(i - 1, 0)

    return pl.pallas_call(
        _swa_kernel,
        grid=(nb,),
        in_specs=[pl.BlockSpec(memory_space=pltpu.SMEM),
                  pl.BlockSpec((1, HEAD_DIM, SWA_Q_HEADS * QB), lambda i: (i, 0, 0)),
                  pl.BlockSpec((SWA_KV_HEADS, QB, HEAD_DIM), lambda i: (0, prev(i), 0)),
                  pl.BlockSpec((SWA_KV_HEADS, QB, HEAD_DIM), lambda i: (0, i, 0)),
                  pl.BlockSpec((SWA_KV_HEADS, VT_ROWS, QB), lambda i: (0, 0, prev(i))),
                  pl.BlockSpec((SWA_KV_HEADS, VT_ROWS, QB), lambda i: (0, 0, i)),
                  _const_spec((1, SWA_Q_HEADS * QB)),
                  pl.BlockSpec((QB, width), lambda i: (i, 0)),
                  _const_spec(bias.shape)],
        out_specs=pl.BlockSpec((QB, width), lambda i: (i, 0)),
        out_shape=jax.ShapeDtypeStruct((T, width), BF16),
        compiler_params=_cparams(),
        name="swa_attention",
    )(bounded, qt, k_hm, k_hm, vt_hm, vt_hm, sink_b, gate, bias)


def _pad_cols(w, width):
    return jnp.pad(w, ((0, 0), (0, width - w.shape[1])))


def _even_layer(x, norm_g, w_in, pool_w, pool_scale, q_gain, k_gain, w_out, rel_bias, dsa_bias):
    pool_width = pool_w.shape[0] * pool_w.shape[1]
    dsa_width = DSA_HEADS * HEAD_DIM
    idx_width = IDX_HEADS * HEAD_DIM
    sizes = (pool_width, pool_width, dsa_width, HEAD_DIM, HEAD_DIM, dsa_width, idx_width, HEAD_DIM, IDX_HEADS)
    offs = [0]
    for s in sizes:
        offs.append(offs[-1] + s)
    col = lambda j: w_in[:, offs[j]:offs[j + 1]]
    w_pool = jnp.concatenate([col(0), col(1)], axis=1).astype(BF16)
    w_q = col(2).astype(BF16)
    w_iq = col(6).astype(BF16)
    w_side = jnp.concatenate([col(5), _pad_cols(col(3), 128), _pad_cols(col(7), 128),
                              _pad_cols(jnp.concatenate([col(4), col(8)], axis=1), 128)], axis=1).astype(BF16)

    q_scale = HEAD_DIM ** -0.5 * LOG2E
    far = rel_bias[NUM_BUCKETS - 1, :DSA_HEADS].astype(F32) * LOG2E
    far_hi = far.astype(BF16)
    far_lo = (far - far_hi.astype(F32)).astype(BF16)
    extra = jnp.zeros((QT_ROWS - HEAD_DIM, DSA_HEADS * QB), BF16)
    extra = extra.at[0].set(jnp.repeat(far_hi, QB)).at[1].set(jnp.repeat(far_lo, QB))
    q_norm = jnp.sqrt(HEAD_DIM * jnp.max(q_gain.astype(F32) ** 2) * q_scale ** 2 + jnp.max(far ** 2))
    k_norm = jnp.sqrt(HEAD_DIM * jnp.max(k_gain.astype(F32) ** 2) + 2.0)
    near_max = jnp.max(jnp.abs(rel_bias[:, :DSA_HEADS] - rel_bias[NUM_BUCKETS - 1, :DSA_HEADS])) * LOG2E
    bound = 1.02 * q_norm * k_norm + near_max
    bounded = (bound < SAFE_LOG2_RANGE).astype(I32).reshape(1)

    py = _pool_mixer(x, norm_g, w_pool, pool_w.astype(BF16), pool_scale)
    qt = _head_t_proj(x, norm_g, w_q, q_gain, normalize=True, scale=q_scale, extra=extra)
    iqt = _head_t_proj(x, norm_g, w_iq, q_gain, normalize=False)
    gate, k, ik, vt, iwt = _dsa_side_proj(x, norm_g, w_side, k_gain, dsa_width,
                                          iw_scale=IDX_HEADS ** -0.5 * HEAD_DIM ** -0.5)
    dy = _dsa_attention(bounded, qt, iqt, iwt, gate, k, ik, vt, dsa_bias)
    w_o = w_out.astype(BF16)
    return _out_proj(x, [(py, w_o[:pool_width]), (dy, w_o[pool_width:])])


def _odd_layer(x, norm_g, w_in, q_gain, k_gain, sinks, w_out, rel_bias, swa_bias):
    q_width = SWA_Q_HEADS * HEAD_DIM
    kv_width = SWA_KV_HEADS * HEAD_DIM
    w_q = w_in[:, :q_width].astype(BF16)
    w_side = jnp.concatenate([w_in[:, q_width + 2 * kv_width:], w_in[:, q_width:q_width + 2 * kv_width]],
                             axis=1).astype(BF16)
    q_scale = HEAD_DIM ** -0.5 * LOG2E
    sink_b = jnp.repeat(sinks.astype(F32) * LOG2E, QB).reshape(1, SWA_Q_HEADS * QB)
    q_norm = jnp.sqrt(HEAD_DIM * jnp.max(q_gain.astype(F32) ** 2)) * q_scale
    k_norm = jnp.sqrt(HEAD_DIM * jnp.max(k_gain.astype(F32) ** 2))
    bound = jnp.maximum(1.02 * q_norm * k_norm + jnp.max(jnp.abs(rel_bias)) * LOG2E, jnp.max(jnp.abs(sink_b)))
    bounded = (bound < SAFE_LOG2_RANGE).astype(I32).reshape(1)

    qt = _head_t_proj(x, norm_g, w_q, q_gain, normalize=True, scale=q_scale)
    gate, k_hm, vt_hm = _swa_side_proj(x, norm_g, w_side, k_gain, q_width)
    go = _swa_attention(bounded, qt, k_hm, vt_hm, sink_b, gate, swa_bias)
    return _out_proj(x, [(go, w_out.astype(BF16))])


def kernel(x, rel_bias, even_norm, even_w_in, even_pool_w, even_pool_scale, even_q_gain, even_k_gain,
           even_w_out, odd_norm, odd_w_in, odd_q_gain, odd_k_gain, odd_sinks, odd_w_out):
    B, T, D = x.shape
    depth = even_norm.shape[0] + odd_norm.shape[0]
    dsa_bias = _bias_tiles(rel_bias, N_BIAS_TILES, DSA_HEADS, QB, QB, 0, minus_far=True)
    swa_bias = _bias_tiles(rel_bias, 1, SWA_Q_HEADS, 2 * QB, 0, QB)[0]
    outs = []
    for b in range(B):
        h = x.reshape(T, D) if B == 1 else x[b]
        for layer in range(depth):
            j = layer // 2
            if layer % 2 == 0:
                h = _even_layer(h, even_norm[j], even_w_in[j], even_pool_w[j], even_pool_scale[j],
                                even_q_gain[j], even_k_gain[j], even_w_out[j], rel_bias, dsa_bias)
            else:
                h = _odd_layer(h, odd_norm[j], odd_w_in[j], odd_q_gain[j], odd_k_gain[j],
                               odd_sinks[j], odd_w_out[j], rel_bias, swa_bias)
        outs.append(h)
    return outs[0].reshape(B, T, D) if B == 1 else jnp.stack(outs, axis=0)
```

```python
import functools

import jax
import jax.numpy as jnp
from jax import lax
from jax.experimental import pallas as pl
from jax.experimental.pallas import tpu as pltpu

F32 = jnp.float32
BF16 = jnp.bfloat16
I32 = jnp.int32

EPS = 1e-6
HEAD_DIM = 64
QB = 128
POOL_WINDOWS = (2, 4, 8, 16)
POOL_HALO = 16
DSA_HEADS = 16
IDX_HEADS = 16
DSA_TOPK = 256
SWA_Q_HEADS = 32
SWA_KV_HEADS = 4
NUM_BUCKETS = 32
IDX_CHUNK = 512
ATT_CHUNK = 256
ATT_PAIR = IDX_CHUNK // ATT_CHUNK
INT_MIN = -(2 ** 31)
FINITE_MIN_KEY = -0x7F800000
SEARCH_PASSES = 32
MASKED = -1e30
LOG2E = 1.4426950408889634
QT_ROWS = 128
VT_ROWS = 80
SAFE_LOG2_RANGE = 60.0
TM = 256
VMEM_LIMIT = 56 * 1024 * 1024


def _bucket_starts():
    max_exact = NUM_BUCKETS // 2
    starts = list(range(max_exact + 1))
    n = max_exact
    for b in range(max_exact + 1, NUM_BUCKETS):
        while n ** 16 < max_exact ** 16 * 64 ** (b - max_exact):
            n += 1
        starts.append(n)
    return tuple(starts)


BUCKET_START = _bucket_starts()
FAR_DELTA = -(-(BUCKET_START[-1] + QB - 1) // QB)
N_BIAS_TILES = FAR_DELTA + 1


def _cparams(n_grid=1):
    return pltpu.CompilerParams(dimension_semantics=("arbitrary",) * n_grid,
                                vmem_limit_bytes=VMEM_LIMIT)


def _const_spec(shape):
    return pl.BlockSpec(shape, lambda i: (0,) * len(shape), pipeline_mode=pl.Buffered(1))


def _rms_rows_bf16(x, g):
    ms = jnp.mean(x * x, axis=-1, keepdims=True)
    return (x * lax.rsqrt(ms + EPS) * g).astype(BF16)


def _silu(x):
    return x / (1.0 + jnp.exp(-x))


def _bias_tiles_kernel(rb_ref, out_ref, *, n_heads, rows, base_step, base0, minus_far):
    base = base0 + pl.program_id(0) * base_step
    shape = (rows, QB)
    dist = base + lax.broadcasted_iota(I32, shape, 1) - lax.broadcasted_iota(I32, shape, 0)
    at_least = [dist >= BUCKET_START[b] for b in range(1, NUM_BUCKETS)]
    for h in range(n_heads):
        val = jnp.full(shape, rb_ref[0, h], F32)
        for b in range(1, NUM_BUCKETS):
            val = jnp.where(at_least[b - 1], rb_ref[b, h], val)
        if minus_far:
            val = val - rb_ref[NUM_BUCKETS - 1, h]
        out_ref[0, h] = val * LOG2E


def _bias_tiles(rel_bias, n_tiles, n_heads, rows, base_step, base0, minus_far=False):
    kern = functools.partial(_bias_tiles_kernel, n_heads=n_heads, rows=rows, base_step=base_step, base0=base0,
                             minus_far=minus_far)
    return pl.pallas_call(
        kern,
        grid=(n_tiles,),
        in_specs=[pl.BlockSpec(memory_space=pltpu.SMEM)],
        out_specs=pl.BlockSpec((1, n_heads, rows, QB), lambda t: (t, 0, 0, 0)),
        out_shape=jax.ShapeDtypeStruct((n_tiles, n_heads, rows, QB), F32),
        compiler_params=_cparams(),
        name="bias_tiles",
    )(rel_bias)


def _head_t_proj_kernel(x_ref, g_ref, w_ref, gain_ref, *rest, n_heads, normalize, scale):
    out_ref = rest[-1]
    if len(rest) == 2:
        for b in range(TM // QB):
            out_ref[b, HEAD_DIM:, :] = rest[0][...]
    xn = _rms_rows_bf16(x_ref[...], g_ref[...])
    h = jnp.dot(xn, w_ref[...], preferred_element_type=F32)
    gain = gain_ref[...]
    for b in range(TM // QB):
        rows = h[b * QB:(b + 1) * QB]
        for p in range(n_heads // 2):
            pair = rows[:, p * 128:(p + 1) * 128].T
            for hh in range(2):
                t = pair[hh * HEAD_DIM:(hh + 1) * HEAD_DIM]
                if normalize:
                    ms = jnp.mean(t * t, axis=0, keepdims=True)
                    t = t * lax.rsqrt(ms + EPS) * gain * scale
                head = 2 * p + hh
                out_ref[b, :HEAD_DIM, head * QB:(head + 1) * QB] = t.astype(BF16)


def _head_t_proj(x, g, w, gain, *, normalize, scale=1.0, extra=None):
    T, D = x.shape
    n_heads = w.shape[1] // HEAD_DIM
    gain_b = jnp.broadcast_to(gain.reshape(HEAD_DIM, 1), (HEAD_DIM, QB)).astype(F32)
    kern = functools.partial(_head_t_proj_kernel, n_heads=n_heads, normalize=normalize, scale=scale)
    args = [x, g.reshape(1, D), w, gain_b]
    in_specs = [pl.BlockSpec((TM, D), lambda i: (i, 0)),
                _const_spec((1, D)),
                _const_spec(w.shape),
                _const_spec((HEAD_DIM, QB))]
    rows = HEAD_DIM
    if extra is not None:
        args.append(extra)
        in_specs.append(_const_spec(extra.shape))
        rows += extra.shape[0]
    return pl.pallas_call(
        kern,
        grid=(T // TM,),
        in_specs=in_specs,
        out_specs=pl.BlockSpec((TM // QB, rows, n_heads * QB), lambda i: (i, 0, 0)),
        out_shape=jax.ShapeDtypeStruct((T // QB, rows, n_heads * QB), BF16),
        compiler_params=_cparams(),
        name="head_t_proj",
    )(*args)


def _pool_kernel(x_ref, g_ref, w_ref, pw_ref, ps_ref, out_ref, halo_ref):
    i = pl.program_id(0)
    width = out_ref.shape[1]
    gc = width // len(POOL_WINDOWS)
    xn = _rms_rows_bf16(x_ref[...], g_ref[...])
    h = jnp.dot(xn, w_ref[...], preferred_element_type=F32)
    pin = h[:, :width]
    gate = h[:, width:]

    @pl.when(i == 0)
    def _():
        halo_ref[...] = jnp.zeros_like(halo_ref)

    ext = jnp.concatenate([halo_ref[...], pin], axis=0)
    halo_ref[...] = pin[TM - POOL_HALO:]
    pos = i * TM + lax.broadcasted_iota(I32, (TM, gc), 0)
    outs = []
    for grp, win in enumerate(POOL_WINDOWS):
        e = ext[:, grp * gc:(grp + 1) * gc]
        s = e
        span = 1
        while span < win:
            s = s + pltpu.roll(s, span, 0)
            span *= 2
        s = s[POOL_HALO:]
        a = pin[:, grp * gc:(grp + 1) * gc]
        cnt = jnp.minimum(pos + 1, win).astype(F32)
        pooled = s / cnt - a
        y = jnp.dot(pooled.astype(BF16), pw_ref[grp], preferred_element_type=F32)
        outs.append(y * ps_ref[:, grp * gc:(grp + 1) * gc])
    py = jnp.concatenate(outs, axis=1)
    out_ref[...] = (_silu(gate) * py).astype(BF16)


def _pool_mixer(x, g, w, pool_w, pool_scale):
    T, D = x.shape
    width = w.shape[1] // 2
    return pl.pallas_call(
        _pool_kernel,
        grid=(T // TM,),
        in_specs=[pl.BlockSpec((TM, D), lambda i: (i, 0)),
                  _const_spec((1, D)),
                  _const_spec(w.shape),
                  _const_spec(pool_w.shape),
                  _const_spec((1, width))],
        out_specs=pl.BlockSpec((TM, width), lambda i: (i, 0)),
        out_shape=jax.ShapeDtypeStruct((T, width), BF16),
        scratch_shapes=[pltpu.VMEM((POOL_HALO, width), F32)],
        compiler_params=_cparams(),
        name="pool_mixer",
    )(x, g.reshape(1, D), w, pool_w, pool_scale.reshape(1, width))


def _dsa_side_kernel(x_ref, g_ref, w_ref, kg_ref, gate_ref, k_ref, ik_ref, vt_ref, iwt_ref, *, gate_w, iw_scale):
    xn = _rms_rows_bf16(x_ref[...], g_ref[...])
    h = jnp.dot(xn, w_ref[...], preferred_element_type=F32)
    gate_ref[...] = _silu(h[:, :gate_w]).astype(BF16)
    k = h[:, gate_w:gate_w + 128]
    ms = jnp.sum(k * k, axis=-1, keepdims=True) * (1.0 / HEAD_DIM)
    kn = k * lax.rsqrt(ms + EPS) * kg_ref[...]
    lane = lax.broadcasted_iota(I32, kn.shape, 1)
    k_ref[...] = jnp.where((lane == HEAD_DIM) | (lane == HEAD_DIM + 1), 1.0, kn).astype(BF16)
    ik_ref[...] = h[:, gate_w + 128:gate_w + 128 + HEAD_DIM].astype(BF16)
    vw = h[:, gate_w + 256:gate_w + 384].T
    row = lax.broadcasted_iota(I32, (VT_ROWS - HEAD_DIM, TM), 0)
    vt_ref[0] = jnp.concatenate([vw[:HEAD_DIM], jnp.where(row == 0, 1.0, 0.0)], axis=0).astype(BF16)
    iwt_ref[...] = vw[HEAD_DIM:HEAD_DIM + IDX_HEADS] * iw_scale


def _dsa_side_proj(x, g, w, k_gain, gate_w, iw_scale):
    T, D = x.shape
    assert TM == ATT_CHUNK
    kern = functools.partial(_dsa_side_kernel, gate_w=gate_w, iw_scale=iw_scale)
    return pl.pallas_call(
        kern,
        grid=(T // TM,),
        in_specs=[pl.BlockSpec((TM, D), lambda i: (i, 0)),
                  _const_spec((1, D)),
                  _const_spec(w.shape),
                  _const_spec((1, QT_ROWS))],
        out_specs=[pl.BlockSpec((TM, gate_w), lambda i: (i, 0)),
                   pl.BlockSpec((TM, QT_ROWS), lambda i: (i, 0)),
                   pl.BlockSpec((TM, HEAD_DIM), lambda i: (i, 0)),
                   pl.BlockSpec((1, VT_ROWS, ATT_CHUNK), lambda i: (i, 0, 0)),
                   pl.BlockSpec((IDX_HEADS, TM), lambda i: (0, i))],
        out_shape=[jax.ShapeDtypeStruct((T, gate_w), BF16),
                   jax.ShapeDtypeStruct((T, QT_ROWS), BF16),
                   jax.ShapeDtypeStruct((T, HEAD_DIM), BF16),
                   jax.ShapeDtypeStruct((T // ATT_CHUNK, VT_ROWS, ATT_CHUNK), BF16),
                   jax.ShapeDtypeStruct((IDX_HEADS, T), F32)],
        compiler_params=_cparams(),
        name="dsa_side_proj",
    )(x, g.reshape(1, D), w, _pad_cols(k_gain.reshape(1, HEAD_DIM), QT_ROWS))


def _swa_side_kernel(x_ref, g_ref, w_ref, kg_ref, gate_ref, k_ref, vt_ref, *, gate_w):
    xn = _rms_rows_bf16(x_ref[...], g_ref[...])
    h = jnp.dot(xn, w_ref[...], preferred_element_type=F32)
    gate_ref[...] = _silu(h[:, :gate_w]).astype(BF16)
    kv_w = SWA_KV_HEADS * HEAD_DIM
    for hd in range(SWA_KV_HEADS):
        k = h[:, gate_w + hd * HEAD_DIM:gate_w + (hd + 1) * HEAD_DIM]
        ms = jnp.mean(k * k, axis=-1, keepdims=True)
        k_ref[hd] = (k * lax.rsqrt(ms + EPS) * kg_ref[...]).astype(BF16)
    row = lax.broadcasted_iota(I32, (VT_ROWS - HEAD_DIM, TM), 0)
    ones_rows = jnp.where(row == 0, 1.0, 0.0)
    for p in range(SWA_KV_HEADS // 2):
        pair = h[:, gate_w + kv_w + p * 128:gate_w + kv_w + (p + 1) * 128].T
        vt_ref[2 * p] = jnp.concatenate([pair[:HEAD_DIM], ones_rows], axis=0).astype(BF16)
        vt_ref[2 * p + 1] = jnp.concatenate([pair[HEAD_DIM:], ones_rows], axis=0).astype(BF16)


def _swa_side_proj(x, g, w, k_gain, gate_w):
    T, D = x.shape
    kern = functools.partial(_swa_side_kernel, gate_w=gate_w)
    return pl.pallas_call(
        kern,
        grid=(T // TM,),
        in_specs=[pl.BlockSpec((TM, D), lambda i: (i, 0)),
                  _const_spec((1, D)),
                  _const_spec(w.shape),
                  _const_spec((1, HEAD_DIM))],
        out_specs=[pl.BlockSpec((TM, gate_w), lambda i: (i, 0)),
                   pl.BlockSpec((SWA_KV_HEADS, TM, HEAD_DIM), lambda i: (0, i, 0)),
                   pl.BlockSpec((SWA_KV_HEADS, VT_ROWS, TM), lambda i: (0, 0, i))],
        out_shape=[jax.ShapeDtypeStruct((T, gate_w), BF16),
                   jax.ShapeDtypeStruct((SWA_KV_HEADS, T, HEAD_DIM), BF16),
                   jax.ShapeDtypeStruct((SWA_KV_HEADS, VT_ROWS, T), BF16)],
        compiler_params=_cparams(),
        name="swa_side_proj",
    )(x, g.reshape(1, D), w, k_gain.reshape(1, HEAD_DIM))


def _out_proj_kernel(*refs, n_terms):
    x_ref = refs[0]
    out_ref = refs[-1]
    acc = x_ref[...]
    for t in range(n_terms):
        acc = acc + jnp.dot(refs[1 + 2 * t][...], refs[2 + 2 * t][...], preferred_element_type=F32)
    out_ref[...] = acc


def _out_proj(x, terms):
    T, D = x.shape
    args = [x]
    in_specs = [pl.BlockSpec((TM, D), lambda i: (i, 0))]
    for a, w in terms:
        args += [a, w]
        in_specs += [pl.BlockSpec((TM, a.shape[1]), lambda i: (i, 0)), _const_spec(w.shape)]
    return pl.pallas_call(
        functools.partial(_out_proj_kernel, n_terms=len(terms)),
        grid=(T // TM,),
        in_specs=in_specs,
        out_specs=pl.BlockSpec((TM, D), lambda i: (i, 0)),
        out_shape=jax.ShapeDtypeStruct((T, D), F32),
        compiler_params=_cparams(),
        name="out_proj",
    )(*args)


def _store_gated(o_t, gate_ref, out_ref, n_heads):
    for p in range(n_heads // 2):
        pair = jnp.concatenate([o_t[:, (2 * p) * QB:(2 * p + 1) * QB],
                                o_t[:, (2 * p + 1) * QB:(2 * p + 2) * QB]], axis=0)
        g = gate_ref[:, p * 128:(p + 1) * 128].astype(F32)
        out_ref[:, p * 128:(p + 1) * 128] = (g * pair.T).astype(BF16)


def _dsa_kernel(bounded_ref, qt_ref, iqt_ref, iwt_ref, gate_ref, k_ref, ik_ref, vt_ref, bias_ref, out_ref,
                score_ref, thr_ref):
    s = pl.program_id(0)
    nb = pl.num_programs(0) - 1
    lanes = DSA_HEADS * QB
    blocks_per_chunk = IDX_CHUNK // QB
    sub = ATT_CHUNK // QB
    cur = s
    prv = jnp.maximum(s - 1, 0)
    n_cur = cur // blocks_per_chunk + 1
    n_prv = prv // blocks_per_chunk + 1
    cur_slot = cur % 2
    prv_slot = prv % 2

    @pl.when(s < nb)
    def _():
        iqt = iqt_ref[0]
        iwt = iwt_ref[...]
        qpos = cur * QB + lax.broadcasted_iota(I32, (IDX_CHUNK, QB), 1)
        krow = lax.broadcasted_iota(I32, (IDX_CHUNK, QB), 0)

        def idx_body(c, carry):
            off = pl.multiple_of(c * IDX_CHUNK, IDX_CHUNK)
            sc = jnp.dot(ik_ref[pl.ds(off, IDX_CHUNK), :], iqt, preferred_element_type=F32)
            score = jnp.zeros((IDX_CHUNK, QB), F32)
            for h in range(IDX_HEADS):
                score = score + jnp.maximum(sc[:, h * QB:(h + 1) * QB], 0.0) * iwt[h:h + 1, :]
            score_ref[cur_slot, pl.ds(off, IDX_CHUNK), :] = jnp.where(off + krow <= qpos, score, -jnp.inf)
            return carry

        lax.fori_loop(0, n_cur, idx_body, 0)

    def key_to_float(key):
        return lax.bitcast_convert_type(jnp.where(key < 0, key ^ 0x7FFFFFFF, key), F32)

    def search_unit(state):
        base, bit, acc, ch = state
        cand = key_to_float(base + bit)
        off = pl.multiple_of(ch * IDX_CHUNK, IDX_CHUNK)
        hit = jnp.where(score_ref[cur_slot, pl.ds(off, IDX_CHUNK), :] >= cand, 1, 0).astype(I32)
        acc = acc + jnp.sum(hit.reshape(IDX_CHUNK // 8, 8, QB), axis=0)
        last = ch + 1 == n_cur
        accept = jnp.sum(acc, axis=0, keepdims=True) >= DSA_TOPK
        base = jnp.where(last & accept, base + bit, base)
        acc = jnp.where(last, 0, acc)
        bit = jnp.where(last, lax.shift_right_logical(bit, 1), bit)
        ch = jnp.where(last, 0, ch + 1)
        return base, bit, acc, ch

    search_init = (jnp.full((1, QB), INT_MIN, I32), jnp.int32(INT_MIN), jnp.zeros((8, QB), I32), jnp.int32(0))

    def store_threshold(state):
        thr = key_to_float(jnp.maximum(state[0], FINITE_MIN_KEY))
        thr_ref[cur_slot] = jnp.broadcast_to(thr, (8, QB))

    qt = qt_ref[0]
    thr_prv = thr_ref[prv_slot][0:1]

    def tiles(c, near, thr, fn):
        off = pl.multiple_of(c * ATT_CHUNK, ATT_CHUNK)
        sc = jnp.dot(k_ref[pl.ds(off, ATT_CHUNK), :], qt, preferred_element_type=F32)
        sel = score_ref[prv_slot, pl.ds(off, ATT_CHUNK), :] >= thr
        blocks = []
        for sb in range(sub):
            delta = jnp.clip(prv - (c * sub + sb), 0, FAR_DELTA)
            sel_sb = sel[sb * QB:(sb + 1) * QB]
            row = []
            for h in range(DSA_HEADS):
                t = sc[sb * QB:(sb + 1) * QB, h * QB:(h + 1) * QB]
                if near:
                    t = t + bias_ref[delta, h]
                row.append(fn(t, sel_sb))
            blocks.append(jnp.concatenate(row, axis=1))
        return jnp.concatenate(blocks, axis=0)

    def bounded_pair(near, pair, thr):
        pvs = []
        for half in range(ATT_PAIR):
            c = pair * ATT_PAIR + half
            p = tiles(c, near, thr, lambda t, sel: jnp.where(sel, jnp.exp2(t), 0.0))
            pvs.append(jnp.dot(vt_ref[c], p.astype(BF16), preferred_element_type=F32))
        return sum(pvs[1:], pvs[0])

    n_far = jnp.maximum(prv - (FAR_DELTA - 1), 0) // (sub * ATT_PAIR)
    acc0 = jnp.zeros((VT_ROWS, lanes), F32)

    def finish_bounded(acc):
        _store_gated(acc[:HEAD_DIM] / acc[HEAD_DIM:HEAD_DIM + 1], gate_ref, out_ref, DSA_HEADS)

    def fused_body(near, c, carry):
        acc, state = carry
        thr = jnp.where(c < n_prv, thr_prv, jnp.inf)
        acc = acc + bounded_pair(near, jnp.minimum(c, n_prv - 1), thr)
        for _ in range(SEARCH_PASSES):
            state = search_unit(state)
        return acc, state

    def fused_path():
        carry = (acc0, search_init)
        carry = lax.fori_loop(0, n_far, functools.partial(fused_body, False), carry)
        acc, state = lax.fori_loop(n_far, n_cur, functools.partial(fused_body, True), carry)
        store_threshold(state)
        finish_bounded(acc)
        return 0

    def general_body(c, carry):
        m, l, acc = carry
        t = tiles(c, True, thr_prv, lambda t, sel: jnp.where(sel, t, MASKED))
        m_new = jnp.maximum(m, jnp.max(t, axis=0, keepdims=True))
        alpha = jnp.exp2(m - m_new)
        p = jnp.exp2(t - m_new)
        l_new = alpha * l + jnp.sum(p, axis=0, keepdims=True)
        pv = jnp.dot(vt_ref[c], p.astype(BF16), preferred_element_type=F32)
        return m_new, l_new, acc * alpha + pv

    def separate_path():
        @pl.when(s < nb)
        def _():
            store_threshold(lax.fori_loop(0, SEARCH_PASSES * n_cur, lambda _, st: search_unit(st), search_init))

        @pl.when((s >= 1) & (bounded_ref[0] > 0))
        def _():
            acc = lax.fori_loop(0, n_far, lambda c, a: a + bounded_pair(False, c, thr_prv), acc0)
            finish_bounded(lax.fori_loop(n_far, n_prv, lambda c, a: a + bounded_pair(True, c, thr_prv), acc))

        @pl.when((s >= 1) & (bounded_ref[0] <= 0))
        def _():
            m0 = jnp.full((1, lanes), MASKED, F32)
            l0 = jnp.zeros((1, lanes), F32)
            _, l, acc = lax.fori_loop(0, n_prv * ATT_PAIR, general_body, (m0, l0, acc0))
            _store_gated(acc[:HEAD_DIM] / l, gate_ref, out_ref, DSA_HEADS)

        return 0

    lax.cond((bounded_ref[0] > 0) & (s >= 1) & (s < nb), fused_path, separate_path)


def _dsa_attention(bounded, qt, iqt, iwt, gate, k, ik, vt, bias):
    nb = qt.shape[0]
    T = k.shape[0]
    width = DSA_HEADS * HEAD_DIM

    def prv(s):
        return jnp.maximum(s - 1, 0)

    def cur(s):
        return jnp.minimum(s, nb - 1)

    return pl.pallas_call(
        _dsa_kernel,
        grid=(nb + 1,),
        in_specs=[pl.BlockSpec(memory_space=pltpu.SMEM),
                  pl.BlockSpec((1, QT_ROWS, DSA_HEADS * QB), lambda s: (prv(s), 0, 0)),
                  pl.BlockSpec((1, HEAD_DIM, IDX_HEADS * QB), lambda s: (cur(s), 0, 0)),
                  pl.BlockSpec((IDX_HEADS, QB), lambda s: (0, cur(s))),
                  pl.BlockSpec((QB, width), lambda s: (prv(s), 0)),
                  _const_spec(k.shape),
                  _const_spec(ik.shape),
                  _const_spec(vt.shape),
                  _const_spec(bias.shape)],
        out_specs=pl.BlockSpec((QB, width), lambda s: (prv(s), 0)),
        out_shape=jax.ShapeDtypeStruct((T, width), BF16),
        scratch_shapes=[pltpu.VMEM((2, T, QB), F32), pltpu.VMEM((2, 8, QB), F32)],
        compiler_params=_cparams(),
        name="dsa_attention",
    )(bounded, qt, iqt, iwt, gate, k, ik, vt, bias)


def _swa_kernel(bounded_ref, qt_ref, kp_ref, kc_ref, vp_ref, vc_ref, sink_ref, gate_ref, bias_ref, out_ref):
    i = pl.program_id(0)
    grp = SWA_Q_HEADS // SWA_KV_HEADS
    shape = (2 * QB, QB)
    c_idx = lax.broadcasted_iota(I32, shape, 0)
    dist = QB + lax.broadcasted_iota(I32, shape, 1) - c_idx
    mask = (dist >= 0) & (dist < QB) & ((c_idx >= QB) | (i > 0))

    def group(kv, bounded):
        kcat = jnp.concatenate([kp_ref[kv], kc_ref[kv]], axis=0)
        vcat = jnp.concatenate([vp_ref[kv], vc_ref[kv]], axis=1)
        qg = qt_ref[0, :, kv * grp * QB:(kv + 1) * grp * QB]
        s = jnp.dot(kcat, qg, preferred_element_type=F32)
        sink = sink_ref[:, kv * grp * QB:(kv + 1) * grp * QB]
        logits = [s[:, j * QB:(j + 1) * QB] + bias_ref[kv * grp + j] for j in range(grp)]
        if bounded:
            p = jnp.concatenate([jnp.where(mask, jnp.exp2(t), 0.0) for t in logits], axis=1)
            pv = jnp.dot(vcat, p.astype(BF16), preferred_element_type=F32)
            return pv[:HEAD_DIM] / (pv[HEAD_DIM:HEAD_DIM + 1] + jnp.exp2(sink))
        t = jnp.concatenate([jnp.where(mask, t, MASKED) for t in logits], axis=1)
        m = jnp.maximum(jnp.max(t, axis=0, keepdims=True), sink)
        p = jnp.exp2(t - m)
        l = jnp.sum(p, axis=0, keepdims=True) + jnp.exp2(sink - m)
        pv = jnp.dot(vcat, p.astype(BF16), preferred_element_type=F32)
        return pv[:HEAD_DIM] / l

    def path(bounded):
        return jnp.concatenate([group(kv, bounded) for kv in range(SWA_KV_HEADS)], axis=1)

    o_t = lax.cond(bounded_ref[0] > 0, functools.partial(path, True), functools.partial(path, False))
    _store_gated(o_t, gate_ref, out_ref, SWA_Q_HEADS)


def _swa_attention(bounded, qt, k_hm, vt_hm, sink_b, gate, bias):
    nb = qt.shape[0]
    T = k_hm.shape[1]
    width = SWA_Q_HEADS * HEAD_DIM

    def prev(i):
        return jnp.maximum(i - 1, 0)

    return pl.pallas_call(
        _swa_kernel,
        grid=(nb,),
        in_specs=[pl.BlockSpec(memory_space=pltpu.SMEM),
                  pl.BlockSpec((1, HEAD_DIM, SWA_Q_HEADS * QB), lambda i: (i, 0, 0)),
                  pl.BlockSpec((SWA_KV_HEADS, QB, HEAD_DIM), lambda i: (0, prev(i), 0)),
                  pl.BlockSpec((SWA_KV_HEADS, QB, HEAD_DIM), lambda i: (0, i, 0)),
                  pl.BlockSpec((SWA_KV_HEADS, VT_ROWS, QB), lambda i: (0, 0, prev(i))),
                  pl.BlockSpec((SWA_KV_HEADS, VT_ROWS, QB), lambda i: (0, 0, i)),
                  _const_spec((1, SWA_Q_HEADS * QB)),
                  pl.BlockSpec((QB, width), lambda i: (i, 0)),
                  _const_spec(bias.shape)],
        out_specs=pl.BlockSpec((QB, width), lambda i: (i, 0)),
        out_shape=jax.ShapeDtypeStruct((T, width), BF16),
        compiler_params=_cparams(),
        name="swa_attention",
    )(bounded, qt, k_hm, k_hm, vt_hm, vt_hm, sink_b, gate, bias)


def _pad_cols(w, width):
    return jnp.pad(w, ((0, 0), (0, width - w.shape[1])))


def _even_layer(x, norm_g, w_in, pool_w, pool_scale, q_gain, k_gain, w_out, rel_bias, dsa_bias):
    pool_width = pool_w.shape[0] * pool_w.shape[1]
    dsa_width = DSA_HEADS * HEAD_DIM
    idx_width = IDX_HEADS * HEAD_DIM
    sizes = (pool_width, pool_width, dsa_width, HEAD_DIM, HEAD_DIM, dsa_width, idx_width, HEAD_DIM, IDX_HEADS)
    offs = [0]
    for s in sizes:
        offs.append(offs[-1] + s)
    col = lambda j: w_in[:, offs[j]:offs[j + 1]]
    w_pool = jnp.concatenate([col(0), col(1)], axis=1).astype(BF16)
    w_q = col(2).astype(BF16)
    w_iq = col(6).astype(BF16)
    w_side = jnp.concatenate([col(5), _pad_cols(col(3), 128), _pad_cols(col(7), 128),
                              _pad_cols(jnp.concatenate([col(4), col(8)], axis=1), 128)], axis=1).astype(BF16)

    q_scale = HEAD_DIM ** -0.5 * LOG2E
    far = rel_bias[NUM_BUCKETS - 1, :DSA_HEADS].astype(F32) * LOG2E
    far_hi = far.astype(BF16)
    far_lo = (far - far_hi.astype(F32)).astype(BF16)
    extra = jnp.zeros((QT_ROWS - HEAD_DIM, DSA_HEADS * QB), BF16)
    extra = extra.at[0].set(jnp.repeat(far_hi, QB)).at[1].set(jnp.repeat(far_lo, QB))
    q_norm = jnp.sqrt(HEAD_DIM * jnp.max(q_gain.astype(F32) ** 2) * q_scale ** 2 + jnp.max(far ** 2))
    k_norm = jnp.sqrt(HEAD_DIM * jnp.max(k_gain.astype(F32) ** 2) + 2.0)
    near_max = jnp.max(jnp.abs(rel_bias[:, :DSA_HEADS] - rel_bias[NUM_BUCKETS - 1, :DSA_HEADS])) * LOG2E
    bound = 1.02 * q_norm * k_norm + near_max
    bounded = (bound < SAFE_LOG2_RANGE).astype(I32).reshape(1)

    py = _pool_mixer(x, norm_g, w_pool, pool_w.astype(BF16), pool_scale)
    qt = _head_t_proj(x, norm_g, w_q, q_gain, normalize=True, scale=q_scale, extra=extra)
    iqt = _head_t_proj(x, norm_g, w_iq, q_gain, normalize=False)
    gate, k, ik, vt, iwt = _dsa_side_proj(x, norm_g, w_side, k_gain, dsa_width,
                                          iw_scale=IDX_HEADS ** -0.5 * HEAD_DIM ** -0.5)
    dy = _dsa_attention(bounded, qt, iqt, iwt, gate, k, ik, vt, dsa_bias)
    w_o = w_out.astype(BF16)
    return _out_proj(x, [(py, w_o[:pool_width]), (dy, w_o[pool_width:])])


def _odd_layer(x, norm_g, w_in, q_gain, k_gain, sinks, w_out, rel_bias, swa_bias):
    q_width = SWA_Q_HEADS * HEAD_DIM
    kv_width = SWA_KV_HEADS * HEAD_DIM
    w_q = w_in[:, :q_width].astype(BF16)
    w_side = jnp.concatenate([w_in[:, q_width + 2 * kv_width:], w_in[:, q_width:q_width + 2 * kv_width]],
                             axis=1).astype(BF16)
    q_scale = HEAD_DIM ** -0.5 * LOG2E
    sink_b = jnp.repeat(sinks.astype(F32) * LOG2E, QB).reshape(1, SWA_Q_HEADS * QB)
    q_norm = jnp.sqrt(HEAD_DIM * jnp.max(q_gain.astype(F32) ** 2)) * q_scale
    k_norm = jnp.sqrt(HEAD_DIM * jnp.max(k_gain.astype(F32) ** 2))
    bound = jnp.maximum(1.02 * q_norm * k_norm + jnp.max(jnp.abs(rel_bias)) * LOG2E, jnp.max(jnp.abs(sink_b)))
    bounded = (bound < SAFE_LOG2_RANGE).astype(I32).reshape(1)

    qt = _head_t_proj(x, norm_g, w_q, q_gain, normalize=True, scale=q_scale)
    gate, k_hm, vt_hm = _swa_side_proj(x, norm_g, w_side, k_gain, q_width)
    go = _swa_attention(bounded, qt, k_hm, vt_hm, sink_b, gate, swa_bias)
    return _out_proj(x, [(go, w_out.astype(BF16))])


def kernel(x, rel_bias, even_norm, even_w_in, even_pool_w, even_pool_scale, even_q_gain, even_k_gain,
           even_w_out, odd_norm, odd_w_in, odd_q_gain, odd_k_gain, odd_sinks, odd_w_out):
    B, T, D = x.shape
    depth = even_norm.shape[0] + odd_norm.shape[0]
    dsa_bias = _bias_tiles(rel_bias, N_BIAS_TILES, DSA_HEADS, QB, QB, 0, minus_far=True)
    swa_bias = _bias_tiles(rel_bias, 1, SWA_Q_HEADS, 2 * QB, 0, QB)[0]
    outs = []
    for b in range(B):
        h = x.reshape(T, D) if B == 1 else x[b]
        for layer in range(depth):
            j = layer // 2
            if layer % 2 == 0:
                h = _even_layer(h, even_norm[j], even_w_in[j], even_pool_w[j], even_pool_scale[j],
                                even_q_gain[j], even_k_gain[j], even_w_out[j], rel_bias, dsa_bias)
            else:
                h = _odd_layer(h, odd_norm[j], odd_w_in[j], odd_q_gain[j], odd_k_gain[j],
                               odd_sinks[j], odd_w_out[j], rel_bias, swa_bias)
        outs.append(h)
    return outs[0].reshape(B, T, D) if B == 1 else jnp.stack(outs, axis=0)
```

```python
import functools

import jax
import jax.numpy as jnp
from jax import lax
from jax.experimental import pallas as pl
from jax.experimental.pallas import tpu as pltpu

F32 = jnp.float32
BF16 = jnp.bfloat16
I32 = jnp.int32

EPS = 1e-6
HEAD_DIM = 64
QB = 128
POOL_WINDOWS = (2, 4, 8, 16)
POOL_HALO = 16
DSA_HEADS = 16
IDX_HEADS = 16
DSA_TOPK = 256
SWA_Q_HEADS = 32
SWA_KV_HEADS = 4
NUM_BUCKETS = 32
IDX_CHUNK = 512
ATT_CHUNK = 256
ATT_PAIR = IDX_CHUNK // ATT_CHUNK
INT_MIN = -(2 ** 31)
FINITE_MIN_KEY = -0x7F800000
SEARCH_PASSES = 32
MASKED = -1e30
LOG2E = 1.4426950408889634
QT_ROWS = 128
VT_ROWS = 80
SAFE_LOG2_RANGE = 60.0
TM = 256
VMEM_LIMIT = 56 * 1024 * 1024


def _bucket_starts():
    max_exact = NUM_BUCKETS // 2
    starts = list(range(max_exact + 1))
    n = max_exact
    for b in range(max_exact + 1, NUM_BUCKETS):
        while n ** 16 < max_exact ** 16 * 64 ** (b - max_exact):
            n += 1
        starts.append(n)
    return tuple(starts)


BUCKET_START = _bucket_starts()
FAR_DELTA = -(-(BUCKET_START[-1] + QB - 1) // QB)
N_BIAS_TILES = FAR_DELTA + 1


def _cparams(n_grid=1):
    return pltpu.CompilerParams(dimension_semantics=("arbitrary",) * n_grid,
                                vmem_limit_bytes=VMEM_LIMIT)


def _const_spec(shape):
    return pl.BlockSpec(shape, lambda i: (0,) * len(shape), pipeline_mode=pl.Buffered(1))


def _rms_rows_bf16(x, g):
    ms = jnp.mean(x * x, axis=-1, keepdims=True)
    return (x * lax.rsqrt(ms + EPS) * g).astype(BF16)


def _silu(x):
    return x / (1.0 + jnp.exp(-x))


def _bias_tiles_kernel(rb_ref, out_ref, *, n_heads, rows, base_step, base0, minus_far):
    base = base0 + pl.program_id(0) * base_step
    shape = (rows, QB)
    dist = base + lax.broadcasted_iota(I32, shape, 1) - lax.broadcasted_iota(I32, shape, 0)
    at_least = [dist >= BUCKET_START[b] for b in range(1, NUM_BUCKETS)]
    for h in range(n_heads):
        val = jnp.full(shape, rb_ref[0, h], F32)
        for b in range(1, NUM_BUCKETS):
            val = jnp.where(at_least[b - 1], rb_ref[b, h], val)
        if minus_far:
            val = val - rb_ref[NUM_BUCKETS - 1, h]
        out_ref[0, h] = val * LOG2E


def _bias_tiles(rel_bias, n_tiles, n_heads, rows, base_step, base0, minus_far=False):
    kern = functools.partial(_bias_tiles_kernel, n_heads=n_heads, rows=rows, base_step=base_step, base0=base0,
                             minus_far=minus_far)
    return pl.pallas_call(
        kern,
        grid=(n_tiles,),
        in_specs=[pl.BlockSpec(memory_space=pltpu.SMEM)],
        out_specs=pl.BlockSpec((1, n_heads, rows, QB), lambda t: (t, 0, 0, 0)),
        out_shape=jax.ShapeDtypeStruct((n_tiles, n_heads, rows, QB), F32),
        compiler_params=_cparams(),
        name="bias_tiles",
    )(rel_bias)


def _head_t_proj_kernel(x_ref, g_ref, w_ref, gain_ref, *rest, n_heads, normalize, scale):
    out_ref = rest[-1]
    if len(rest) == 2:
        for b in range(TM // QB):
            out_ref[b, HEAD_DIM:, :] = rest[0][...]
    xn = _rms_rows_bf16(x_ref[...], g_ref[...])
    h = jnp.dot(xn, w_ref[...], preferred_element_type=F32)
    gain = gain_ref[...]
    for b in range(TM // QB):
        rows = h[b * QB:(b + 1) * QB]
        for p in range(n_heads // 2):
            pair = rows[:, p * 128:(p + 1) * 128].T
            for hh in range(2):
                t = pair[hh * HEAD_DIM:(hh + 1) * HEAD_DIM]
                if normalize:
                    ms = jnp.mean(t * t, axis=0, keepdims=True)
                    t = t * lax.rsqrt(ms + EPS) * gain * scale
                head = 2 * p + hh
                out_ref[b, :HEAD_DIM, head * QB:(head + 1) * QB] = t.astype(BF16)


def _head_t_proj(x, g, w, gain, *, normalize, scale=1.0, extra=None):
    T, D = x.shape
    n_heads = w.shape[1] // HEAD_DIM
    gain_b = jnp.broadcast_to(gain.reshape(HEAD_DIM, 1), (HEAD_DIM, QB)).astype(F32)
    kern = functools.partial(_head_t_proj_kernel, n_heads=n_heads, normalize=normalize, scale=scale)
    args = [x, g.reshape(1, D), w, gain_b]
    in_specs = [pl.BlockSpec((TM, D), lambda i: (i, 0)),
                _const_spec((1, D)),
                _const_spec(w.shape),
                _const_spec((HEAD_DIM, QB))]
    rows = HEAD_DIM
    if extra is not None:
        args.append(extra)
        in_specs.append(_const_spec(extra.shape))
        rows += extra.shape[0]
    return pl.pallas_call(
        kern,
        grid=(T // TM,),
        in_specs=in_specs,
        out_specs=pl.BlockSpec((TM // QB, rows, n_heads * QB), lambda i: (i, 0, 0)),
        out_shape=jax.ShapeDtypeStruct((T // QB, rows, n_heads * QB), BF16),
        compiler_params=_cparams(),
        name="head_t_proj",
    )(*args)


def _pool_kernel(x_ref, g_ref, w_ref, pw_ref, ps_ref, out_ref, halo_ref):
    i = pl.program_id(0)
    width = out_ref.shape[1]
    gc = width // len(POOL_WINDOWS)
    xn = _rms_rows_bf16(x_ref[...], g_ref[...])
    h = jnp.dot(xn, w_ref[...], preferred_element_type=F32)
    pin = h[:, :width]
    gate = h[:, width:]

    @pl.when(i == 0)
    def _():
        halo_ref[...] = jnp.zeros_like(halo_ref)

    ext = jnp.concatenate([halo_ref[...], pin], axis=0)
    halo_ref[...] = pin[TM - POOL_HALO:]
    pos = i * TM + lax.broadcasted_iota(I32, (TM, gc), 0)
    outs = []
    for grp, win in enumerate(POOL_WINDOWS):
        e = ext[:, grp * gc:(grp + 1) * gc]
        s = e
        span = 1
        while span < win:
            s = s + pltpu.roll(s, span, 0)
            span *= 2
        s = s[POOL_HALO:]
        a = pin[:, grp * gc:(grp + 1) * gc]
        cnt = jnp.minimum(pos + 1, win).astype(F32)
        pooled = s / cnt - a
        y = jnp.dot(pooled.astype(BF16), pw_ref[grp], preferred_element_type=F32)
        outs.append(y * ps_ref[:, grp * gc:(grp + 1) * gc])
    py = jnp.concatenate(outs, axis=1)
    out_ref[...] = (_silu(gate) * py).astype(BF16)


def _pool_mixer(x, g, w, pool_w, pool_scale):
    T, D = x.shape
    width = w.shape[1] // 2
    return pl.pallas_call(
        _pool_kernel,
        grid=(T // TM,),
        in_specs=[pl.BlockSpec((TM, D), lambda i: (i, 0)),
                  _const_spec((1, D)),
                  _const_spec(w.shape),
                  _const_spec(pool_w.shape),
                  _const_spec((1, width))],
        out_specs=pl.BlockSpec((TM, width), lambda i: (i, 0)),
        out_shape=jax.ShapeDtypeStruct((T, width), BF16),
        scratch_shapes=[pltpu.VMEM((POOL_HALO, width), F32)],
        compiler_params=_cparams(),
        name="pool_mixer",
    )(x, g.reshape(1, D), w, pool_w, pool_scale.reshape(1, width))


def _dsa_side_kernel(x_ref, g_ref, w_ref, kg_ref, gate_ref, k_ref, ik_ref, vt_ref, iwt_ref, *, gate_w, iw_scale):
    xn = _rms_rows_bf16(x_ref[...], g_ref[...])
    h = jnp.dot(xn, w_ref[...], preferred_element_type=F32)
    gate_ref[...] = _silu(h[:, :gate_w]).astype(BF16)
    k = h[:, gate_w:gate_w + 128]
    ms = jnp.sum(k * k, axis=-1, keepdims=True) * (1.0 / HEAD_DIM)
    kn = k * lax.rsqrt(ms + EPS) * kg_ref[...]
    lane = lax.broadcasted_iota(I32, kn.shape, 1)
    k_ref[...] = jnp.where((lane == HEAD_DIM) | (lane == HEAD_DIM + 1), 1.0, kn).astype(BF16)
    ik_ref[...] = h[:, gate_w + 128:gate_w + 128 + HEAD_DIM].astype(BF16)
    vw = h[:, gate_w + 256:gate_w + 384].T
    row = lax.broadcasted_iota(I32, (VT_ROWS - HEAD_DIM, TM), 0)
    vt_ref[0] = jnp.concatenate([vw[:HEAD_DIM], jnp.where(row == 0, 1.0, 0.0)], axis=0).astype(BF16)
    iwt_ref[...] = vw[HEAD_DIM:HEAD_DIM + IDX_HEADS] * iw_scale


def _dsa_side_proj(x, g, w, k_gain, gate_w, iw_scale):
    T, D = x.shape
    assert TM == ATT_CHUNK
    kern = functools.partial(_dsa_side_kernel, gate_w=gate_w, iw_scale=iw_scale)
    return pl.pallas_call(
        kern,
        grid=(T // TM,),
        in_specs=[pl.BlockSpec((TM, D), lambda i: (i, 0)),
                  _const_spec((1, D)),
                  _const_spec(w.shape),
                  _const_spec((1, QT_ROWS))],
        out_specs=[pl.BlockSpec((TM, gate_w), lambda i: (i, 0)),
                   pl.BlockSpec((TM, QT_ROWS), lambda i: (i, 0)),
                   pl.BlockSpec((TM, HEAD_DIM), lambda i: (i, 0)),
                   pl.BlockSpec((1, VT_ROWS, ATT_CHUNK), lambda i: (i, 0, 0)),
                   pl.BlockSpec((IDX_HEADS, TM), lambda i: (0, i))],
        out_shape=[jax.ShapeDtypeStruct((T, gate_w), BF16),
                   jax.ShapeDtypeStruct((T, QT_ROWS), BF16),
                   jax.ShapeDtypeStruct((T, HEAD_DIM), BF16),
                   jax.ShapeDtypeStruct((T // ATT_CHUNK, VT_ROWS, ATT_CHUNK), BF16),
                   jax.ShapeDtypeStruct((IDX_HEADS, T), F32)],
        compiler_params=_cparams(),
        name="dsa_side_proj",
    )(x, g.reshape(1, D), w, _pad_cols(k_gain.reshape(1, HEAD_DIM), QT_ROWS))


def _swa_side_kernel(x_ref, g_ref, w_ref, kg_ref, gate_ref, k_ref, vt_ref, *, gate_w):
    xn = _rms_rows_bf16(x_ref[...], g_ref[...])
    h = jnp.dot(xn, w_ref[...], preferred_element_type=F32)
    gate_ref[...] = _silu(h[:, :gate_w]).astype(BF16)
    kv_w = SWA_KV_HEADS * HEAD_DIM
    for hd in range(SWA_KV_HEADS):
        k = h[:, gate_w + hd * HEAD_DIM:gate_w + (hd + 1) * HEAD_DIM]
        ms = jnp.mean(k * k, axis=-1, keepdims=True)
        k_ref[hd] = (k * lax.rsqrt(ms + EPS) * kg_ref[...]).astype(BF16)
    row = lax.broadcasted_iota(I32, (VT_ROWS - HEAD_DIM, TM), 0)
    ones_rows = jnp.where(row == 0, 1.0, 0.0)
    for p in range(SWA_KV_HEADS // 2):
        pair = h[:, gate_w + kv_w + p * 128:gate_w + kv_w + (p + 1) * 128].T
        vt_ref[2 * p] = jnp.concatenate([pair[:HEAD_DIM], ones_rows], axis=0).astype(BF16)
        vt_ref[2 * p + 1] = jnp.concatenate([pair[HEAD_DIM:], ones_rows], axis=0).astype(BF16)


def _swa_side_proj(x, g, w, k_gain, gate_w):
    T, D = x.shape
    kern = functools.partial(_swa_side_kernel, gate_w=gate_w)
    return pl.pallas_call(
        kern,
        grid=(T // TM,),
        in_specs=[pl.BlockSpec((TM, D), lambda i: (i, 0)),
                  _const_spec((1, D)),
                  _const_spec(w.shape),
                  _const_spec((1, HEAD_DIM))],
        out_specs=[pl.BlockSpec((TM, gate_w), lambda i: (i, 0)),
                   pl.BlockSpec((SWA_KV_HEADS, TM, HEAD_DIM), lambda i: (0, i, 0)),
                   pl.BlockSpec((SWA_KV_HEADS, VT_ROWS, TM), lambda i: (0, 0, i))],
        out_shape=[jax.ShapeDtypeStruct((T, gate_w), BF16),
                   jax.ShapeDtypeStruct((SWA_KV_HEADS, T, HEAD_DIM), BF16),
                   jax.ShapeDtypeStruct((SWA_KV_HEADS, VT_ROWS, T), BF16)],
        compiler_params=_cparams(),
        name="swa_side_proj",
    )(x, g.reshape(1, D), w, k_gain.reshape(1, HEAD_DIM))


def _out_proj_kernel(*refs, n_terms):
    x_ref = refs[0]
    out_ref = refs[-1]
    acc = x_ref[...]
    for t in range(n_terms):
        acc = acc + jnp.dot(refs[1 + 2 * t][...], refs[2 + 2 * t][...], preferred_element_type=F32)
    out_ref[...] = acc


def _out_proj(x, terms):
    T, D = x.shape
    args = [x]
    in_specs = [pl.BlockSpec((TM, D), lambda i: (i, 0))]
    for a, w in terms:
        args += [a, w]
        in_specs += [pl.BlockSpec((TM, a.shape[1]), lambda i: (i, 0)), _const_spec(w.shape)]
    return pl.pallas_call(
        functools.partial(_out_proj_kernel, n_terms=len(terms)),
        grid=(T // TM,),
        in_specs=in_specs,
        out_specs=pl.BlockSpec((TM, D), lambda i: (i, 0)),
        out_shape=jax.ShapeDtypeStruct((T, D), F32),
        compiler_params=_cparams(),
        name="out_proj",
    )(*args)


def _store_gated(o_t, gate_ref, out_ref, n_heads):
    for p in range(n_heads // 2):
        pair = jnp.concatenate([o_t[:, (2 * p) * QB:(2 * p + 1) * QB],
                                o_t[:, (2 * p + 1) * QB:(2 * p + 2) * QB]], axis=0)
        g = gate_ref[:, p * 128:(p + 1) * 128].astype(F32)
        out_ref[:, p * 128:(p + 1) * 128] = (g * pair.T).astype(BF16)


def _dsa_kernel(bounded_ref, qt_ref, iqt0_ref, iwt0_ref, iqt_ref, iwt_ref, gate_ref, k_ref, ik_ref, vt_ref, bias_ref,
                out_ref, score_ref, thr_ref):
    s = pl.program_id(0)
    nb = pl.num_programs(0) - 1
    lanes = DSA_HEADS * QB
    blocks_per_chunk = IDX_CHUNK // QB
    sub = ATT_CHUNK // QB
    nxt = s + 1
    cur = s
    prv = jnp.maximum(s - 1, 0)
    n_nxt = nxt // blocks_per_chunk + 1
    n_cur = cur // blocks_per_chunk + 1
    n_prv = prv // blocks_per_chunk + 1
    nxt_slot = nxt % 3
    cur_slot = cur % 3
    prv_slot = prv % 3
    krow = lax.broadcasted_iota(I32, (IDX_CHUNK, QB), 0)
    lane_pos = lax.broadcasted_iota(I32, (IDX_CHUNK, QB), 1)

    def index_chunk(c, block, slot, iqt, iwt):
        off = pl.multiple_of(c * IDX_CHUNK, IDX_CHUNK)
        sc = jnp.dot(ik_ref[pl.ds(off, IDX_CHUNK), :], iqt, preferred_element_type=F32)
        score = jnp.zeros((IDX_CHUNK, QB), F32)
        for h in range(IDX_HEADS):
            score = score + jnp.maximum(sc[:, h * QB:(h + 1) * QB], 0.0) * iwt[h:h + 1, :]
        causal = off + krow <= block * QB + lane_pos
        score_ref[slot, pl.ds(off, IDX_CHUNK), :] = jnp.where(causal, score, -jnp.inf)

    def index_chunks(lo, hi, block, slot, iqt_r, iwt_r):
        iqt = iqt_r[0]
        iwt = iwt_r[...]

        def body(c, carry):
            index_chunk(c, block, slot, iqt, iwt)
            return carry

        lax.fori_loop(lo, hi, body, 0)

    def key_to_float(key):
        return lax.bitcast_convert_type(jnp.where(key < 0, key ^ 0x7FFFFFFF, key), F32)

    def search_unit(state):
        base, bit, acc, ch = state
        cand = key_to_float(base + bit)
        off = pl.multiple_of(ch * IDX_CHUNK, IDX_CHUNK)
        hit = jnp.where(score_ref[cur_slot, pl.ds(off, IDX_CHUNK), :] >= cand, 1, 0).astype(I32)
        acc = acc + jnp.sum(hit.reshape(IDX_CHUNK // 8, 8, QB), axis=0)
        last = ch + 1 == n_cur
        accept = jnp.sum(acc, axis=0, keepdims=True) >= DSA_TOPK
        base = jnp.where(last & accept, base + bit, base)
        acc = jnp.where(last, 0, acc)
        bit = jnp.where(last, lax.shift_right_logical(bit, 1), bit)
        ch = jnp.where(last, 0, ch + 1)
        return base, bit, acc, ch

    search_init = (jnp.full((1, QB), INT_MIN, I32), jnp.int32(INT_MIN), jnp.zeros((8, QB), I32), jnp.int32(0))

    def store_threshold(state):
        thr = key_to_float(jnp.maximum(state[0], FINITE_MIN_KEY))
        thr_ref[cur % 2] = jnp.broadcast_to(thr, (8, QB))

    qt = qt_ref[0]
    thr_prv = thr_ref[prv % 2][0:1]

    def tiles(c, near, thr, fn):
        off = pl.multiple_of(c * ATT_CHUNK, ATT_CHUNK)
        sc = jnp.dot(k_ref[pl.ds(off, ATT_CHUNK), :], qt, preferred_element_type=F32)
        sel = score_ref[prv_slot, pl.ds(off, ATT_CHUNK), :] >= thr
        blocks = []
        for sb in range(sub):
            delta = jnp.clip(prv - (c * sub + sb), 0, FAR_DELTA)
            sel_sb = sel[sb * QB:(sb + 1) * QB]
            row = []
            for h in range(DSA_HEADS):
                t = sc[sb * QB:(sb + 1) * QB, h * QB:(h + 1) * QB]
                if near:
                    t = t + bias_ref[delta, h]
                row.append(fn(t, sel_sb))
            blocks.append(jnp.concatenate(row, axis=1))
        return jnp.concatenate(blocks, axis=0)

    def bounded_pair(near, pair, thr):
        pvs = []
        for half in range(ATT_PAIR):
            c = pair * ATT_PAIR + half
            p = tiles(c, near, thr, lambda t, sel: jnp.where(sel, jnp.exp2(t), 0.0))
            pvs.append(jnp.dot(vt_ref[c], p.astype(BF16), preferred_element_type=F32))
        return sum(pvs[1:], pvs[0])

    n_far = jnp.maximum(prv - (FAR_DELTA - 1), 0) // (sub * ATT_PAIR)
    acc0 = jnp.zeros((VT_ROWS, lanes), F32)

    def finish_bounded(acc):
        _store_gated(acc[:HEAD_DIM] / acc[HEAD_DIM:HEAD_DIM + 1], gate_ref, out_ref, DSA_HEADS)

    def fused_path():
        iqt = iqt_ref[0]
        iwt = iwt_ref[...]

        def body(near, c, carry):
            acc, state = carry
            thr = jnp.where(c < n_prv, thr_prv, jnp.inf)
            acc = acc + bounded_pair(near, jnp.minimum(c, n_prv - 1), thr)
            for _ in range(SEARCH_PASSES):
                state = search_unit(state)
            index_chunk(c, nxt, nxt_slot, iqt, iwt)
            return acc, state

        carry = (acc0, search_init)
        carry = lax.fori_loop(0, n_far, functools.partial(body, False), carry)
        acc, state = lax.fori_loop(n_far, n_cur, functools.partial(body, True), carry)
        store_threshold(state)
        finish_bounded(acc)
        index_chunks(n_cur, jnp.where(nxt < nb, n_nxt, n_cur), nxt, nxt_slot, iqt_ref, iwt_ref)
        return 0

    def general_body(c, carry):
        m, l, acc = carry
        t = tiles(c, True, thr_prv, lambda t, sel: jnp.where(sel, t, MASKED))
        m_new = jnp.maximum(m, jnp.max(t, axis=0, keepdims=True))
        alpha = jnp.exp2(m - m_new)
        p = jnp.exp2(t - m_new)
        l_new = alpha * l + jnp.sum(p, axis=0, keepdims=True)
        pv = jnp.dot(vt_ref[c], p.astype(BF16), preferred_element_type=F32)
        return m_new, l_new, acc * alpha + pv

    def separate_path():
        @pl.when(s == 0)
        def _():
            index_chunks(0, n_cur, cur, cur_slot, iqt0_ref, iwt0_ref)

        @pl.when(nxt < nb)
        def _():
            index_chunks(0, n_nxt, nxt, nxt_slot, iqt_ref, iwt_ref)

        @pl.when(s < nb)
        def _():
            store_threshold(lax.fori_loop(0, SEARCH_PASSES * n_cur, lambda _, st: search_unit(st), search_init))

        @pl.when((s >= 1) & (bounded_ref[0] > 0))
        def _():
            acc = lax.fori_loop(0, n_far, lambda c, a: a + bounded_pair(False, c, thr_prv), acc0)
            finish_bounded(lax.fori_loop(n_far, n_prv, lambda c, a: a + bounded_pair(True, c, thr_prv), acc))

        @pl.when((s >= 1) & (bounded_ref[0] <= 0))
        def _():
            m0 = jnp.full((1, lanes), MASKED, F32)
            l0 = jnp.zeros((1, lanes), F32)
            _, l, acc = lax.fori_loop(0, n_prv * ATT_PAIR, general_body, (m0, l0, acc0))
            _store_gated(acc[:HEAD_DIM] / l, gate_ref, out_ref, DSA_HEADS)

        return 0

    lax.cond((bounded_ref[0] > 0) & (s >= 1) & (s < nb), fused_path, separate_path)


def _dsa_attention(bounded, qt, iqt, iwt, gate, k, ik, vt, bias):
    nb = qt.shape[0]
    T = k.shape[0]
    width = DSA_HEADS * HEAD_DIM

    def prv(s):
        return jnp.maximum(s - 1, 0)

    def nxt(s):
        return jnp.minimum(s + 1, nb - 1)

    return pl.pallas_call(
        _dsa_kernel,
        grid=(nb + 1,),
        in_specs=[pl.BlockSpec(memory_space=pltpu.SMEM),
                  pl.BlockSpec((1, QT_ROWS, DSA_HEADS * QB), lambda s: (prv(s), 0, 0)),
                  pl.BlockSpec((1, HEAD_DIM, IDX_HEADS * QB), lambda s: (0, 0, 0)),
                  pl.BlockSpec((IDX_HEADS, QB), lambda s: (0, 0)),
                  pl.BlockSpec((1, HEAD_DIM, IDX_HEADS * QB), lambda s: (nxt(s), 0, 0)),
                  pl.BlockSpec((IDX_HEADS, QB), lambda s: (0, nxt(s))),
                  pl.BlockSpec((QB, width), lambda s: (prv(s), 0)),
                  _const_spec(k.shape),
                  _const_spec(ik.shape),
                  _const_spec(vt.shape),
                  _const_spec(bias.shape)],
        out_specs=pl.BlockSpec((QB, width), lambda s: (prv(s), 0)),
        out_shape=jax.ShapeDtypeStruct((T, width), BF16),
        scratch_shapes=[pltpu.VMEM((3, T, QB), F32), pltpu.VMEM((2, 8, QB), F32)],
        compiler_params=_cparams(),
        name="dsa_attention",
    )(bounded, qt, iqt, iwt, iqt, iwt, gate, k, ik, vt, bias)


def _swa_kernel(bounded_ref, qt_ref, kp_ref, kc_ref, vp_ref, vc_ref, sink_ref, gate_ref, bias_ref, out_ref):
    i = pl.program_id(0)
    grp = SWA_Q_HEADS // SWA_KV_HEADS
    shape = (2 * QB, QB)
    c_idx = lax.broadcasted_iota(I32, shape, 0)
    dist = QB + lax.broadcasted_iota(I32, shape, 1) - c_idx
    mask = (dist >= 0) & (dist < QB) & ((c_idx >= QB) | (i > 0))

    def group(kv, bounded):
        kcat = jnp.concatenate([kp_ref[kv], kc_ref[kv]], axis=0)
        vcat = jnp.concatenate([vp_ref[kv], vc_ref[kv]], axis=1)
        qg = qt_ref[0, :, kv * grp * QB:(kv + 1) * grp * QB]
        s = jnp.dot(kcat, qg, preferred_element_type=F32)
        sink = sink_ref[:, kv * grp * QB:(kv + 1) * grp * QB]
        logits = [s[:, j * QB:(j + 1) * QB] + bias_ref[kv * grp + j] for j in range(grp)]
        if bounded:
            p = jnp.concatenate([jnp.where(mask, jnp.exp2(t), 0.0) for t in logits], axis=1)
            pv = jnp.dot(vcat, p.astype(BF16), preferred_element_type=F32)
            return pv[:HEAD_DIM] / (pv[HEAD_DIM:HEAD_DIM + 1] + jnp.exp2(sink))
        t = jnp.concatenate([jnp.where(mask, t, MASKED) for t in logits], axis=1)
        m = jnp.maximum(jnp.max(t, axis=0, keepdims=True), sink)
        p = jnp.exp2(t - m)
        l = jnp.sum(p, axis=0, keepdims=True) + jnp.exp2(sink - m)
        pv = jnp.dot(vcat, p.astype(BF16), preferred_element_type=F32)
        return pv[:HEAD_DIM] / l

    def path(bounded):
        return jnp.concatenate([group(kv, bounded) for kv in range(SWA_KV_HEADS)], axis=1)

    o_t = lax.cond(bounded_ref[0] > 0, functools.partial(path, True), functools.partial(path, False))
    _store_gated(o_t, gate_ref, out_ref, SWA_Q_HEADS)


def _swa_attention(bounded, qt, k_hm, vt_hm, sink_b, gate, bias):
    nb = qt.shape[0]
    T = k_hm.shape[1]
    width = SWA_Q_HEADS * HEAD_DIM

    def prev(i):
        return jnp.maximum(i - 1, 0)

    return pl.pallas_call(
        _swa_kernel,
        grid=(nb,),
        in_specs=[pl.BlockSpec(memory_space=pltpu.SMEM),
                  pl.BlockSpec((1, HEAD_DIM, SWA_Q_HEADS * QB), lambda i: (i, 0, 0)),
                  pl.BlockSpec((SWA_KV_HEADS, QB, HEAD_DIM), lambda i: (0, prev(i), 0)),
                  pl.BlockSpec((SWA_KV_HEADS, QB, HEAD_DIM), lambda i: (0, i, 0)),
                  pl.BlockSpec((SWA_KV_HEADS, VT_ROWS, QB), lambda i: (0, 0, prev(i))),
                  pl.BlockSpec((SWA_KV_HEADS, VT_ROWS, QB), lambda i: (0, 0, i)),
                  _const_spec((1, SWA_Q_HEADS * QB)),
                  pl.BlockSpec((QB, width), lambda i: (i, 0)),
                  _const_spec(bias.shape)],
        out_specs=pl.BlockSpec((QB, width), lambda i: (i, 0)),
        out_shape=jax.ShapeDtypeStruct((T, width), BF16),
        compiler_params=_cparams(),
        name="swa_attention",
    )(bounded, qt, k_hm, k_hm, vt_hm, vt_hm, sink_b, gate, bias)


def _pad_cols(w, width):
    return jnp.pad(w, ((0, 0), (0, width - w.shape[1])))


def _even_layer(x, norm_g, w_in, pool_w, pool_scale, q_gain, k_gain, w_out, rel_bias, dsa_bias):
    pool_width = pool_w.shape[0] * pool_w.shape[1]
    dsa_width = DSA_HEADS * HEAD_DIM
    idx_width = IDX_HEADS * HEAD_DIM
    sizes = (pool_width, pool_width, dsa_width, HEAD_DIM, HEAD_DIM, dsa_width, idx_width, HEAD_DIM, IDX_HEADS)
    offs = [0]
    for s in sizes:
        offs.append(offs[-1] + s)
    col = lambda j: w_in[:, offs[j]:offs[j + 1]]
    w_pool = jnp.concatenate([col(0), col(1)], axis=1).astype(BF16)
    w_q = col(2).astype(BF16)
    w_iq = col(6).astype(BF16)
    w_side = jnp.concatenate([col(5), _pad_cols(col(3), 128), _pad_cols(col(7), 128),
                              _pad_cols(jnp.concatenate([col(4), col(8)], axis=1), 128)], axis=1).astype(BF16)

    q_scale = HEAD_DIM ** -0.5 * LOG2E
    far = rel_bias[NUM_BUCKETS - 1, :DSA_HEADS].astype(F32) * LOG2E
    far_hi = far.astype(BF16)
    far_lo = (far - far_hi.astype(F32)).astype(BF16)
    extra = jnp.zeros((QT_ROWS - HEAD_DIM, DSA_HEADS * QB), BF16)
    extra = extra.at[0].set(jnp.repeat(far_hi, QB)).at[1].set(jnp.repeat(far_lo, QB))
    q_norm = jnp.sqrt(HEAD_DIM * jnp.max(q_gain.astype(F32) ** 2) * q_scale ** 2 + jnp.max(far ** 2))
    k_norm = jnp.sqrt(HEAD_DIM * jnp.max(k_gain.astype(F32) ** 2) + 2.0)
    near_max = jnp.max(jnp.abs(rel_bias[:, :DSA_HEADS] - rel_bias[NUM_BUCKETS - 1, :DSA_HEADS])) * LOG2E
    bound = 1.02 * q_norm * k_norm + near_max
    bounded = (bound < SAFE_LOG2_RANGE).astype(I32).reshape(1)

    py = _pool_mixer(x, norm_g, w_pool, pool_w.astype(BF16), pool_scale)
    qt = _head_t_proj(x, norm_g, w_q, q_gain, normalize=True, scale=q_scale, extra=extra)
    iqt = _head_t_proj(x, norm_g, w_iq, q_gain, normalize=False)
    gate, k, ik, vt, iwt = _dsa_side_proj(x, norm_g, w_side, k_gain, dsa_width,
                                          iw_scale=IDX_HEADS ** -0.5 * HEAD_DIM ** -0.5)
    dy = _dsa_attention(bounded, qt, iqt, iwt, gate, k, ik, vt, dsa_bias)
    w_o = w_out.astype(BF16)
    return _out_proj(x, [(py, w_o[:pool_width]), (dy, w_o[pool_width:])])


def _odd_layer(x, norm_g, w_in, q_gain, k_gain, sinks, w_out, rel_bias, swa_bias):
    q_width = SWA_Q_HEADS * HEAD_DIM
    kv_width = SWA_KV_HEADS * HEAD_DIM
    w_q = w_in[:, :q_width].astype(BF16)
    w_side = jnp.concatenate([w_in[:, q_width + 2 * kv_width:], w_in[:, q_width:q_width + 2 * kv_width]],
                             axis=1).astype(BF16)
    q_scale = HEAD_DIM ** -0.5 * LOG2E
    sink_b = jnp.repeat(sinks.astype(F32) * LOG2E, QB).reshape(1, SWA_Q_HEADS * QB)
    q_norm = jnp.sqrt(HEAD_DIM * jnp.max(q_gain.astype(F32) ** 2)) * q_scale
    k_norm = jnp.sqrt(HEAD_DIM * jnp.max(k_gain.astype(F32) ** 2))
    bound = jnp.maximum(1.02 * q_norm * k_norm + jnp.max(jnp.abs(rel_bias)) * LOG2E, jnp.max(jnp.abs(sink_b)))
    bounded = (bound < SAFE_LOG2_RANGE).astype(I32).reshape(1)

    qt = _head_t_proj(x, norm_g, w_q, q_gain, normalize=True, scale=q_scale)
    gate, k_hm, vt_hm = _swa_side_proj(x, norm_g, w_side, k_gain, q_width)
    go = _swa_attention(bounded, qt, k_hm, vt_hm, sink_b, gate, swa_bias)
    return _out_proj(x, [(go, w_out.astype(BF16))])


def kernel(x, rel_bias, even_norm, even_w_in, even_pool_w, even_pool_scale, even_q_gain, even_k_gain,
           even_w_out, odd_norm, odd_w_in, odd_q_gain, odd_k_gain, odd_sinks, odd_w_out):
    B, T, D = x.shape
    depth = even_norm.shape[0] + odd_norm.shape[0]
    dsa_bias = _bias_tiles(rel_bias, N_BIAS_TILES, DSA_HEADS, QB, QB, 0, minus_far=True)
    swa_bias = _bias_tiles(rel_bias, 1, SWA_Q_HEADS, 2 * QB, 0, QB)[0]
    outs = []
    for b in range(B):
        h = x.reshape(T, D) if B == 1 else x[b]
        for layer in range(depth):
            j = layer // 2
            if layer % 2 == 0:
                h = _even_layer(h, even_norm[j], even_w_in[j], even_pool_w[j], even_pool_scale[j],
                                even_q_gain[j], even_k_gain[j], even_w_out[j], rel_bias, dsa_bias)
            else:
                h = _odd_layer(h, odd_norm[j], odd_w_in[j], odd_q_gain[j], odd_k_gain[j],
                               odd_sinks[j], odd_w_out[j], rel_bias, swa_bias)
        outs.append(h)
    return outs[0].reshape(B, T, D) if B == 1 else jnp.stack(outs, axis=0)
```

```python
import functools

import jax
import jax.numpy as jnp
from jax import lax
from jax.experimental import pallas as pl
from jax.experimental.pallas import tpu as pltpu

F32 = jnp.float32
BF16 = jnp.bfloat16
I32 = jnp.int32

EPS = 1e-6
HEAD_DIM = 64
QB = 128
POOL_WINDOWS = (2, 4, 8, 16)
POOL_HALO = 16
DSA_HEADS = 16
IDX_HEADS = 16
DSA_TOPK = 256
SWA_Q_HEADS = 32
SWA_KV_HEADS = 4
NUM_BUCKETS = 32
IDX_CHUNK = 512
ATT_CHUNK = 256
ATT_PAIR = IDX_CHUNK // ATT_CHUNK
INT_MIN = -(2 ** 31)
FINITE_MIN_KEY = -0x7F800000
SEARCH_PASSES = 32
MASKED = -1e30
LOG2E = 1.4426950408889634
QT_ROWS = 128
VT_ROWS = 80
SAFE_LOG2_RANGE = 60.0
TM = 256
VMEM_LIMIT = 56 * 1024 * 1024


def _bucket_starts():
    max_exact = NUM_BUCKETS // 2
    starts = list(range(max_exact + 1))
    n = max_exact
    for b in range(max_exact + 1, NUM_BUCKETS):
        while n ** 16 < max_exact ** 16 * 64 ** (b - max_exact):
            n += 1
        starts.append(n)
    return tuple(starts)


BUCKET_START = _bucket_starts()
FAR_DELTA = -(-(BUCKET_START[-1] + QB - 1) // QB)
N_BIAS_TILES = FAR_DELTA + 1


def _cparams(n_grid=1):
    return pltpu.CompilerParams(dimension_semantics=("arbitrary",) * n_grid,
                                vmem_limit_bytes=VMEM_LIMIT)


def _const_spec(shape):
    return pl.BlockSpec(shape, lambda i: (0,) * len(shape), pipeline_mode=pl.Buffered(1))


def _rms_rows_bf16(x, g):
    ms = jnp.mean(x * x, axis=-1, keepdims=True)
    return (x * lax.rsqrt(ms + EPS) * g).astype(BF16)


def _silu(x):
    return x / (1.0 + jnp.exp(-x))


def _bias_tiles_kernel(rb_ref, out_ref, *, n_heads, rows, base_step, base0, minus_far):
    base = base0 + pl.program_id(0) * base_step
    shape = (rows, QB)
    dist = base + lax.broadcasted_iota(I32, shape, 1) - lax.broadcasted_iota(I32, shape, 0)
    at_least = [dist >= BUCKET_START[b] for b in range(1, NUM_BUCKETS)]
    for h in range(n_heads):
        val = jnp.full(shape, rb_ref[0, h], F32)
        for b in range(1, NUM_BUCKETS):
            val = jnp.where(at_least[b - 1], rb_ref[b, h], val)
        if minus_far:
            val = val - rb_ref[NUM_BUCKETS - 1, h]
        out_ref[0, h] = val * LOG2E


def _bias_tiles(rel_bias, n_tiles, n_heads, rows, base_step, base0, minus_far=False):
    kern = functools.partial(_bias_tiles_kernel, n_heads=n_heads, rows=rows, base_step=base_step, base0=base0,
                             minus_far=minus_far)
    return pl.pallas_call(
        kern,
        grid=(n_tiles,),
        in_specs=[pl.BlockSpec(memory_space=pltpu.SMEM)],
        out_specs=pl.BlockSpec((1, n_heads, rows, QB), lambda t: (t, 0, 0, 0)),
        out_shape=jax.ShapeDtypeStruct((n_tiles, n_heads, rows, QB), F32),
        compiler_params=_cparams(),
        name="bias_tiles",
    )(rel_bias)


def _store_heads_t(h, out_ref, gain_ref, *, n_heads, normalize, scale):
    for b in range(TM // QB):
        rows = h[b * QB:(b + 1) * QB]
        for p in range(n_heads // 2):
            pair = rows[:, p * 128:(p + 1) * 128].T
            for hh in range(2):
                t = pair[hh * HEAD_DIM:(hh + 1) * HEAD_DIM]
                if normalize:
                    ms = jnp.mean(t * t, axis=0, keepdims=True)
                    t = t * lax.rsqrt(ms + EPS) * gain_ref[...] * scale
                head = 2 * p + hh
                out_ref[b, :HEAD_DIM, head * QB:(head + 1) * QB] = t.astype(BF16)


def _pool_mix(i, pin, gate, pw_ref, ps_ref, out_ref, halo_ref):
    width = pin.shape[1]
    gc = width // len(POOL_WINDOWS)

    @pl.when(i == 0)
    def _():
        halo_ref[...] = jnp.zeros_like(halo_ref)

    ext = jnp.concatenate([halo_ref[...], pin], axis=0)
    halo_ref[...] = pin[TM - POOL_HALO:]
    pos = i * TM + lax.broadcasted_iota(I32, (TM, gc), 0)
    outs = []
    for grp, win in enumerate(POOL_WINDOWS):
        e = ext[:, grp * gc:(grp + 1) * gc]
        s = e
        span = 1
        while span < win:
            s = s + pltpu.roll(s, span, 0)
            span *= 2
        s = s[POOL_HALO:]
        a = pin[:, grp * gc:(grp + 1) * gc]
        cnt = jnp.minimum(pos + 1, win).astype(F32)
        pooled = s / cnt - a
        y = jnp.dot(pooled.astype(BF16), pw_ref[grp], preferred_element_type=F32)
        outs.append(y * ps_ref[:, grp * gc:(grp + 1) * gc])
    py = jnp.concatenate(outs, axis=1)
    out_ref[...] = (_silu(gate) * py).astype(BF16)


def _ones_row_pad(cols):
    row = lax.broadcasted_iota(I32, (VT_ROWS - HEAD_DIM, cols), 0)
    return jnp.where(row == 0, 1.0, 0.0)


def _even_proj_kernel(x_ref, g_ref, w_ref, pw_ref, ps_ref, qg_ref, extra_ref, kg_ref,
                      py_ref, qt_ref, iqt_ref, gate_ref, k_ref, ik_ref, vt_ref, iwt_ref, halo_ref,
                      *, pool_w, dsa_w, q_scale, iw_scale):
    i = pl.program_id(0)
    xn = _rms_rows_bf16(x_ref[...], g_ref[...])
    col = 0

    def proj(width):
        nonlocal col
        h = jnp.dot(xn, w_ref[:, col:col + width], preferred_element_type=F32)
        col += width
        return h

    h = proj(2 * pool_w)
    _pool_mix(i, h[:, :pool_w], h[:, pool_w:], pw_ref, ps_ref, py_ref, halo_ref)

    for b in range(TM // QB):
        qt_ref[b, HEAD_DIM:, :] = extra_ref[...]
    _store_heads_t(proj(dsa_w), qt_ref, qg_ref, n_heads=DSA_HEADS, normalize=True, scale=q_scale)
    _store_heads_t(proj(IDX_HEADS * HEAD_DIM), iqt_ref, qg_ref, n_heads=IDX_HEADS, normalize=False, scale=1.0)

    h = proj(dsa_w + 3 * 128)
    gate_ref[...] = _silu(h[:, :dsa_w]).astype(BF16)
    k = h[:, dsa_w:dsa_w + 128]
    ms = jnp.sum(k * k, axis=-1, keepdims=True) * (1.0 / HEAD_DIM)
    kn = k * lax.rsqrt(ms + EPS) * kg_ref[...]
    lane = lax.broadcasted_iota(I32, kn.shape, 1)
    k_ref[...] = jnp.where((lane == HEAD_DIM) | (lane == HEAD_DIM + 1), 1.0, kn).astype(BF16)
    ik_ref[...] = h[:, dsa_w + 128:dsa_w + 128 + HEAD_DIM].astype(BF16)
    vw = h[:, dsa_w + 256:dsa_w + 384].T
    vt_ref[0] = jnp.concatenate([vw[:HEAD_DIM], _ones_row_pad(TM)], axis=0).astype(BF16)
    iwt_ref[...] = vw[HEAD_DIM:HEAD_DIM + IDX_HEADS] * iw_scale


def _even_proj(x, g, w, pool_w, pool_scale, q_gain, extra, k_gain, *, pool_width, dsa_width, q_scale, iw_scale):
    T, D = x.shape
    assert TM == ATT_CHUNK
    nb = T // QB
    lanes = DSA_HEADS * QB
    gain_b = jnp.broadcast_to(q_gain.reshape(HEAD_DIM, 1), (HEAD_DIM, QB)).astype(F32)
    kern = functools.partial(_even_proj_kernel, pool_w=pool_width, dsa_w=dsa_width, q_scale=q_scale,
                             iw_scale=iw_scale)
    row_blk = lambda width: pl.BlockSpec((TM, width), lambda i: (i, 0))
    return pl.pallas_call(
        kern,
        grid=(T // TM,),
        in_specs=[row_blk(D),
                  _const_spec((1, D)),
                  _const_spec(w.shape),
                  _const_spec(pool_w.shape),
                  _const_spec((1, pool_width)),
                  _const_spec((HEAD_DIM, QB)),
                  _const_spec(extra.shape),
                  _const_spec((1, QT_ROWS))],
        out_specs=[row_blk(pool_width),
                   pl.BlockSpec((TM // QB, QT_ROWS, lanes), lambda i: (i, 0, 0)),
                   pl.BlockSpec((TM // QB, HEAD_DIM, lanes), lambda i: (i, 0, 0)),
                   row_blk(dsa_width),
                   row_blk(QT_ROWS),
                   row_blk(HEAD_DIM),
                   pl.BlockSpec((1, VT_ROWS, ATT_CHUNK), lambda i: (i, 0, 0)),
                   pl.BlockSpec((IDX_HEADS, TM), lambda i: (0, i))],
        out_shape=[jax.ShapeDtypeStruct((T, pool_width), BF16),
                   jax.ShapeDtypeStruct((nb, QT_ROWS, lanes), BF16),
                   jax.ShapeDtypeStruct((nb, HEAD_DIM, lanes), BF16),
                   jax.ShapeDtypeStruct((T, dsa_width), BF16),
                   jax.ShapeDtypeStruct((T, QT_ROWS), BF16),
                   jax.ShapeDtypeStruct((T, HEAD_DIM), BF16),
                   jax.ShapeDtypeStruct((T // ATT_CHUNK, VT_ROWS, ATT_CHUNK), BF16),
                   jax.ShapeDtypeStruct((IDX_HEADS, T), F32)],
        scratch_shapes=[pltpu.VMEM((POOL_HALO, pool_width), F32)],
        compiler_params=_cparams(),
        name="even_proj",
    )(x, g.reshape(1, D), w, pool_w, pool_scale.reshape(1, pool_width), gain_b, extra,
      _pad_cols(k_gain.reshape(1, HEAD_DIM), QT_ROWS))


def _odd_proj_kernel(x_ref, g_ref, w_ref, qg_ref, kg_ref, qt_ref, gate_ref, k_ref, vt_ref, *, q_w, q_scale):
    xn = _rms_rows_bf16(x_ref[...], g_ref[...])
    hq = jnp.dot(xn, w_ref[:, :q_w], preferred_element_type=F32)
    _store_heads_t(hq, qt_ref, qg_ref, n_heads=SWA_Q_HEADS, normalize=True, scale=q_scale)
    gate_ref[...] = _silu(jnp.dot(xn, w_ref[:, q_w:2 * q_w], preferred_element_type=F32)).astype(BF16)
    h = jnp.dot(xn, w_ref[:, 2 * q_w:], preferred_element_type=F32)
    kv_w = SWA_KV_HEADS * HEAD_DIM
    for hd in range(SWA_KV_HEADS):
        k = h[:, hd * HEAD_DIM:(hd + 1) * HEAD_DIM]
        ms = jnp.mean(k * k, axis=-1, keepdims=True)
        k_ref[hd] = (k * lax.rsqrt(ms + EPS) * kg_ref[...]).astype(BF16)
    ones_rows = _ones_row_pad(TM)
    for p in range(SWA_KV_HEADS // 2):
        pair = h[:, kv_w + p * 128:kv_w + (p + 1) * 128].T
        vt_ref[2 * p] = jnp.concatenate([pair[:HEAD_DIM], ones_rows], axis=0).astype(BF16)
        vt_ref[2 * p + 1] = jnp.concatenate([pair[HEAD_DIM:], ones_rows], axis=0).astype(BF16)


def _odd_proj(x, g, w, q_gain, k_gain, *, q_width, q_scale):
    T, D = x.shape
    nb = T // QB
    lanes = SWA_Q_HEADS * QB
    gain_b = jnp.broadcast_to(q_gain.reshape(HEAD_DIM, 1), (HEAD_DIM, QB)).astype(F32)
    kern = functools.partial(_odd_proj_kernel, q_w=q_width, q_scale=q_scale)
    return pl.pallas_call(
        kern,
        grid=(T // TM,),
        in_specs=[pl.BlockSpec((TM, D), lambda i: (i, 0)),
                  _const_spec((1, D)),
                  _const_spec(w.shape),
                  _const_spec((HEAD_DIM, QB)),
                  _const_spec((1, HEAD_DIM))],
        out_specs=[pl.BlockSpec((TM // QB, HEAD_DIM, lanes), lambda i: (i, 0, 0)),
                   pl.BlockSpec((TM, q_width), lambda i: (i, 0)),
                   pl.BlockSpec((SWA_KV_HEADS, TM, HEAD_DIM), lambda i: (0, i, 0)),
                   pl.BlockSpec((SWA_KV_HEADS, VT_ROWS, TM), lambda i: (0, 0, i))],
        out_shape=[jax.ShapeDtypeStruct((nb, HEAD_DIM, lanes), BF16),
                   jax.ShapeDtypeStruct((T, q_width), BF16),
                   jax.ShapeDtypeStruct((SWA_KV_HEADS, T, HEAD_DIM), BF16),
                   jax.ShapeDtypeStruct((SWA_KV_HEADS, VT_ROWS, T), BF16)],
        compiler_params=_cparams(),
        name="odd_proj",
    )(x, g.reshape(1, D), w, gain_b, k_gain.reshape(1, HEAD_DIM))


def _out_proj_kernel(*refs, n_terms):
    x_ref = refs[0]
    out_ref = refs[-1]
    acc = x_ref[...]
    for t in range(n_terms):
        acc = acc + jnp.dot(refs[1 + 2 * t][...], refs[2 + 2 * t][...], preferred_element_type=F32)
    out_ref[...] = acc


def _out_proj(x, terms):
    T, D = x.shape
    args = [x]
    in_specs = [pl.BlockSpec((TM, D), lambda i: (i, 0))]
    for a, w in terms:
        args += [a, w]
        in_specs += [pl.BlockSpec((TM, a.shape[1]), lambda i: (i, 0)), _const_spec(w.shape)]
    return pl.pallas_call(
        functools.partial(_out_proj_kernel, n_terms=len(terms)),
        grid=(T // TM,),
        in_specs=in_specs,
        out_specs=pl.BlockSpec((TM, D), lambda i: (i, 0)),
        out_shape=jax.ShapeDtypeStruct((T, D), F32),
        compiler_params=_cparams(),
        name="out_proj",
    )(*args)


def _store_gated(o_t, gate_ref, out_ref, n_heads):
    for p in range(n_heads // 2):
        pair = jnp.concatenate([o_t[:, (2 * p) * QB:(2 * p + 1) * QB],
                                o_t[:, (2 * p + 1) * QB:(2 * p + 2) * QB]], axis=0)
        g = gate_ref[:, p * 128:(p + 1) * 128].astype(F32)
        out_ref[:, p * 128:(p + 1) * 128] = (g * pair.T).astype(BF16)


def _dsa_kernel(bounded_ref, qt_ref, iqt0_ref, iwt0_ref, iqt_ref, iwt_ref, gate_ref, k_ref, ik_ref, vt_ref, bias_ref,
                out_ref, score_ref, thr_ref):
    s = pl.program_id(0)
    nb = pl.num_programs(0) - 1
    lanes = DSA_HEADS * QB
    blocks_per_chunk = IDX_CHUNK // QB
    sub = ATT_CHUNK // QB
    nxt = s + 1
    cur = s
    prv = jnp.maximum(s - 1, 0)
    n_nxt = nxt // blocks_per_chunk + 1
    n_cur = cur // blocks_per_chunk + 1
    n_prv = prv // blocks_per_chunk + 1
    nxt_slot = nxt % 3
    cur_slot = cur % 3
    prv_slot = prv % 3
    krow = lax.broadcasted_iota(I32, (IDX_CHUNK, QB), 0)
    lane_pos = lax.broadcasted_iota(I32, (IDX_CHUNK, QB), 1)

    def index_chunk(c, block, slot, iqt, iwt):
        off = pl.multiple_of(c * IDX_CHUNK, IDX_CHUNK)
        sc = jnp.dot(ik_ref[pl.ds(off, IDX_CHUNK), :], iqt, preferred_element_type=F32)
        score = jnp.zeros((IDX_CHUNK, QB), F32)
        for h in range(IDX_HEADS):
            score = score + jnp.maximum(sc[:, h * QB:(h + 1) * QB], 0.0) * iwt[h:h + 1, :]
        causal = off + krow <= block * QB + lane_pos
        score_ref[slot, pl.ds(off, IDX_CHUNK), :] = jnp.where(causal, score, -jnp.inf)

    def index_chunks(lo, hi, block, slot, iqt_r, iwt_r):
        iqt = iqt_r[0]
        iwt = iwt_r[...]

        def body(c, carry):
            index_chunk(c, block, slot, iqt, iwt)
            return carry

        lax.fori_loop(lo, hi, body, 0)

    def key_to_float(key):
        return lax.bitcast_convert_type(jnp.where(key < 0, key ^ 0x7FFFFFFF, key), F32)

    def search_unit(state):
        base, bit, acc, ch = state
        cand = key_to_float(base + bit)
        off = pl.multiple_of(ch * IDX_CHUNK, IDX_CHUNK)
        hit = jnp.where(score_ref[cur_slot, pl.ds(off, IDX_CHUNK), :] >= cand, 1, 0).astype(I32)
        acc = acc + jnp.sum(hit.reshape(IDX_CHUNK // 8, 8, QB), axis=0)
        last = ch + 1 == n_cur
        accept = jnp.sum(acc, axis=0, keepdims=True) >= DSA_TOPK
        base = jnp.where(last & accept, base + bit, base)
        acc = jnp.where(last, 0, acc)
        bit = jnp.where(last, lax.shift_right_logical(bit, 1), bit)
        ch = jnp.where(last, 0, ch + 1)
        return base, bit, acc, ch

    search_init = (jnp.full((1, QB), INT_MIN, I32), jnp.int32(INT_MIN), jnp.zeros((8, QB), I32), jnp.int32(0))

    def store_threshold(state):
        thr = key_to_float(jnp.maximum(state[0], FINITE_MIN_KEY))
        thr_ref[cur % 2] = jnp.broadcast_to(thr, (8, QB))

    qt = qt_ref[0]
    thr_prv = thr_ref[prv % 2][0:1]

    def tiles(c, near, thr, fn):
        off = pl.multiple_of(c * ATT_CHUNK, ATT_CHUNK)
        sc = jnp.dot(k_ref[pl.ds(off, ATT_CHUNK), :], qt, preferred_element_type=F32)
        sel = score_ref[prv_slot, pl.ds(off, ATT_CHUNK), :] >= thr
        blocks = []
        for sb in range(sub):
            delta = jnp.clip(prv - (c * sub + sb), 0, FAR_DELTA)
            sel_sb = sel[sb * QB:(sb + 1) * QB]
            row = []
            for h in range(DSA_HEADS):
                t = sc[sb * QB:(sb + 1) * QB, h * QB:(h + 1) * QB]
                if near:
                    t = t + bias_ref[delta, h]
                row.append(fn(t, sel_sb))
            blocks.append(jnp.concatenate(row, axis=1))
        return jnp.concatenate(blocks, axis=0)

    def bounded_pair(near, pair, thr):
        pvs = []
        for half in range(ATT_PAIR):
            c = pair * ATT_PAIR + half
            p = tiles(c, near, thr, lambda t, sel: jnp.where(sel, jnp.exp2(t), 0.0))
            pvs.append(jnp.dot(vt_ref[c], p.astype(BF16), preferred_element_type=F32))
        return sum(pvs[1:], pvs[0])

    n_far = jnp.maximum(prv - (FAR_DELTA - 1), 0) // (sub * ATT_PAIR)
    acc0 = jnp.zeros((VT_ROWS, lanes), F32)

    def finish_bounded(acc):
        _store_gated(acc[:HEAD_DIM] / acc[HEAD_DIM:HEAD_DIM + 1], gate_ref, out_ref, DSA_HEADS)

    def fused_path():
        iqt = iqt_ref[0]
        iwt = iwt_ref[...]

        def body(near, c, carry):
            acc, state = carry
            thr = jnp.where(c < n_prv, thr_prv, jnp.inf)
            acc = acc + bounded_pair(near, jnp.minimum(c, n_prv - 1), thr)
            for _ in range(SEARCH_PASSES):
                state = search_unit(state)
            index_chunk(c, nxt, nxt_slot, iqt, iwt)
            return acc, state

        carry = (acc0, search_init)
        carry = lax.fori_loop(0, n_far, functools.partial(body, False), carry)
        acc, state = lax.fori_loop(n_far, n_cur, functools.partial(body, True), carry)
        store_threshold(state)
        finish_bounded(acc)
        index_chunks(n_cur, jnp.where(nxt < nb, n_nxt, n_cur), nxt, nxt_slot, iqt_ref, iwt_ref)
        return 0

    def general_body(c, carry):
        m, l, acc = carry
        t = tiles(c, True, thr_prv, lambda t, sel: jnp.where(sel, t, MASKED))
        m_new = jnp.maximum(m, jnp.max(t, axis=0, keepdims=True))
        alpha = jnp.exp2(m - m_new)
        p = jnp.exp2(t - m_new)
        l_new = alpha * l + jnp.sum(p, axis=0, keepdims=True)
        pv = jnp.dot(vt_ref[c], p.astype(BF16), preferred_element_type=F32)
        return m_new, l_new, acc * alpha + pv

    def separate_path():
        @pl.when(s == 0)
        def _():
            index_chunks(0, n_cur, cur, cur_slot, iqt0_ref, iwt0_ref)

        @pl.when(nxt < nb)
        def _():
            index_chunks(0, n_nxt, nxt, nxt_slot, iqt_ref, iwt_ref)

        @pl.when(s < nb)
        def _():
            store_threshold(lax.fori_loop(0, SEARCH_PASSES * n_cur, lambda _, st: search_unit(st), search_init))

        @pl.when((s >= 1) & (bounded_ref[0] > 0))
        def _():
            acc = lax.fori_loop(0, n_far, lambda c, a: a + bounded_pair(False, c, thr_prv), acc0)
            finish_bounded(lax.fori_loop(n_far, n_prv, lambda c, a: a + bounded_pair(True, c, thr_prv), acc))

        @pl.when((s >= 1) & (bounded_ref[0] <= 0))
        def _():
            m0 = jnp.full((1, lanes), MASKED, F32)
            l0 = jnp.zeros((1, lanes), F32)
            _, l, acc = lax.fori_loop(0, n_prv * ATT_PAIR, general_body, (m0, l0, acc0))
            _store_gated(acc[:HEAD_DIM] / l, gate_ref, out_ref, DSA_HEADS)

        return 0

    lax.cond((bounded_ref[0] > 0) & (s >= 1) & (s < nb), fused_path, separate_path)


def _dsa_attention(bounded, qt, iqt, iwt, gate, k, ik, vt, bias):
    nb = qt.shape[0]
    T = k.shape[0]
    width = DSA_HEADS * HEAD_DIM

    def prv(s):
        return jnp.maximum(s - 1, 0)

    def nxt(s):
        return jnp.minimum(s + 1, nb - 1)

    return pl.pallas_call(
        _dsa_kernel,
        grid=(nb + 1,),
        in_specs=[pl.BlockSpec(memory_space=pltpu.SMEM),
                  pl.BlockSpec((1, QT_ROWS, DSA_HEADS * QB), lambda s: (prv(s), 0, 0)),
                  pl.BlockSpec((1, HEAD_DIM, IDX_HEADS * QB), lambda s: (0, 0, 0)),
                  pl.BlockSpec((IDX_HEADS, QB), lambda s: (0, 0)),
                  pl.BlockSpec((1, HEAD_DIM, IDX_HEADS * QB), lambda s: (nxt(s), 0, 0)),
                  pl.BlockSpec((IDX_HEADS, QB), lambda s: (0, nxt(s))),
                  pl.BlockSpec((QB, width), lambda s: (prv(s), 0)),
                  _const_spec(k.shape),
                  _const_spec(ik.shape),
                  _const_spec(vt.shape),
                  _const_spec(bias.shape)],
        out_specs=pl.BlockSpec((QB, width), lambda s: (prv(s), 0)),
        out_shape=jax.ShapeDtypeStruct((T, width), BF16),
        scratch_shapes=[pltpu.VMEM((3, T, QB), F32), pltpu.VMEM((2, 8, QB), F32)],
        compiler_params=_cparams(),
        name="dsa_attention",
    )(bounded, qt, iqt, iwt, iqt, iwt, gate, k, ik, vt, bias)


def _swa_kernel(bounded_ref, qt_ref, kp_ref, kc_ref, vp_ref, vc_ref, sink_ref, gate_ref, bias_ref, out_ref):
    i = pl.program_id(0)
    grp = SWA_Q_HEADS // SWA_KV_HEADS
    shape = (2 * QB, QB)
    c_idx = lax.broadcasted_iota(I32, shape, 0)
    dist = QB + lax.broadcasted_iota(I32, shape, 1) - c_idx
    mask = (dist >= 0) & (dist < QB) & ((c_idx >= QB) | (i > 0))

    def group(kv, bounded):
        kcat = jnp.concatenate([kp_ref[kv], kc_ref[kv]], axis=0)
        vcat = jnp.concatenate([vp_ref[kv], vc_ref[kv]], axis=1)
        qg = qt_ref[0, :, kv * grp * QB:(kv + 1) * grp * QB]
        s = jnp.dot(kcat, qg, preferred_element_type=F32)
        sink = sink_ref[:, kv * grp * QB:(kv + 1) * grp * QB]
        logits = [s[:, j * QB:(j + 1) * QB] + bias_ref[kv * grp + j] for j in range(grp)]
        if bounded:
            p = jnp.concatenate([jnp.where(mask, jnp.exp2(t), 0.0) for t in logits], axis=1)
            pv = jnp.dot(vcat, p.astype(BF16), preferred_element_type=F32)
            return pv[:HEAD_DIM] / (pv[HEAD_DIM:HEAD_DIM + 1] + jnp.exp2(sink))
        t = jnp.concatenate([jnp.where(mask, t, MASKED) for t in logits], axis=1)
        m = jnp.maximum(jnp.max(t, axis=0, keepdims=True), sink)
        p = jnp.exp2(t - m)
        l = jnp.sum(p, axis=0, keepdims=True) + jnp.exp2(sink - m)
        pv = jnp.dot(vcat, p.astype(BF16), preferred_element_type=F32)
        return pv[:HEAD_DIM] / l

    def path(bounded):
        return jnp.concatenate([group(kv, bounded) for kv in range(SWA_KV_HEADS)], axis=1)

    o_t = lax.cond(bounded_ref[0] > 0, functools.partial(path, True), functools.partial(path, False))
    _store_gated(o_t, gate_ref, out_ref, SWA_Q_HEADS)


def _swa_attention(bounded, qt, k_hm, vt_hm, sink_b, gate, bias):
    nb = qt.shape[0]
    T = k_hm.shape[1]
    width = SWA_Q_HEADS * HEAD_DIM

    def prev(i):
        return jnp.maximum(i - 1, 0)

    return pl.pallas_call(
        _swa_kernel,
        grid=(nb,),
        in_specs=[pl.BlockSpec(memory_space=pltpu.SMEM),
                  pl.BlockSpec((1, HEAD_DIM, SWA_Q_HEADS * QB), lambda i: (i, 0, 0)),
                  pl.BlockSpec((SWA_KV_HEADS, QB, HEAD_DIM), lambda i: (0, prev(i), 0)),
                  pl.BlockSpec((SWA_KV_HEADS, QB, HEAD_DIM), lambda i: (0, i, 0)),
                  pl.BlockSpec((SWA_KV_HEADS, VT_ROWS, QB), lambda i: (0, 0, prev(i))),
                  pl.BlockSpec((SWA_KV_HEADS, VT_ROWS, QB), lambda i: (0, 0, i)),
                  _const_spec((1, SWA_Q_HEADS * QB)),
                  pl.BlockSpec((QB, width), lambda i: (i, 0)),
                  _const_spec(bias.shape)],
        out_specs=pl.BlockSpec((QB, width), lambda i: (i, 0)),
        out_shape=jax.ShapeDtypeStruct((T, width), BF16),
        compiler_params=_cparams(),
        name="swa_attention",
    )(bounded, qt, k_hm, k_hm, vt_hm, vt_hm, sink_b, gate, bias)


def _pad_cols(w, width):
    return jnp.pad(w, ((0, 0), (0, width - w.shape[1])))


def _even_layer(x, norm_g, w_in, pool_w, pool_scale, q_gain, k_gain, w_out, rel_bias, dsa_bias):
    pool_width = pool_w.shape[0] * pool_w.shape[1]
    dsa_width = DSA_HEADS * HEAD_DIM
    idx_width = IDX_HEADS * HEAD_DIM
    sizes = (pool_width, pool_width, dsa_width, HEAD_DIM, HEAD_DIM, dsa_width, idx_width, HEAD_DIM, IDX_HEADS)
    offs = [0]
    for s in sizes:
        offs.append(offs[-1] + s)
    col = lambda j: w_in[:, offs[j]:offs[j + 1]]
    w_all = jnp.concatenate([col(0), col(1), col(2), col(6), col(5), _pad_cols(col(3), 128), _pad_cols(col(7), 128),
                             _pad_cols(jnp.concatenate([col(4), col(8)], axis=1), 128)], axis=1).astype(BF16)

    q_scale = HEAD_DIM ** -0.5 * LOG2E
    far = rel_bias[NUM_BUCKETS - 1, :DSA_HEADS].astype(F32) * LOG2E
    far_hi = far.astype(BF16)
    far_lo = (far - far_hi.astype(F32)).astype(BF16)
    extra = jnp.zeros((QT_ROWS - HEAD_DIM, DSA_HEADS * QB), BF16)
    extra = extra.at[0].set(jnp.repeat(far_hi, QB)).at[1].set(jnp.repeat(far_lo, QB))
    q_norm = jnp.sqrt(HEAD_DIM * jnp.max(q_gain.astype(F32) ** 2) * q_scale ** 2 + jnp.max(far ** 2))
    k_norm = jnp.sqrt(HEAD_DIM * jnp.max(k_gain.astype(F32) ** 2) + 2.0)
    near_max = jnp.max(jnp.abs(rel_bias[:, :DSA_HEADS] - rel_bias[NUM_BUCKETS - 1, :DSA_HEADS])) * LOG2E
    bound = 1.02 * q_norm * k_norm + near_max
    bounded = (bound < SAFE_LOG2_RANGE).astype(I32).reshape(1)

    py, qt, iqt, gate, k, ik, vt, iwt = _even_proj(
        x, norm_g, w_all, pool_w.astype(BF16), pool_scale, q_gain, extra, k_gain, pool_width=pool_width,
        dsa_width=dsa_width, q_scale=q_scale, iw_scale=IDX_HEADS ** -0.5 * HEAD_DIM ** -0.5)
    dy = _dsa_attention(bounded, qt, iqt, iwt, gate, k, ik, vt, dsa_bias)
    w_o = w_out.astype(BF16)
    return _out_proj(x, [(py, w_o[:pool_width]), (dy, w_o[pool_width:])])


def _odd_layer(x, norm_g, w_in, q_gain, k_gain, sinks, w_out, rel_bias, swa_bias):
    q_width = SWA_Q_HEADS * HEAD_DIM
    kv_width = SWA_KV_HEADS * HEAD_DIM
    w_all = jnp.concatenate([w_in[:, :q_width], w_in[:, q_width + 2 * kv_width:],
                             w_in[:, q_width:q_width + 2 * kv_width]], axis=1).astype(BF16)
    q_scale = HEAD_DIM ** -0.5 * LOG2E
    sink_b = jnp.repeat(sinks.astype(F32) * LOG2E, QB).reshape(1, SWA_Q_HEADS * QB)
    q_norm = jnp.sqrt(HEAD_DIM * jnp.max(q_gain.astype(F32) ** 2)) * q_scale
    k_norm = jnp.sqrt(HEAD_DIM * jnp.max(k_gain.astype(F32) ** 2))
    bound = jnp.maximum(1.02 * q_norm * k_norm + jnp.max(jnp.abs(rel_bias)) * LOG2E, jnp.max(jnp.abs(sink_b)))
    bounded = (bound < SAFE_LOG2_RANGE).astype(I32).reshape(1)

    qt, gate, k_hm, vt_hm = _odd_proj(x, norm_g, w_all, q_gain, k_gain, q_width=q_width, q_scale=q_scale)
    go = _swa_attention(bounded, qt, k_hm, vt_hm, sink_b, gate, swa_bias)
    return _out_proj(x, [(go, w_out.astype(BF16))])


def kernel(x, rel_bias, even_norm, even_w_in, even_pool_w, even_pool_scale, even_q_gain, even_k_gain,
           even_w_out, odd_norm, odd_w_in, odd_q_gain, odd_k_gain, odd_sinks, odd_w_out):
    B, T, D = x.shape
    depth = even_norm.shape[0] + odd_norm.shape[0]
    dsa_bias = _bias_tiles(rel_bias, N_BIAS_TILES, DSA_HEADS, QB, QB, 0, minus_far=True)
    swa_bias = _bias_tiles(rel_bias, 1, SWA_Q_HEADS, 2 * QB, 0, QB)[0]
    outs = []
    for b in range(B):
        h = x.reshape(T, D) if B == 1 else x[b]
        for layer in range(depth):
            j = layer // 2
            if layer % 2 == 0:
                h = _even_layer(h, even_norm[j], even_w_in[j], even_pool_w[j], even_pool_scale[j],
                                even_q_gain[j], even_k_gain[j], even_w_out[j], rel_bias, dsa_bias)
            else:
                h = _odd_layer(h, odd_norm[j], odd_w_in[j], odd_q_gain[j], odd_k_gain[j],
                               odd_sinks[j], odd_w_out[j], rel_bias, swa_bias)
        outs.append(h)
    return outs[0].reshape(B, T, D) if B == 1 else jnp.stack(outs, axis=0)
```

```python
import functools

import jax
import jax.numpy as jnp
from jax import lax
from jax.experimental import pallas as pl
from jax.experimental.pallas import tpu as pltpu

F32 = jnp.float32
BF16 = jnp.bfloat16
I32 = jnp.int32

EPS = 1e-6
HEAD_DIM = 64
QB = 128
POOL_WINDOWS = (2, 4, 8, 16)
POOL_HALO = 16
DSA_HEADS = 16
IDX_HEADS = 16
DSA_TOPK = 256
SWA_Q_HEADS = 32
SWA_KV_HEADS = 4
NUM_BUCKETS = 32
IDX_CHUNK = 512
ATT_CHUNK = 256
ATT_PAIR = IDX_CHUNK // ATT_CHUNK
INT_MIN = -(2 ** 31)
FINITE_MIN_KEY = -0x7F800000
SEARCH_PASSES = 32
BLOCKS_PER_STEP = 2
MASKED = -1e30
LOG2E = 1.4426950408889634
QT_ROWS = 128
VT_ROWS = 80
SAFE_LOG2_RANGE = 60.0
TM = 256
VMEM_LIMIT = 56 * 1024 * 1024


def _bucket_starts():
    max_exact = NUM_BUCKETS // 2
    starts = list(range(max_exact + 1))
    n = max_exact
    for b in range(max_exact + 1, NUM_BUCKETS):
        while n ** 16 < max_exact ** 16 * 64 ** (b - max_exact):
            n += 1
        starts.append(n)
    return tuple(starts)


BUCKET_START = _bucket_starts()
FAR_DELTA = -(-(BUCKET_START[-1] + QB - 1) // QB)
N_BIAS_TILES = FAR_DELTA + 1


def _cparams(n_grid=1):
    return pltpu.CompilerParams(dimension_semantics=("arbitrary",) * n_grid,
                                vmem_limit_bytes=VMEM_LIMIT)


def _const_spec(shape):
    return pl.BlockSpec(shape, lambda i: (0,) * len(shape), pipeline_mode=pl.Buffered(1))


def _rms_rows_bf16(x, g):
    ms = jnp.mean(x * x, axis=-1, keepdims=True)
    return (x * lax.rsqrt(ms + EPS) * g).astype(BF16)


def _silu(x):
    return x / (1.0 + jnp.exp(-x))


def _bias_tiles_kernel(rb_ref, out_ref, *, n_heads, rows, base_step, base0, minus_far):
    base = base0 + pl.program_id(0) * base_step
    shape = (rows, QB)
    dist = base + lax.broadcasted_iota(I32, shape, 1) - lax.broadcasted_iota(I32, shape, 0)
    at_least = [dist >= BUCKET_START[b] for b in range(1, NUM_BUCKETS)]
    for h in range(n_heads):
        val = jnp.full(shape, rb_ref[0, h], F32)
        for b in range(1, NUM_BUCKETS):
            val = jnp.where(at_least[b - 1], rb_ref[b, h], val)
        if minus_far:
            val = val - rb_ref[NUM_BUCKETS - 1, h]
        out_ref[0, h] = val * LOG2E


def _bias_tiles(rel_bias, n_tiles, n_heads, rows, base_step, base0, minus_far=False):
    kern = functools.partial(_bias_tiles_kernel, n_heads=n_heads, rows=rows, base_step=base_step, base0=base0,
                             minus_far=minus_far)
    return pl.pallas_call(
        kern,
        grid=(n_tiles,),
        in_specs=[pl.BlockSpec(memory_space=pltpu.SMEM)],
        out_specs=pl.BlockSpec((1, n_heads, rows, QB), lambda t: (t, 0, 0, 0)),
        out_shape=jax.ShapeDtypeStruct((n_tiles, n_heads, rows, QB), F32),
        compiler_params=_cparams(),
        name="bias_tiles",
    )(rel_bias)


def _store_heads_t(h, out_ref, gain_ref, *, n_heads, normalize, scale):
    for b in range(TM // QB):
        rows = h[b * QB:(b + 1) * QB]
        for p in range(n_heads // 2):
            pair = rows[:, p * 128:(p + 1) * 128].T
            for hh in range(2):
                t = pair[hh * HEAD_DIM:(hh + 1) * HEAD_DIM]
                if normalize:
                    ms = jnp.mean(t * t, axis=0, keepdims=True)
                    t = t * lax.rsqrt(ms + EPS) * gain_ref[...] * scale
                head = 2 * p + hh
                out_ref[b, :HEAD_DIM, head * QB:(head + 1) * QB] = t.astype(BF16)


def _pool_mix(i, pin, gate, pw_ref, ps_ref, out_ref, halo_ref):
    width = pin.shape[1]
    gc = width // len(POOL_WINDOWS)

    @pl.when(i == 0)
    def _():
        halo_ref[...] = jnp.zeros_like(halo_ref)

    ext = jnp.concatenate([halo_ref[...], pin], axis=0)
    halo_ref[...] = pin[TM - POOL_HALO:]
    pos = i * TM + lax.broadcasted_iota(I32, (TM, gc), 0)
    outs = []
    for grp, win in enumerate(POOL_WINDOWS):
        e = ext[:, grp * gc:(grp + 1) * gc]
        s = e
        span = 1
        while span < win:
            s = s + pltpu.roll(s, span, 0)
            span *= 2
        s = s[POOL_HALO:]
        a = pin[:, grp * gc:(grp + 1) * gc]
        cnt = jnp.minimum(pos + 1, win).astype(F32)
        pooled = s / cnt - a
        y = jnp.dot(pooled.astype(BF16), pw_ref[grp], preferred_element_type=F32)
        outs.append(y * ps_ref[:, grp * gc:(grp + 1) * gc])
    py = jnp.concatenate(outs, axis=1)
    out_ref[...] = (_silu(gate) * py).astype(BF16)


def _ones_row_pad(cols):
    row = lax.broadcasted_iota(I32, (VT_ROWS - HEAD_DIM, cols), 0)
    return jnp.where(row == 0, 1.0, 0.0)


def _even_proj_kernel(x_ref, g_ref, w_ref, pw_ref, ps_ref, qg_ref, extra_ref, kg_ref,
                      py_ref, qt_ref, iqt_ref, gate_ref, k_ref, ik_ref, vt_ref, iwt_ref, halo_ref,
                      *, pool_w, dsa_w, q_scale, iw_scale):
    i = pl.program_id(0)
    xn = _rms_rows_bf16(x_ref[...], g_ref[...])
    col = 0

    def proj(width):
        nonlocal col
        h = jnp.dot(xn, w_ref[:, col:col + width], preferred_element_type=F32)
        col += width
        return h

    h = proj(2 * pool_w)
    _pool_mix(i, h[:, :pool_w], h[:, pool_w:], pw_ref, ps_ref, py_ref, halo_ref)

    for b in range(TM // QB):
        qt_ref[b, HEAD_DIM:, :] = extra_ref[...]
    _store_heads_t(proj(dsa_w), qt_ref, qg_ref, n_heads=DSA_HEADS, normalize=True, scale=q_scale)
    _store_heads_t(proj(IDX_HEADS * HEAD_DIM), iqt_ref, qg_ref, n_heads=IDX_HEADS, normalize=False, scale=1.0)

    h = proj(dsa_w + 3 * 128)
    gate_ref[...] = _silu(h[:, :dsa_w]).astype(BF16)
    k = h[:, dsa_w:dsa_w + 128]
    ms = jnp.sum(k * k, axis=-1, keepdims=True) * (1.0 / HEAD_DIM)
    kn = k * lax.rsqrt(ms + EPS) * kg_ref[...]
    lane = lax.broadcasted_iota(I32, kn.shape, 1)
    k_ref[...] = jnp.where((lane == HEAD_DIM) | (lane == HEAD_DIM + 1), 1.0, kn).astype(BF16)
    ik_ref[...] = h[:, dsa_w + 128:dsa_w + 128 + HEAD_DIM].astype(BF16)
    vw = h[:, dsa_w + 256:dsa_w + 384].T
    vt_ref[0] = jnp.concatenate([vw[:HEAD_DIM], _ones_row_pad(TM)], axis=0).astype(BF16)
    iwt_ref[...] = vw[HEAD_DIM:HEAD_DIM + IDX_HEADS] * iw_scale


def _even_proj(x, g, w, pool_w, pool_scale, q_gain, extra, k_gain, *, pool_width, dsa_width, q_scale, iw_scale):
    T, D = x.shape
    assert TM == ATT_CHUNK
    nb = T // QB
    lanes = DSA_HEADS * QB
    gain_b = jnp.broadcast_to(q_gain.reshape(HEAD_DIM, 1), (HEAD_DIM, QB)).astype(F32)
    kern = functools.partial(_even_proj_kernel, pool_w=pool_width, dsa_w=dsa_width, q_scale=q_scale,
                             iw_scale=iw_scale)
    row_blk = lambda width: pl.BlockSpec((TM, width), lambda i: (i, 0))
    return pl.pallas_call(
        kern,
        grid=(T // TM,),
        in_specs=[row_blk(D),
                  _const_spec((1, D)),
                  _const_spec(w.shape),
                  _const_spec(pool_w.shape),
                  _const_spec((1, pool_width)),
                  _const_spec((HEAD_DIM, QB)),
                  _const_spec(extra.shape),
                  _const_spec((1, QT_ROWS))],
        out_specs=[row_blk(pool_width),
                   pl.BlockSpec((TM // QB, QT_ROWS, lanes), lambda i: (i, 0, 0)),
                   pl.BlockSpec((TM // QB, HEAD_DIM, lanes), lambda i: (i, 0, 0)),
                   row_blk(dsa_width),
                   row_blk(QT_ROWS),
                   row_blk(HEAD_DIM),
                   pl.BlockSpec((1, VT_ROWS, ATT_CHUNK), lambda i: (i, 0, 0)),
                   pl.BlockSpec((IDX_HEADS, TM), lambda i: (0, i))],
        out_shape=[jax.ShapeDtypeStruct((T, pool_width), BF16),
                   jax.ShapeDtypeStruct((nb, QT_ROWS, lanes), BF16),
                   jax.ShapeDtypeStruct((nb, HEAD_DIM, lanes), BF16),
                   jax.ShapeDtypeStruct((T, dsa_width), BF16),
                   jax.ShapeDtypeStruct((T, QT_ROWS), BF16),
                   jax.ShapeDtypeStruct((T, HEAD_DIM), BF16),
                   jax.ShapeDtypeStruct((T // ATT_CHUNK, VT_ROWS, ATT_CHUNK), BF16),
                   jax.ShapeDtypeStruct((IDX_HEADS, T), F32)],
        scratch_shapes=[pltpu.VMEM((POOL_HALO, pool_width), F32)],
        compiler_params=_cparams(),
        name="even_proj",
    )(x, g.reshape(1, D), w, pool_w, pool_scale.reshape(1, pool_width), gain_b, extra,
      _pad_cols(k_gain.reshape(1, HEAD_DIM), QT_ROWS))


def _odd_proj_kernel(x_ref, g_ref, w_ref, qg_ref, kg_ref, qt_ref, gate_ref, k_ref, vt_ref, *, q_w, q_scale):
    xn = _rms_rows_bf16(x_ref[...], g_ref[...])
    hq = jnp.dot(xn, w_ref[:, :q_w], preferred_element_type=F32)
    _store_heads_t(hq, qt_ref, qg_ref, n_heads=SWA_Q_HEADS, normalize=True, scale=q_scale)
    gate_ref[...] = _silu(jnp.dot(xn, w_ref[:, q_w:2 * q_w], preferred_element_type=F32)).astype(BF16)
    h = jnp.dot(xn, w_ref[:, 2 * q_w:], preferred_element_type=F32)
    kv_w = SWA_KV_HEADS * HEAD_DIM
    for hd in range(SWA_KV_HEADS):
        k = h[:, hd * HEAD_DIM:(hd + 1) * HEAD_DIM]
        ms = jnp.mean(k * k, axis=-1, keepdims=True)
        k_ref[hd] = (k * lax.rsqrt(ms + EPS) * kg_ref[...]).astype(BF16)
    ones_rows = _ones_row_pad(TM)
    for p in range(SWA_KV_HEADS // 2):
        pair = h[:, kv_w + p * 128:kv_w + (p + 1) * 128].T
        vt_ref[2 * p] = jnp.concatenate([pair[:HEAD_DIM], ones_rows], axis=0).astype(BF16)
        vt_ref[2 * p + 1] = jnp.concatenate([pair[HEAD_DIM:], ones_rows], axis=0).astype(BF16)


def _odd_proj(x, g, w, q_gain, k_gain, *, q_width, q_scale):
    T, D = x.shape
    nb = T // QB
    lanes = SWA_Q_HEADS * QB
    gain_b = jnp.broadcast_to(q_gain.reshape(HEAD_DIM, 1), (HEAD_DIM, QB)).astype(F32)
    kern = functools.partial(_odd_proj_kernel, q_w=q_width, q_scale=q_scale)
    return pl.pallas_call(
        kern,
        grid=(T // TM,),
        in_specs=[pl.BlockSpec((TM, D), lambda i: (i, 0)),
                  _const_spec((1, D)),
                  _const_spec(w.shape),
                  _const_spec((HEAD_DIM, QB)),
                  _const_spec((1, HEAD_DIM))],
        out_specs=[pl.BlockSpec((TM // QB, HEAD_DIM, lanes), lambda i: (i, 0, 0)),
                   pl.BlockSpec((TM, q_width), lambda i: (i, 0)),
                   pl.BlockSpec((SWA_KV_HEADS, TM, HEAD_DIM), lambda i: (0, i, 0)),
                   pl.BlockSpec((SWA_KV_HEADS, VT_ROWS, TM), lambda i: (0, 0, i))],
        out_shape=[jax.ShapeDtypeStruct((nb, HEAD_DIM, lanes), BF16),
                   jax.ShapeDtypeStruct((T, q_width), BF16),
                   jax.ShapeDtypeStruct((SWA_KV_HEADS, T, HEAD_DIM), BF16),
                   jax.ShapeDtypeStruct((SWA_KV_HEADS, VT_ROWS, T), BF16)],
        compiler_params=_cparams(),
        name="odd_proj",
    )(x, g.reshape(1, D), w, gain_b, k_gain.reshape(1, HEAD_DIM))


def _out_proj_kernel(*refs, n_terms):
    x_ref = refs[0]
    out_ref = refs[-1]
    acc = x_ref[...]
    for t in range(n_terms):
        acc = acc + jnp.dot(refs[1 + 2 * t][...], refs[2 + 2 * t][...], preferred_element_type=F32)
    out_ref[...] = acc


def _out_proj(x, terms):
    T, D = x.shape
    args = [x]
    in_specs = [pl.BlockSpec((TM, D), lambda i: (i, 0))]
    for a, w in terms:
        args += [a, w]
        in_specs += [pl.BlockSpec((TM, a.shape[1]), lambda i: (i, 0)), _const_spec(w.shape)]
    return pl.pallas_call(
        functools.partial(_out_proj_kernel, n_terms=len(terms)),
        grid=(T // TM,),
        in_specs=in_specs,
        out_specs=pl.BlockSpec((TM, D), lambda i: (i, 0)),
        out_shape=jax.ShapeDtypeStruct((T, D), F32),
        compiler_params=_cparams(),
        name="out_proj",
    )(*args)


def _store_gated(o_t, gate_ref, out_ref, n_heads):
    for p in range(n_heads // 2):
        pair = jnp.concatenate([o_t[:, (2 * p) * QB:(2 * p + 1) * QB],
                                o_t[:, (2 * p + 1) * QB:(2 * p + 2) * QB]], axis=0)
        g = gate_ref[:, p * 128:(p + 1) * 128].astype(F32)
        out_ref[:, p * 128:(p + 1) * 128] = (g * pair.T).astype(BF16)


def _dsa_kernel(bounded_ref, qt_ref, iqt_ref, iwt_ref, gate_ref, k_ref, ik_ref, vt_ref, bias_ref, out_ref,
                score_ref, thr_ref):
    s = pl.program_id(0)
    ns = pl.num_programs(0) - 1
    lanes = DSA_HEADS * QB
    sub = ATT_CHUNK // QB
    cur = s
    prv = jnp.maximum(s - 1, 0)
    pairs_per_chunk = IDX_CHUNK // (BLOCKS_PER_STEP * QB)
    n_cur = cur // pairs_per_chunk + 1
    n_prv = prv // pairs_per_chunk + 1
    cur_slot = cur % 2
    prv_slot = prv % 2
    both = range(BLOCKS_PER_STEP)

    @pl.when(s < ns)
    def _():
        iqt = [iqt_ref[j] for j in both]
        iwt = iwt_ref[...]
        krow = lax.broadcasted_iota(I32, (IDX_CHUNK, QB), 0)
        lane_pos = lax.broadcasted_iota(I32, (IDX_CHUNK, QB), 1)

        def idx_body(c, carry):
            off = pl.multiple_of(c * IDX_CHUNK, IDX_CHUNK)
            ik = ik_ref[pl.ds(off, IDX_CHUNK), :]
            for j in both:
                sc = jnp.dot(ik, iqt[j], preferred_element_type=F32)
                score = jnp.zeros((IDX_CHUNK, QB), F32)
                for h in range(IDX_HEADS):
                    score = score + jnp.maximum(sc[:, h * QB:(h + 1) * QB], 0.0) * iwt[h:h + 1, j * QB:(j + 1) * QB]
                causal = off + krow <= (cur * BLOCKS_PER_STEP + j) * QB + lane_pos
                score_ref[cur_slot, j, pl.ds(off, IDX_CHUNK), :] = jnp.where(causal, score, -jnp.inf)
            return carry

        lax.fori_loop(0, n_cur, idx_body, 0)

    def key_to_float(key):
        return lax.bitcast_convert_type(jnp.where(key < 0, key ^ 0x7FFFFFFF, key), F32)

    def search_unit(state):
        bases, bit, accs, ch = state
        off = pl.multiple_of(ch * IDX_CHUNK, IDX_CHUNK)
        last = ch + 1 == n_cur
        new_bases, new_accs = [], []
        for j in both:
            cand = key_to_float(bases[j] + bit)
            hit = jnp.where(score_ref[cur_slot, j, pl.ds(off, IDX_CHUNK), :] >= cand, 1, 0).astype(I32)
            acc = accs[j] + jnp.sum(hit.reshape(IDX_CHUNK // 8, 8, QB), axis=0)
            accept = jnp.sum(acc, axis=0, keepdims=True) >= DSA_TOPK
            new_bases.append(jnp.where(last & accept, bases[j] + bit, bases[j]))
            new_accs.append(jnp.where(last, 0, acc))
        bit = jnp.where(last, lax.shift_right_logical(bit, 1), bit)
        ch = jnp.where(last, 0, ch + 1)
        return tuple(new_bases), bit, tuple(new_accs), ch

    search_init = (tuple(jnp.full((1, QB), INT_MIN, I32) for _ in both), jnp.int32(INT_MIN),
                   tuple(jnp.zeros((8, QB), I32) for _ in both), jnp.int32(0))

    def store_thresholds(state):
        for j in both:
            thr = key_to_float(jnp.maximum(state[0][j], FINITE_MIN_KEY))
            thr_ref[cur_slot, j] = jnp.broadcast_to(thr, (8, QB))

    qts = [qt_ref[j] for j in both]
    thr_prv = [thr_ref[prv_slot, j][0:1] for j in both]

    def tiles(j, c, near, thr, fn):
        off = pl.multiple_of(c * ATT_CHUNK, ATT_CHUNK)
        sc = jnp.dot(k_ref[pl.ds(off, ATT_CHUNK), :], qts[j], preferred_element_type=F32)
        sel = score_ref[prv_slot, j, pl.ds(off, ATT_CHUNK), :] >= thr
        blocks = []
        for sb in range(sub):
            delta = jnp.clip(prv * BLOCKS_PER_STEP + j - (c * sub + sb), 0, FAR_DELTA)
            sel_sb = sel[sb * QB:(sb + 1) * QB]
            row = []
            for h in range(DSA_HEADS):
                t = sc[sb * QB:(sb + 1) * QB, h * QB:(h + 1) * QB]
                if near:
                    t = t + bias_ref[delta, h]
                row.append(fn(t, sel_sb))
            blocks.append(jnp.concatenate(row, axis=1))
        return jnp.concatenate(blocks, axis=0)

    def bounded_pair(j, near, pair, thr):
        pvs = []
        for half in range(ATT_PAIR):
            c = pair * ATT_PAIR + half
            p = tiles(j, c, near, thr, lambda t, sel: jnp.where(sel, jnp.exp2(t), 0.0))
            pvs.append(jnp.dot(vt_ref[c], p.astype(BF16), preferred_element_type=F32))
        return sum(pvs[1:], pvs[0])

    n_far = jnp.maximum(prv * BLOCKS_PER_STEP - (FAR_DELTA - 1), 0) // (sub * ATT_PAIR)
    acc0 = tuple(jnp.zeros((VT_ROWS, lanes), F32) for _ in both)

    def store_block(j, o_t):
        rows = slice(j * QB, (j + 1) * QB)
        for p in range(DSA_HEADS // 2):
            pair = jnp.concatenate([o_t[:, (2 * p) * QB:(2 * p + 1) * QB],
                                    o_t[:, (2 * p + 1) * QB:(2 * p + 2) * QB]], axis=0)
            g = gate_ref[rows, p * 128:(p + 1) * 128].astype(F32)
            out_ref[rows, p * 128:(p + 1) * 128] = (g * pair.T).astype(BF16)

    def finish_bounded(accs):
        for j in both:
            store_block(j, accs[j][:HEAD_DIM] / accs[j][HEAD_DIM:HEAD_DIM + 1])

    def attend(near, c, accs, thrs):
        return tuple(accs[j] + bounded_pair(j, near, c, thrs[j]) for j in both)

    def fused_body(near, c, carry):
        accs, state = carry
        thrs = [jnp.where(c < n_prv, thr_prv[j], jnp.inf) for j in both]
        accs = attend(near, jnp.minimum(c, n_prv - 1), accs, thrs)
        for _ in range(SEARCH_PASSES):
            state = search_unit(state)
        return accs, state

    def fused_path():
        carry = (acc0, search_init)
        carry = lax.fori_loop(0, n_far, functools.partial(fused_body, False), carry)
        accs, state = lax.fori_loop(n_far, n_cur, functools.partial(fused_body, True), carry)
        store_thresholds(state)
        finish_bounded(accs)
        return 0

    def general_body(j, c, carry):
        m, l, acc = carry
        t = tiles(j, c, True, thr_prv[j], lambda t, sel: jnp.where(sel, t, MASKED))
        m_new = jnp.maximum(m, jnp.max(t, axis=0, keepdims=True))
        alpha = jnp.exp2(m - m_new)
        p = jnp.exp2(t - m_new)
        l_new = alpha * l + jnp.sum(p, axis=0, keepdims=True)
        pv = jnp.dot(vt_ref[c], p.astype(BF16), preferred_element_type=F32)
        return m_new, l_new, acc * alpha + pv

    def separate_path():
        @pl.when(s < ns)
        def _():
            store_thresholds(lax.fori_loop(0, SEARCH_PASSES * n_cur, lambda _, st: search_unit(st), search_init))

        @pl.when((s >= 1) & (bounded_ref[0] > 0))
        def _():
            accs = lax.fori_loop(0, n_far, lambda c, a: attend(False, c, a, thr_prv), acc0)
            finish_bounded(lax.fori_loop(n_far, n_prv, lambda c, a: attend(True, c, a, thr_prv), accs))

        @pl.when((s >= 1) & (bounded_ref[0] <= 0))
        def _():
            for j in both:
                m0 = jnp.full((1, lanes), MASKED, F32)
                l0 = jnp.zeros((1, lanes), F32)
                _, l, acc = lax.fori_loop(0, n_prv * ATT_PAIR, functools.partial(general_body, j), (m0, l0, acc0[j]))
                store_block(j, acc[:HEAD_DIM] / l)

        return 0

    lax.cond((bounded_ref[0] > 0) & (s >= 1) & (s < ns), fused_path, separate_path)


def _dsa_attention(bounded, qt, iqt, iwt, gate, k, ik, vt, bias):
    ns = qt.shape[0] // BLOCKS_PER_STEP
    T = k.shape[0]
    width = DSA_HEADS * HEAD_DIM
    rows = BLOCKS_PER_STEP * QB

    def prv(s):
        return jnp.maximum(s - 1, 0)

    def cur(s):
        return jnp.minimum(s, ns - 1)

    return pl.pallas_call(
        _dsa_kernel,
        grid=(ns + 1,),
        in_specs=[pl.BlockSpec(memory_space=pltpu.SMEM),
                  pl.BlockSpec((BLOCKS_PER_STEP, QT_ROWS, DSA_HEADS * QB), lambda s: (prv(s), 0, 0)),
                  pl.BlockSpec((BLOCKS_PER_STEP, HEAD_DIM, IDX_HEADS * QB), lambda s: (cur(s), 0, 0)),
                  pl.BlockSpec((IDX_HEADS, rows), lambda s: (0, cur(s))),
                  pl.BlockSpec((rows, width), lambda s: (prv(s), 0)),
                  _const_spec(k.shape),
                  _const_spec(ik.shape),
                  _const_spec(vt.shape),
                  _const_spec(bias.shape)],
        out_specs=pl.BlockSpec((rows, width), lambda s: (prv(s), 0)),
        out_shape=jax.ShapeDtypeStruct((T, width), BF16),
        scratch_shapes=[pltpu.VMEM((2, BLOCKS_PER_STEP, T, QB), F32), pltpu.VMEM((2, BLOCKS_PER_STEP, 8, QB), F32)],
        compiler_params=_cparams(),
        name="dsa_attention",
    )(bounded, qt, iqt, iwt, gate, k, ik, vt, bias)


def _swa_kernel(bounded_ref, qt_ref, kp_ref, kc_ref, vp_ref, vc_ref, sink_ref, gate_ref, bias_ref, out_ref):
    i = pl.program_id(0)
    grp = SWA_Q_HEADS // SWA_KV_HEADS
    shape = (2 * QB, QB)
    c_idx = lax.broadcasted_iota(I32, shape, 0)
    dist = QB + lax.broadcasted_iota(I32, shape, 1) - c_idx
    mask = (dist >= 0) & (dist < QB) & ((c_idx >= QB) | (i > 0))

    def group(kv, bounded):
        kcat = jnp.concatenate([kp_ref[kv], kc_ref[kv]], axis=0)
        vcat = jnp.concatenate([vp_ref[kv], vc_ref[kv]], axis=1)
        qg = qt_ref[0, :, kv * grp * QB:(kv + 1) * grp * QB]
        s = jnp.dot(kcat, qg, preferred_element_type=F32)
        sink = sink_ref[:, kv * grp * QB:(kv + 1) * grp * QB]
        logits = [s[:, j * QB:(j + 1) * QB] + bias_ref[kv * grp + j] for j in range(grp)]
        if bounded:
            p = jnp.concatenate([jnp.where(mask, jnp.exp2(t), 0.0) for t in logits], axis=1)
            pv = jnp.dot(vcat, p.astype(BF16), preferred_element_type=F32)
            return pv[:HEAD_DIM] / (pv[HEAD_DIM:HEAD_DIM + 1] + jnp.exp2(sink))
        t = jnp.concatenate([jnp.where(mask, t, MASKED) for t in logits], axis=1)
        m = jnp.maximum(jnp.max(t, axis=0, keepdims=True), sink)
        p = jnp.exp2(t - m)
        l = jnp.sum(p, axis=0, keepdims=True) + jnp.exp2(sink - m)
        pv = jnp.dot(vcat, p.astype(BF16), preferred_element_type=F32)
        return pv[:HEAD_DIM] / l

    def path(bounded):
        return jnp.concatenate([group(kv, bounded) for kv in range(SWA_KV_HEADS)], axis=1)

    o_t = lax.cond(bounded_ref[0] > 0, functools.partial(path, True), functools.partial(path, False))
    _store_gated(o_t, gate_ref, out_ref, SWA_Q_HEADS)


def _swa_attention(bounded, qt, k_hm, vt_hm, sink_b, gate, bias):
    nb = qt.shape[0]
    T = k_hm.shape[1]
    width = SWA_Q_HEADS * HEAD_DIM

    def prev(i):
        return jnp.maximum(i - 1, 0)

    return pl.pallas_call(
        _swa_kernel,
        grid=(nb,),
        in_specs=[pl.BlockSpec(memory_space=pltpu.SMEM),
                  pl.BlockSpec((1, HEAD_DIM, SWA_Q_HEADS * QB), lambda i: (i, 0, 0)),
                  pl.BlockSpec((SWA_KV_HEADS, QB, HEAD_DIM), lambda i: (0, prev(i), 0)),
                  pl.BlockSpec((SWA_KV_HEADS, QB, HEAD_DIM), lambda i: (0, i, 0)),
                  pl.BlockSpec((SWA_KV_HEADS, VT_ROWS, QB), lambda i: (0, 0, prev(i))),
                  pl.BlockSpec((SWA_KV_HEADS, VT_ROWS, QB), lambda i: (0, 0, i)),
                  _const_spec((1, SWA_Q_HEADS * QB)),
                  pl.BlockSpec((QB, width), lambda i: (i, 0)),
                  _const_spec(bias.shape)],
        out_specs=pl.BlockSpec((QB, width), lambda i: (i, 0)),
        out_shape=jax.ShapeDtypeStruct((T, width), BF16),
        compiler_params=_cparams(),
        name="swa_attention",
    )(bounded, qt, k_hm, k_hm, vt_hm, vt_hm, sink_b, gate, bias)


def _pad_cols(w, width):
    return jnp.pad(w, ((0, 0), (0, width - w.shape[1])))


def _even_layer(x, norm_g, w_in, pool_w, pool_scale, q_gain, k_gain, w_out, rel_bias, dsa_bias):
    pool_width = pool_w.shape[0] * pool_w.shape[1]
    dsa_width = DSA_HEADS * HEAD_DIM
    idx_width = IDX_HEADS * HEAD_DIM
    sizes = (pool_width, pool_width, dsa_width, HEAD_DIM, HEAD_DIM, dsa_width, idx_width, HEAD_DIM, IDX_HEADS)
    offs = [0]
    for s in sizes:
        offs.append(offs[-1] + s)
    col = lambda j: w_in[:, offs[j]:offs[j + 1]]
    w_all = jnp.concatenate([col(0), col(1), col(2), col(6), col(5), _pad_cols(col(3), 128), _pad_cols(col(7), 128),
                             _pad_cols(jnp.concatenate([col(4), col(8)], axis=1), 128)], axis=1).astype(BF16)

    q_scale = HEAD_DIM ** -0.5 * LOG2E
    far = rel_bias[NUM_BUCKETS - 1, :DSA_HEADS].astype(F32) * LOG2E
    far_hi = far.astype(BF16)
    far_lo = (far - far_hi.astype(F32)).astype(BF16)
    extra = jnp.zeros((QT_ROWS - HEAD_DIM, DSA_HEADS * QB), BF16)
    extra = extra.at[0].set(jnp.repeat(far_hi, QB)).at[1].set(jnp.repeat(far_lo, QB))
    q_norm = jnp.sqrt(HEAD_DIM * jnp.max(q_gain.astype(F32) ** 2) * q_scale ** 2 + jnp.max(far ** 2))
    k_norm = jnp.sqrt(HEAD_DIM * jnp.max(k_gain.astype(F32) ** 2) + 2.0)
    near_max = jnp.max(jnp.abs(rel_bias[:, :DSA_HEADS] - rel_bias[NUM_BUCKETS - 1, :DSA_HEADS])) * LOG2E
    bound = 1.02 * q_norm * k_norm + near_max
    bounded = (bound < SAFE_LOG2_RANGE).astype(I32).reshape(1)

    py, qt, iqt, gate, k, ik, vt, iwt = _even_proj(
        x, norm_g, w_all, pool_w.astype(BF16), pool_scale, q_gain, extra, k_gain, pool_width=pool_width,
        dsa_width=dsa_width, q_scale=q_scale, iw_scale=IDX_HEADS ** -0.5 * HEAD_DIM ** -0.5)
    dy = _dsa_attention(bounded, qt, iqt, iwt, gate, k, ik, vt, dsa_bias)
    w_o = w_out.astype(BF16)
    return _out_proj(x, [(py, w_o[:pool_width]), (dy, w_o[pool_width:])])


def _odd_layer(x, norm_g, w_in, q_gain, k_gain, sinks, w_out, rel_bias, swa_bias):
    q_width = SWA_Q_HEADS * HEAD_DIM
    kv_width = SWA_KV_HEADS * HEAD_DIM
    w_all = jnp.concatenate([w_in[:, :q_width], w_in[:, q_width + 2 * kv_width:],
                             w_in[:, q_width:q_width + 2 * kv_width]], axis=1).astype(BF16)
    q_scale = HEAD_DIM ** -0.5 * LOG2E
    sink_b = jnp.repeat(sinks.astype(F32) * LOG2E, QB).reshape(1, SWA_Q_HEADS * QB)
    q_norm = jnp.sqrt(HEAD_DIM * jnp.max(q_gain.astype(F32) ** 2)) * q_scale
    k_norm = jnp.sqrt(HEAD_DIM * jnp.max(k_gain.astype(F32) ** 2))
    bound = jnp.maximum(1.02 * q_norm * k_norm + jnp.max(jnp.abs(rel_bias)) * LOG2E, jnp.max(jnp.abs(sink_b)))
    bounded = (bound < SAFE_LOG2_RANGE).astype(I32).reshape(1)

    qt, gate, k_hm, vt_hm = _odd_proj(x, norm_g, w_all, q_gain, k_gain, q_width=q_width, q_scale=q_scale)
    go = _swa_attention(bounded, qt, k_hm, vt_hm, sink_b, gate, swa_bias)
    return _out_proj(x, [(go, w_out.astype(BF16))])


def kernel(x, rel_bias, even_norm, even_w_in, even_pool_w, even_pool_scale, even_q_gain, even_k_gain,
           even_w_out, odd_norm, odd_w_in, odd_q_gain, odd_k_gain, odd_sinks, odd_w_out):
    B, T, D = x.shape
    depth = even_norm.shape[0] + odd_norm.shape[0]
    dsa_bias = _bias_tiles(rel_bias, N_BIAS_TILES, DSA_HEADS, QB, QB, 0, minus_far=True)
    swa_bias = _bias_tiles(rel_bias, 1, SWA_Q_HEADS, 2 * QB, 0, QB)[0]
    outs = []
    for b in range(B):
        h = x.reshape(T, D) if B == 1 else x[b]
        for layer in range(depth):
            j = layer // 2
            if layer % 2 == 0:
                h = _even_layer(h, even_norm[j], even_w_in[j], even_pool_w[j], even_pool_scale[j],
                                even_q_gain[j], even_k_gain[j], even_w_out[j], rel_bias, dsa_bias)
            else:
                h = _odd_layer(h, odd_norm[j], odd_w_in[j], odd_q_gain[j], odd_k_gain[j],
                               odd_sinks[j], odd_w_out[j], rel_bias, swa_bias)
        outs.append(h)
    return outs[0].reshape(B, T, D) if B == 1 else jnp.stack(outs, axis=0)
```

```python
import functools

import jax
import jax.numpy as jnp
from jax import lax
from jax.experimental import pallas as pl
from jax.experimental.pallas import tpu as pltpu

F32 = jnp.float32
BF16 = jnp.bfloat16
I32 = jnp.int32

EPS = 1e-6
HEAD_DIM = 64
QB = 128
POOL_WINDOWS = (2, 4, 8, 16)
POOL_HALO = 16
DSA_HEADS = 16
IDX_HEADS = 16
DSA_TOPK = 256
SWA_Q_HEADS = 32
SWA_KV_HEADS = 4
NUM_BUCKETS = 32
IDX_CHUNK = 512
ATT_CHUNK = 256
ATT_PAIR = IDX_CHUNK // ATT_CHUNK
INT_MIN = -(2 ** 31)
FINITE_MIN_KEY = -0x7F800000
SEARCH_PASSES = 32
BLOCKS_PER_STEP = 2
MASKED = -1e30
LOG2E = 1.4426950408889634
QT_ROWS = 128
VT_ROWS = 80
SAFE_LOG2_RANGE = 60.0
TM = 256
TM_OUT = 512
VMEM_LIMIT = 56 * 1024 * 1024


def _bucket_starts():
    max_exact = NUM_BUCKETS // 2
    starts = list(range(max_exact + 1))
    n = max_exact
    for b in range(max_exact + 1, NUM_BUCKETS):
        while n ** 16 < max_exact ** 16 * 64 ** (b - max_exact):
            n += 1
        starts.append(n)
    return tuple(starts)


BUCKET_START = _bucket_starts()
FAR_DELTA = -(-(BUCKET_START[-1] + QB - 1) // QB)
N_BIAS_TILES = FAR_DELTA + 1


def _cparams(n_grid=1):
    return pltpu.CompilerParams(dimension_semantics=("arbitrary",) * n_grid,
                                vmem_limit_bytes=VMEM_LIMIT)


def _const_spec(shape):
    return pl.BlockSpec(shape, lambda i: (0,) * len(shape), pipeline_mode=pl.Buffered(1))


def _rms_rows_bf16(x, g):
    ms = jnp.mean(x * x, axis=-1, keepdims=True)
    return (x * lax.rsqrt(ms + EPS) * g).astype(BF16)


def _silu(x):
    return x / (1.0 + jnp.exp(-x))


def _bias_tiles_kernel(rb_ref, out_ref, *, n_heads, rows, base_step, base0, minus_far):
    base = base0 + pl.program_id(0) * base_step
    span = rows + QB
    dist = base - rows + lax.broadcasted_iota(I32, (8, span), 1)
    at_least = [dist >= BUCKET_START[b] for b in range(1, NUM_BUCKETS)]
    for h in range(n_heads):
        val = jnp.full((8, span), rb_ref[0, h], F32)
        for b in range(1, NUM_BUCKETS):
            val = jnp.where(at_least[b - 1], rb_ref[b, h], val)
        if minus_far:
            val = val - rb_ref[NUM_BUCKETS - 1, h]
        wide = jnp.broadcast_to(val[0:1] * LOG2E, (rows, span))
        out_ref[0, h] = pltpu.roll(wide, 0, 1, stride=1, stride_axis=0)[:, rows:]


def _bias_tiles(rel_bias, n_tiles, n_heads, rows, base_step, base0, minus_far=False):
    kern = functools.partial(_bias_tiles_kernel, n_heads=n_heads, rows=rows, base_step=base_step, base0=base0,
                             minus_far=minus_far)
    return pl.pallas_call(
        kern,
        grid=(n_tiles,),
        in_specs=[pl.BlockSpec(memory_space=pltpu.SMEM)],
        out_specs=pl.BlockSpec((1, n_heads, rows, QB), lambda t: (t, 0, 0, 0)),
        out_shape=jax.ShapeDtypeStruct((n_tiles, n_heads, rows, QB), F32),
        compiler_params=_cparams(),
        name="bias_tiles",
    )(rel_bias)


def _store_heads_t(h, out_ref, gain_ref, *, n_heads, normalize, scale):
    for b in range(TM // QB):
        rows = h[b * QB:(b + 1) * QB]
        for p in range(n_heads // 2):
            pair = rows[:, p * 128:(p + 1) * 128].T
            for hh in range(2):
                t = pair[hh * HEAD_DIM:(hh + 1) * HEAD_DIM]
                if normalize:
                    ms = jnp.mean(t * t, axis=0, keepdims=True)
                    t = t * lax.rsqrt(ms + EPS) * gain_ref[...] * scale
                head = 2 * p + hh
                out_ref[b, :HEAD_DIM, head * QB:(head + 1) * QB] = t.astype(BF16)


def _pool_mix(i, pin, gate, pw_ref, ps_ref, out_ref, halo_ref):
    width = pin.shape[1]
    gc = width // len(POOL_WINDOWS)

    @pl.when(i == 0)
    def _():
        halo_ref[...] = jnp.zeros_like(halo_ref)

    ext = jnp.concatenate([halo_ref[...], pin], axis=0)
    halo_ref[...] = pin[TM - POOL_HALO:]
    pos = i * TM + lax.broadcasted_iota(I32, (TM, gc), 0)
    outs = []
    for grp, win in enumerate(POOL_WINDOWS):
        e = ext[:, grp * gc:(grp + 1) * gc]
        s = e
        span = 1
        while span < win:
            s = s + pltpu.roll(s, span, 0)
            span *= 2
        s = s[POOL_HALO:]
        a = pin[:, grp * gc:(grp + 1) * gc]
        cnt = jnp.minimum(pos + 1, win).astype(F32)
        pooled = s / cnt - a
        y = jnp.dot(pooled.astype(BF16), pw_ref[grp], preferred_element_type=F32)
        outs.append(y * ps_ref[:, grp * gc:(grp + 1) * gc])
    py = jnp.concatenate(outs, axis=1)
    out_ref[...] = (_silu(gate) * py).astype(BF16)


def _ones_row_pad(cols):
    row = lax.broadcasted_iota(I32, (VT_ROWS - HEAD_DIM, cols), 0)
    return jnp.where(row == 0, 1.0, 0.0)


def _even_proj_kernel(x_ref, g_ref, w_ref, pw_ref, ps_ref, qg_ref, extra_ref, kg_ref,
                      py_ref, qt_ref, iqt_ref, gate_ref, k_ref, ik_ref, vt_ref, iwt_ref, halo_ref,
                      *, pool_w, dsa_w, q_scale, iw_scale):
    i = pl.program_id(0)
    xn = _rms_rows_bf16(x_ref[...], g_ref[...])
    col = 0

    def proj(width):
        nonlocal col
        h = jnp.dot(xn, w_ref[:, col:col + width], preferred_element_type=F32)
        col += width
        return h

    h = proj(2 * pool_w)
    _pool_mix(i, h[:, :pool_w], h[:, pool_w:], pw_ref, ps_ref, py_ref, halo_ref)

    for b in range(TM // QB):
        qt_ref[b, HEAD_DIM:, :] = extra_ref[...]
    _store_heads_t(proj(dsa_w), qt_ref, qg_ref, n_heads=DSA_HEADS, normalize=True, scale=q_scale)
    _store_heads_t(proj(IDX_HEADS * HEAD_DIM), iqt_ref, qg_ref, n_heads=IDX_HEADS, normalize=False, scale=1.0)

    h = proj(dsa_w + 3 * 128)
    gate_ref[...] = _silu(h[:, :dsa_w]).astype(BF16)
    k = h[:, dsa_w:dsa_w + 128]
    ms = jnp.sum(k * k, axis=-1, keepdims=True) * (1.0 / HEAD_DIM)
    kn = k * lax.rsqrt(ms + EPS) * kg_ref[...]
    lane = lax.broadcasted_iota(I32, kn.shape, 1)
    k_ref[...] = jnp.where((lane == HEAD_DIM) | (lane == HEAD_DIM + 1), 1.0, kn).astype(BF16)
    ik_ref[...] = h[:, dsa_w + 128:dsa_w + 128 + HEAD_DIM].astype(BF16)
    vw = h[:, dsa_w + 256:dsa_w + 384].T
    vt_ref[0] = jnp.concatenate([vw[:HEAD_DIM], _ones_row_pad(TM)], axis=0).astype(BF16)
    iwt_ref[...] = vw[HEAD_DIM:HEAD_DIM + IDX_HEADS] * iw_scale


def _even_proj(x, g, w, pool_w, pool_scale, q_gain, extra, k_gain, *, pool_width, dsa_width, q_scale, iw_scale):
    T, D = x.shape
    assert TM == ATT_CHUNK
    nb = T // QB
    lanes = DSA_HEADS * QB
    gain_b = jnp.broadcast_to(q_gain.reshape(HEAD_DIM, 1), (HEAD_DIM, QB)).astype(F32)
    kern = functools.partial(_even_proj_kernel, pool_w=pool_width, dsa_w=dsa_width, q_scale=q_scale,
                             iw_scale=iw_scale)
    row_blk = lambda width: pl.BlockSpec((TM, width), lambda i: (i, 0))
    return pl.pallas_call(
        kern,
        grid=(T // TM,),
        in_specs=[row_blk(D),
                  _const_spec((1, D)),
                  _const_spec(w.shape),
                  _const_spec(pool_w.shape),
                  _const_spec((1, pool_width)),
                  _const_spec((HEAD_DIM, QB)),
                  _const_spec(extra.shape),
                  _const_spec((1, QT_ROWS))],
        out_specs=[row_blk(pool_width),
                   pl.BlockSpec((TM // QB, QT_ROWS, lanes), lambda i: (i, 0, 0)),
                   pl.BlockSpec((TM // QB, HEAD_DIM, lanes), lambda i: (i, 0, 0)),
                   row_blk(dsa_width),
                   row_blk(QT_ROWS),
                   row_blk(HEAD_DIM),
                   pl.BlockSpec((1, VT_ROWS, ATT_CHUNK), lambda i: (i, 0, 0)),
                   pl.BlockSpec((IDX_HEADS, TM), lambda i: (0, i))],
        out_shape=[jax.ShapeDtypeStruct((T, pool_width), BF16),
                   jax.ShapeDtypeStruct((nb, QT_ROWS, lanes), BF16),
                   jax.ShapeDtypeStruct((nb, HEAD_DIM, lanes), BF16),
                   jax.ShapeDtypeStruct((T, dsa_width), BF16),
                   jax.ShapeDtypeStruct((T, QT_ROWS), BF16),
                   jax.ShapeDtypeStruct((T, HEAD_DIM), BF16),
                   jax.ShapeDtypeStruct((T // ATT_CHUNK, VT_ROWS, ATT_CHUNK), BF16),
                   jax.ShapeDtypeStruct((IDX_HEADS, T), F32)],
        scratch_shapes=[pltpu.VMEM((POOL_HALO, pool_width), F32)],
        compiler_params=_cparams(),
        name="even_proj",
    )(x, g.reshape(1, D), w, pool_w, pool_scale.reshape(1, pool_width), gain_b, extra,
      _pad_cols(k_gain.reshape(1, HEAD_DIM), QT_ROWS))


def _odd_proj_kernel(x_ref, g_ref, w_ref, qg_ref, kg_ref, qt_ref, gate_ref, k_ref, vt_ref, *, q_w, q_scale):
    xn = _rms_rows_bf16(x_ref[...], g_ref[...])
    hq = jnp.dot(xn, w_ref[:, :q_w], preferred_element_type=F32)
    _store_heads_t(hq, qt_ref, qg_ref, n_heads=SWA_Q_HEADS, normalize=True, scale=q_scale)
    gate_ref[...] = _silu(jnp.dot(xn, w_ref[:, q_w:2 * q_w], preferred_element_type=F32)).astype(BF16)
    h = jnp.dot(xn, w_ref[:, 2 * q_w:], preferred_element_type=F32)
    kv_w = SWA_KV_HEADS * HEAD_DIM
    for hd in range(SWA_KV_HEADS):
        k = h[:, hd * HEAD_DIM:(hd + 1) * HEAD_DIM]
        ms = jnp.mean(k * k, axis=-1, keepdims=True)
        k_ref[hd] = (k * lax.rsqrt(ms + EPS) * kg_ref[...]).astype(BF16)
    ones_rows = _ones_row_pad(TM)
    for p in range(SWA_KV_HEADS // 2):
        pair = h[:, kv_w + p * 128:kv_w + (p + 1) * 128].T
        vt_ref[2 * p] = jnp.concatenate([pair[:HEAD_DIM], ones_rows], axis=0).astype(BF16)
        vt_ref[2 * p + 1] = jnp.concatenate([pair[HEAD_DIM:], ones_rows], axis=0).astype(BF16)


def _odd_proj(x, g, w, q_gain, k_gain, *, q_width, q_scale):
    T, D = x.shape
    nb = T // QB
    lanes = SWA_Q_HEADS * QB
    gain_b = jnp.broadcast_to(q_gain.reshape(HEAD_DIM, 1), (HEAD_DIM, QB)).astype(F32)
    kern = functools.partial(_odd_proj_kernel, q_w=q_width, q_scale=q_scale)
    return pl.pallas_call(
        kern,
        grid=(T // TM,),
        in_specs=[pl.BlockSpec((TM, D), lambda i: (i, 0)),
                  _const_spec((1, D)),
                  _const_spec(w.shape),
                  _const_spec((HEAD_DIM, QB)),
                  _const_spec((1, HEAD_DIM))],
        out_specs=[pl.BlockSpec((TM // QB, HEAD_DIM, lanes), lambda i: (i, 0, 0)),
                   pl.BlockSpec((TM, q_width), lambda i: (i, 0)),
                   pl.BlockSpec((SWA_KV_HEADS, TM, HEAD_DIM), lambda i: (0, i, 0)),
                   pl.BlockSpec((SWA_KV_HEADS, VT_ROWS, TM), lambda i: (0, 0, i))],
        out_shape=[jax.ShapeDtypeStruct((nb, HEAD_DIM, lanes), BF16),
                   jax.ShapeDtypeStruct((T, q_width), BF16),
                   jax.ShapeDtypeStruct((SWA_KV_HEADS, T, HEAD_DIM), BF16),
                   jax.ShapeDtypeStruct((SWA_KV_HEADS, VT_ROWS, T), BF16)],
        compiler_params=_cparams(),
        name="odd_proj",
    )(x, g.reshape(1, D), w, gain_b, k_gain.reshape(1, HEAD_DIM))


def _out_proj_kernel(*refs, n_terms):
    x_ref = refs[0]
    out_ref = refs[-1]
    acc = x_ref[...]
    for t in range(n_terms):
        acc = acc + jnp.dot(refs[1 + 2 * t][...], refs[2 + 2 * t][...], preferred_element_type=F32)
    out_ref[...] = acc


def _out_proj(x, terms):
    T, D = x.shape
    args = [x]
    in_specs = [pl.BlockSpec((TM_OUT, D), lambda i: (i, 0))]
    for a, w in terms:
        args += [a, w]
        in_specs += [pl.BlockSpec((TM_OUT, a.shape[1]), lambda i: (i, 0)), _const_spec(w.shape)]
    return pl.pallas_call(
        functools.partial(_out_proj_kernel, n_terms=len(terms)),
        grid=(T // TM_OUT,),
        in_specs=in_specs,
        out_specs=pl.BlockSpec((TM_OUT, D), lambda i: (i, 0)),
        out_shape=jax.ShapeDtypeStruct((T, D), F32),
        compiler_params=_cparams(),
        name="out_proj",
    )(*args)


def _store_gated(o_t, gate_ref, out_ref, n_heads):
    for p in range(n_heads // 2):
        pair = jnp.concatenate([o_t[:, (2 * p) * QB:(2 * p + 1) * QB],
                                o_t[:, (2 * p + 1) * QB:(2 * p + 2) * QB]], axis=0)
        g = gate_ref[:, p * 128:(p + 1) * 128].astype(F32)
        out_ref[:, p * 128:(p + 1) * 128] = (g * pair.T).astype(BF16)


def _dsa_kernel(bounded_ref, qt_ref, iqt_ref, iwt_ref, gate_ref, k_ref, ik_ref, vt_ref, bias_ref, out_ref,
                score_ref, thr_ref):
    s = pl.program_id(0)
    ns = pl.num_programs(0) - 1
    lanes = DSA_HEADS * QB
    sub = ATT_CHUNK // QB
    cur = s
    prv = jnp.maximum(s - 1, 0)
    pairs_per_chunk = IDX_CHUNK // (BLOCKS_PER_STEP * QB)
    n_cur = cur // pairs_per_chunk + 1
    n_prv = prv // pairs_per_chunk + 1
    cur_slot = cur % 2
    prv_slot = prv % 2
    both = range(BLOCKS_PER_STEP)

    @pl.when(s < ns)
    def _():
        iqt = [iqt_ref[j] for j in both]
        iwt = iwt_ref[...]
        krow = lax.broadcasted_iota(I32, (IDX_CHUNK, QB), 0)
        lane_pos = lax.broadcasted_iota(I32, (IDX_CHUNK, QB), 1)

        def idx_body(c, carry):
            off = pl.multiple_of(c * IDX_CHUNK, IDX_CHUNK)
            ik = ik_ref[pl.ds(off, IDX_CHUNK), :]
            for j in both:
                sc = jnp.dot(ik, iqt[j], preferred_element_type=F32)
                score = jnp.zeros((IDX_CHUNK, QB), F32)
                for h in range(IDX_HEADS):
                    score = score + jnp.maximum(sc[:, h * QB:(h + 1) * QB], 0.0) * iwt[h:h + 1, j * QB:(j + 1) * QB]
                causal = off + krow <= (cur * BLOCKS_PER_STEP + j) * QB + lane_pos
                score_ref[cur_slot, j, pl.ds(off, IDX_CHUNK), :] = jnp.where(causal, score, -jnp.inf)
            return carry

        lax.fori_loop(0, n_cur, idx_body, 0)

    def key_to_float(key):
        return lax.bitcast_convert_type(jnp.where(key < 0, key ^ 0x7FFFFFFF, key), F32)

    def search_unit(state):
        bases, bit, accs, ch = state
        off = pl.multiple_of(ch * IDX_CHUNK, IDX_CHUNK)
        last = ch + 1 == n_cur
        new_bases, new_accs = [], []
        for j in both:
            cand = key_to_float(bases[j] + bit)
            hit = jnp.where(score_ref[cur_slot, j, pl.ds(off, IDX_CHUNK), :] >= cand, 1, 0).astype(I32)
            acc = accs[j] + jnp.sum(hit.reshape(IDX_CHUNK // 8, 8, QB), axis=0)
            accept = jnp.sum(acc, axis=0, keepdims=True) >= DSA_TOPK
            new_bases.append(jnp.where(last & accept, bases[j] + bit, bases[j]))
            new_accs.append(jnp.where(last, 0, acc))
        bit = jnp.where(last, lax.shift_right_logical(bit, 1), bit)
        ch = jnp.where(last, 0, ch + 1)
        return tuple(new_bases), bit, tuple(new_accs), ch

    search_init = (tuple(jnp.full((1, QB), INT_MIN, I32) for _ in both), jnp.int32(INT_MIN),
                   tuple(jnp.zeros((8, QB), I32) for _ in both), jnp.int32(0))

    def store_thresholds(state):
        for j in both:
            thr = key_to_float(jnp.maximum(state[0][j], FINITE_MIN_KEY))
            thr_ref[cur_slot, j] = jnp.broadcast_to(thr, (8, QB))

    qts = [qt_ref[j] for j in both]
    thr_prv = [thr_ref[prv_slot, j][0:1] for j in both]

    def tiles(j, c, near, thr, fn):
        off = pl.multiple_of(c * ATT_CHUNK, ATT_CHUNK)
        sc = jnp.dot(k_ref[pl.ds(off, ATT_CHUNK), :], qts[j], preferred_element_type=F32)
        sel = score_ref[prv_slot, j, pl.ds(off, ATT_CHUNK), :] >= thr
        blocks = []
        for sb in range(sub):
            delta = jnp.clip(prv * BLOCKS_PER_STEP + j - (c * sub + sb), 0, FAR_DELTA)
            sel_sb = sel[sb * QB:(sb + 1) * QB]
            row = []
            for h in range(DSA_HEADS):
                t = sc[sb * QB:(sb + 1) * QB, h * QB:(h + 1) * QB]
                if near:
                    t = t + bias_ref[delta, h]
                row.append(fn(t, sel_sb))
            blocks.append(jnp.concatenate(row, axis=1))
        return jnp.concatenate(blocks, axis=0)

    def bounded_pair(j, near, pair, thr):
        pvs = []
        for half in range(ATT_PAIR):
            c = pair * ATT_PAIR + half
            p = tiles(j, c, near, thr, lambda t, sel: jnp.where(sel, jnp.exp2(t), 0.0))
            pvs.append(jnp.dot(vt_ref[c], p.astype(BF16), preferred_element_type=F32))
        return sum(pvs[1:], pvs[0])

    n_far = jnp.maximum(prv * BLOCKS_PER_STEP - (FAR_DELTA - 1), 0) // (sub * ATT_PAIR)
    acc0 = tuple(jnp.zeros((VT_ROWS, lanes), F32) for _ in both)

    def store_block(j, o_t):
        rows = slice(j * QB, (j + 1) * QB)
        for p in range(DSA_HEADS // 2):
            pair = jnp.concatenate([o_t[:, (2 * p) * QB:(2 * p + 1) * QB],
                                    o_t[:, (2 * p + 1) * QB:(2 * p + 2) * QB]], axis=0)
            g = gate_ref[rows, p * 128:(p + 1) * 128].astype(F32)
            out_ref[rows, p * 128:(p + 1) * 128] = (g * pair.T).astype(BF16)

    def finish_bounded(accs):
        for j in both:
            store_block(j, accs[j][:HEAD_DIM] / accs[j][HEAD_DIM:HEAD_DIM + 1])

    def attend(near, c, accs, thrs):
        return tuple(accs[j] + bounded_pair(j, near, c, thrs[j]) for j in both)

    def fused_body(near, c, carry):
        accs, state = carry
        thrs = [jnp.where(c < n_prv, thr_prv[j], jnp.inf) for j in both]
        accs = attend(near, jnp.minimum(c, n_prv - 1), accs, thrs)
        for _ in range(SEARCH_PASSES):
            state = search_unit(state)
        return accs, state

    def fused_path():
        carry = (acc0, search_init)
        carry = lax.fori_loop(0, n_far, functools.partial(fused_body, False), carry)
        accs, state = lax.fori_loop(n_far, n_cur, functools.partial(fused_body, True), carry)
        store_thresholds(state)
        finish_bounded(accs)
        return 0

    def general_body(j, c, carry):
        m, l, acc = carry
        t = tiles(j, c, True, thr_prv[j], lambda t, sel: jnp.where(sel, t, MASKED))
        m_new = jnp.maximum(m, jnp.max(t, axis=0, keepdims=True))
        alpha = jnp.exp2(m - m_new)
        p = jnp.exp2(t - m_new)
        l_new = alpha * l + jnp.sum(p, axis=0, keepdims=True)
        pv = jnp.dot(vt_ref[c], p.astype(BF16), preferred_element_type=F32)
        return m_new, l_new, acc * alpha + pv

    def separate_path():
        @pl.when(s < ns)
        def _():
            store_thresholds(lax.fori_loop(0, SEARCH_PASSES * n_cur, lambda _, st: search_unit(st), search_init))

        @pl.when((s >= 1) & (bounded_ref[0] > 0))
        def _():
            accs = lax.fori_loop(0, n_far, lambda c, a: attend(False, c, a, thr_prv), acc0)
            finish_bounded(lax.fori_loop(n_far, n_prv, lambda c, a: attend(True, c, a, thr_prv), accs))

        @pl.when((s >= 1) & (bounded_ref[0] <= 0))
        def _():
            for j in both:
                m0 = jnp.full((1, lanes), MASKED, F32)
                l0 = jnp.zeros((1, lanes), F32)
                _, l, acc = lax.fori_loop(0, n_prv * ATT_PAIR, functools.partial(general_body, j), (m0, l0, acc0[j]))
                store_block(j, acc[:HEAD_DIM] / l)

        return 0

    lax.cond((bounded_ref[0] > 0) & (s >= 1) & (s < ns), fused_path, separate_path)


def _dsa_attention(bounded, qt, iqt, iwt, gate, k, ik, vt, bias):
    ns = qt.shape[0] // BLOCKS_PER_STEP
    T = k.shape[0]
    width = DSA_HEADS * HEAD_DIM
    rows = BLOCKS_PER_STEP * QB

    def prv(s):
        return jnp.maximum(s - 1, 0)

    def cur(s):
        return jnp.minimum(s, ns - 1)

    return pl.pallas_call(
        _dsa_kernel,
        grid=(ns + 1,),
        in_specs=[pl.BlockSpec(memory_space=pltpu.SMEM),
                  pl.BlockSpec((BLOCKS_PER_STEP, QT_ROWS, DSA_HEADS * QB), lambda s: (prv(s), 0, 0)),
                  pl.BlockSpec((BLOCKS_PER_STEP, HEAD_DIM, IDX_HEADS * QB), lambda s: (cur(s), 0, 0)),
                  pl.BlockSpec((IDX_HEADS, rows), lambda s: (0, cur(s))),
                  pl.BlockSpec((rows, width), lambda s: (prv(s), 0)),
                  _const_spec(k.shape),
                  _const_spec(ik.shape),
                  _const_spec(vt.shape),
                  _const_spec(bias.shape)],
        out_specs=pl.BlockSpec((rows, width), lambda s: (prv(s), 0)),
        out_shape=jax.ShapeDtypeStruct((T, width), BF16),
        scratch_shapes=[pltpu.VMEM((2, BLOCKS_PER_STEP, T, QB), F32), pltpu.VMEM((2, BLOCKS_PER_STEP, 8, QB), F32)],
        compiler_params=_cparams(),
        name="dsa_attention",
    )(bounded, qt, iqt, iwt, gate, k, ik, vt, bias)


def _swa_kernel(bounded_ref, qt_ref, kp_ref, kc_ref, vp_ref, vc_ref, sink_ref, gate_ref, bias_ref, out_ref):
    i = pl.program_id(0)
    grp = SWA_Q_HEADS // SWA_KV_HEADS
    shape = (2 * QB, QB)
    c_idx = lax.broadcasted_iota(I32, shape, 0)
    dist = QB + lax.broadcasted_iota(I32, shape, 1) - c_idx
    mask = (dist >= 0) & (dist < QB) & ((c_idx >= QB) | (i > 0))

    def group(kv, bounded):
        kcat = jnp.concatenate([kp_ref[kv], kc_ref[kv]], axis=0)
        vcat = jnp.concatenate([vp_ref[kv], vc_ref[kv]], axis=1)
        qg = qt_ref[0, :, kv * grp * QB:(kv + 1) * grp * QB]
        s = jnp.dot(kcat, qg, preferred_element_type=F32)
        sink = sink_ref[:, kv * grp * QB:(kv + 1) * grp * QB]
        logits = [s[:, j * QB:(j + 1) * QB] + bias_ref[kv * grp + j] for j in range(grp)]
        if bounded:
            p = jnp.concatenate([jnp.where(mask, jnp.exp2(t), 0.0) for t in logits], axis=1)
            pv = jnp.dot(vcat, p.astype(BF16), preferred_element_type=F32)
            return pv[:HEAD_DIM] / (pv[HEAD_DIM:HEAD_DIM + 1] + jnp.exp2(sink))
        t = jnp.concatenate([jnp.where(mask, t, MASKED) for t in logits], axis=1)
        m = jnp.maximum(jnp.max(t, axis=0, keepdims=True), sink)
        p = jnp.exp2(t - m)
        l = jnp.sum(p, axis=0, keepdims=True) + jnp.exp2(sink - m)
        pv = jnp.dot(vcat, p.astype(BF16), preferred_element_type=F32)
        return pv[:HEAD_DIM] / l

    def path(bounded):
        return jnp.concatenate([group(kv, bounded) for kv in range(SWA_KV_HEADS)], axis=1)

    o_t = lax.cond(bounded_ref[0] > 0, functools.partial(path, True), functools.partial(path, False))
    _store_gated(o_t, gate_ref, out_ref, SWA_Q_HEADS)


def _swa_attention(bounded, qt, k_hm, vt_hm, sink_b, gate, bias):
    nb = qt.shape[0]
    T = k_hm.shape[1]
    width = SWA_Q_HEADS * HEAD_DIM

    def prev(i):
        return jnp.maximum(i - 1, 0)

    return pl.pallas_call(
        _swa_kernel,
        grid=(nb,),
        in_specs=[pl.BlockSpec(memory_space=pltpu.SMEM),
                  pl.BlockSpec((1, HEAD_DIM, SWA_Q_HEADS * QB), lambda i: (i, 0, 0)),
                  pl.BlockSpec((SWA_KV_HEADS, QB, HEAD_DIM), lambda i: (0, prev(i), 0)),
                  pl.BlockSpec((SWA_KV_HEADS, QB, HEAD_DIM), lambda i: (0, i, 0)),
                  pl.BlockSpec((SWA_KV_HEADS, VT_ROWS, QB), lambda i: (0, 0, prev(i))),
                  pl.BlockSpec((SWA_KV_HEADS, VT_ROWS, QB), lambda i: (0, 0, i)),
                  _const_spec((1, SWA_Q_HEADS * QB)),
                  pl.BlockSpec((QB, width), lambda i: (i, 0)),
                  _const_spec(bias.shape)],
        out_specs=pl.BlockSpec((QB, width), lambda i: (i, 0)),
        out_shape=jax.ShapeDtypeStruct((T, width), BF16),
        compiler_params=_cparams(),
        name="swa_attention",
    )(bounded, qt, k_hm, k_hm, vt_hm, vt_hm, sink_b, gate, bias)


def _pad_cols(w, width):
    return jnp.pad(w, ((0, 0), (0, width - w.shape[1])))


def _even_layer(x, norm_g, w_in, pool_w, pool_scale, q_gain, k_gain, w_out, rel_bias, dsa_bias):
    pool_width = pool_w.shape[0] * pool_w.shape[1]
    dsa_width = DSA_HEADS * HEAD_DIM
    idx_width = IDX_HEADS * HEAD_DIM
    sizes = (pool_width, pool_width, dsa_width, HEAD_DIM, HEAD_DIM, dsa_width, idx_width, HEAD_DIM, IDX_HEADS)
    offs = [0]
    for s in sizes:
        offs.append(offs[-1] + s)
    col = lambda j: w_in[:, offs[j]:offs[j + 1]]
    w_all = jnp.concatenate([col(0), col(1), col(2), col(6), col(5), _pad_cols(col(3), 128), _pad_cols(col(7), 128),
                             _pad_cols(jnp.concatenate([col(4), col(8)], axis=1), 128)], axis=1).astype(BF16)

    q_scale = HEAD_DIM ** -0.5 * LOG2E
    far = rel_bias[NUM_BUCKETS - 1, :DSA_HEADS].astype(F32) * LOG2E
    far_hi = far.astype(BF16)
    far_lo = (far - far_hi.astype(F32)).astype(BF16)
    extra = jnp.zeros((QT_ROWS - HEAD_DIM, DSA_HEADS * QB), BF16)
    extra = extra.at[0].set(jnp.repeat(far_hi, QB)).at[1].set(jnp.repeat(far_lo, QB))
    q_norm = jnp.sqrt(HEAD_DIM * jnp.max(q_gain.astype(F32) ** 2) * q_scale ** 2 + jnp.max(far ** 2))
    k_norm = jnp.sqrt(HEAD_DIM * jnp.max(k_gain.astype(F32) ** 2) + 2.0)
    near_max = jnp.max(jnp.abs(rel_bias[:, :DSA_HEADS] - rel_bias[NUM_BUCKETS - 1, :DSA_HEADS])) * LOG2E
    bound = 1.02 * q_norm * k_norm + near_max
    bounded = (bound < SAFE_LOG2_RANGE).astype(I32).reshape(1)

    py, qt, iqt, gate, k, ik, vt, iwt = _even_proj(
        x, norm_g, w_all, pool_w.astype(BF16), pool_scale, q_gain, extra, k_gain, pool_width=pool_width,
        dsa_width=dsa_width, q_scale=q_scale, iw_scale=IDX_HEADS ** -0.5 * HEAD_DIM ** -0.5)
    dy = _dsa_attention(bounded, qt, iqt, iwt, gate, k, ik, vt, dsa_bias)
    w_o = w_out.astype(BF16)
    return _out_proj(x, [(py, w_o[:pool_width]), (dy, w_o[pool_width:])])


def _odd_layer(x, norm_g, w_in, q_gain, k_gain, sinks, w_out, rel_bias, swa_bias):
    q_width = SWA_Q_HEADS * HEAD_DIM
    kv_width = SWA_KV_HEADS * HEAD_DIM
    w_all = jnp.concatenate([w_in[:, :q_width], w_in[:, q_width + 2 * kv_width:],
                             w_in[:, q_width:q_width + 2 * kv_width]], axis=1).astype(BF16)
    q_scale = HEAD_DIM ** -0.5 * LOG2E
    sink_b = jnp.repeat(sinks.astype(F32) * LOG2E, QB).reshape(1, SWA_Q_HEADS * QB)
    q_norm = jnp.sqrt(HEAD_DIM * jnp.max(q_gain.astype(F32) ** 2)) * q_scale
    k_norm = jnp.sqrt(HEAD_DIM * jnp.max(k_gain.astype(F32) ** 2))
    bound = jnp.maximum(1.02 * q_norm * k_norm + jnp.max(jnp.abs(rel_bias)) * LOG2E, jnp.max(jnp.abs(sink_b)))
    bounded = (bound < SAFE_LOG2_RANGE).astype(I32).reshape(1)

    qt, gate, k_hm, vt_hm = _odd_proj(x, norm_g, w_all, q_gain, k_gain, q_width=q_width, q_scale=q_scale)
    go = _swa_attention(bounded, qt, k_hm, vt_hm, sink_b, gate, swa_bias)
    return _out_proj(x, [(go, w_out.astype(BF16))])


def kernel(x, rel_bias, even_norm, even_w_in, even_pool_w, even_pool_scale, even_q_gain, even_k_gain,
           even_w_out, odd_norm, odd_w_in, odd_q_gain, odd_k_gain, odd_sinks, odd_w_out):
    B, T, D = x.shape
    depth = even_norm.shape[0] + odd_norm.shape[0]
    dsa_bias = _bias_tiles(rel_bias, N_BIAS_TILES, DSA_HEADS, QB, QB, 0, minus_far=True)
    swa_bias = _bias_tiles(rel_bias, 1, SWA_Q_HEADS, 2 * QB, 0, QB)[0]
    outs = []
    for b in range(B):
        h = x.reshape(T, D) if B == 1 else x[b]
        for layer in range(depth):
            j = layer // 2
            if layer % 2 == 0:
                h = _even_layer(h, even_norm[j], even_w_in[j], even_pool_w[j], even_pool_scale[j],
                                even_q_gain[j], even_k_gain[j], even_w_out[j], rel_bias, dsa_bias)
            else:
                h = _odd_layer(h, odd_norm[j], odd_w_in[j], odd_q_gain[j], odd_k_gain[j],
                               odd_sinks[j], odd_w_out[j], rel_bias, swa_bias)
        outs.append(h)
    return outs[0].reshape(B, T, D) if B == 1 else jnp.stack(outs, axis=0)
```

```python
import functools

import jax
import jax.numpy as jnp
from jax import lax
from jax.experimental import pallas as pl
from jax.experimental.pallas import tpu as pltpu

F32 = jnp.float32
BF16 = jnp.bfloat16
I32 = jnp.int32

EPS = 1e-6
HEAD_DIM = 64
QB = 128
POOL_WINDOWS = (2, 4, 8, 16)
POOL_HALO = 16
DSA_HEADS = 16
IDX_HEADS = 16
DSA_TOPK = 256
SWA_Q_HEADS = 32
SWA_KV_HEADS = 4
NUM_BUCKETS = 32
IDX_CHUNK = 512
ATT_CHUNK = 256
ATT_PAIR = IDX_CHUNK // ATT_CHUNK
INT_MIN = -(2 ** 31)
FINITE_MIN_KEY = -0x7F800000
SEARCH_PASSES = 32
BLOCKS_PER_STEP = 2
MASKED = -1e30
LOG2E = 1.4426950408889634
QT_ROWS = 128
VT_ROWS = 80
SAFE_LOG2_RANGE = 60.0
TM = 256
TM_OUT = 512
VMEM_LIMIT = 56 * 1024 * 1024


def _bucket_starts():
    max_exact = NUM_BUCKETS // 2
    starts = list(range(max_exact + 1))
    n = max_exact
    for b in range(max_exact + 1, NUM_BUCKETS):
        while n ** 16 < max_exact ** 16 * 64 ** (b - max_exact):
            n += 1
        starts.append(n)
    return tuple(starts)


BUCKET_START = _bucket_starts()
FAR_DELTA = -(-(BUCKET_START[-1] + QB - 1) // QB)
N_BIAS_TILES = FAR_DELTA + 1


def _cparams(n_grid=1):
    return pltpu.CompilerParams(dimension_semantics=("arbitrary",) * n_grid,
                                vmem_limit_bytes=VMEM_LIMIT)


def _const_spec(shape):
    return pl.BlockSpec(shape, lambda i: (0,) * len(shape), pipeline_mode=pl.Buffered(1))


def _rms_rows_bf16(x, g):
    ms = jnp.mean(x * x, axis=-1, keepdims=True)
    return (x * lax.rsqrt(ms + EPS) * g).astype(BF16)


def _silu(x):
    return x / (1.0 + jnp.exp(-x))


def _bias_tiles_kernel(rb_ref, out_ref, *, n_heads, rows, base_step, base0, minus_far):
    base = base0 + pl.program_id(0) * base_step
    span = rows + QB
    dist = base - rows + lax.broadcasted_iota(I32, (8, span), 1)
    at_least = [dist >= BUCKET_START[b] for b in range(1, NUM_BUCKETS)]
    for h in range(n_heads):
        val = jnp.full((8, span), rb_ref[0, h], F32)
        for b in range(1, NUM_BUCKETS):
            val = jnp.where(at_least[b - 1], rb_ref[b, h], val)
        if minus_far:
            val = val - rb_ref[NUM_BUCKETS - 1, h]
        wide = jnp.broadcast_to(val[0:1] * LOG2E, (rows, span))
        out_ref[0, h] = pltpu.roll(wide, 0, 1, stride=1, stride_axis=0)[:, rows:]


def _bias_tiles(rel_bias, n_tiles, n_heads, rows, base_step, base0, minus_far=False):
    kern = functools.partial(_bias_tiles_kernel, n_heads=n_heads, rows=rows, base_step=base_step, base0=base0,
                             minus_far=minus_far)
    return pl.pallas_call(
        kern,
        grid=(n_tiles,),
        in_specs=[pl.BlockSpec(memory_space=pltpu.SMEM)],
        out_specs=pl.BlockSpec((1, n_heads, rows, QB), lambda t: (t, 0, 0, 0)),
        out_shape=jax.ShapeDtypeStruct((n_tiles, n_heads, rows, QB), F32),
        compiler_params=_cparams(),
        name="bias_tiles",
    )(rel_bias)


def _store_heads_t(h, out_ref, gain_ref, *, n_heads, normalize, scale):
    for b in range(TM // QB):
        rows = h[b * QB:(b + 1) * QB]
        for p in range(n_heads // 2):
            pair = rows[:, p * 128:(p + 1) * 128].T
            for hh in range(2):
                t = pair[hh * HEAD_DIM:(hh + 1) * HEAD_DIM]
                if normalize:
                    ms = jnp.mean(t * t, axis=0, keepdims=True)
                    t = t * lax.rsqrt(ms + EPS) * gain_ref[...] * scale
                head = 2 * p + hh
                out_ref[b, :HEAD_DIM, head * QB:(head + 1) * QB] = t.astype(BF16)


def _pool_mix(i, pin, gate, pw_ref, ps_ref, out_ref, halo_ref):
    width = pin.shape[1]
    gc = width // len(POOL_WINDOWS)

    @pl.when(i == 0)
    def _():
        halo_ref[...] = jnp.zeros_like(halo_ref)

    ext = jnp.concatenate([halo_ref[...], pin], axis=0)
    halo_ref[...] = pin[TM - POOL_HALO:]
    pos = i * TM + lax.broadcasted_iota(I32, (TM, gc), 0)
    outs = []
    for grp, win in enumerate(POOL_WINDOWS):
        e = ext[:, grp * gc:(grp + 1) * gc]
        s = e
        span = 1
        while span < win:
            s = s + pltpu.roll(s, span, 0)
            span *= 2
        s = s[POOL_HALO:]
        a = pin[:, grp * gc:(grp + 1) * gc]
        cnt = jnp.minimum(pos + 1, win).astype(F32)
        pooled = s / cnt - a
        y = jnp.dot(pooled.astype(BF16), pw_ref[grp], preferred_element_type=F32)
        outs.append(y * ps_ref[:, grp * gc:(grp + 1) * gc])
    py = jnp.concatenate(outs, axis=1)
    out_ref[...] = (_silu(gate) * py).astype(BF16)


def _ones_row_pad(cols):
    row = lax.broadcasted_iota(I32, (VT_ROWS - HEAD_DIM, cols), 0)
    return jnp.where(row == 0, 1.0, 0.0)


def _even_proj_kernel(x_ref, g_ref, w_ref, pw_ref, ps_ref, qg_ref, extra_ref, kg_ref,
                      py_ref, qt_ref, iqt_ref, gate_ref, k_ref, ik_ref, vt_ref, iwt_ref, halo_ref,
                      *, pool_w, dsa_w, q_scale, iw_scale):
    i = pl.program_id(0)
    xn = _rms_rows_bf16(x_ref[...], g_ref[...])
    col = 0

    def proj(width):
        nonlocal col
        h = jnp.dot(xn, w_ref[:, col:col + width], preferred_element_type=F32)
        col += width
        return h

    h = proj(2 * pool_w)
    _pool_mix(i, h[:, :pool_w], h[:, pool_w:], pw_ref, ps_ref, py_ref, halo_ref)

    for b in range(TM // QB):
        qt_ref[b, HEAD_DIM:, :] = extra_ref[...]
    _store_heads_t(proj(dsa_w), qt_ref, qg_ref, n_heads=DSA_HEADS, normalize=True, scale=q_scale)

    kv = proj(2 * HEAD_DIM)
    lane = lax.broadcasted_iota(I32, kv.shape, 1)
    is_k = lane < HEAD_DIM
    ms = jnp.sum(jnp.where(is_k, kv * kv, 0.0), axis=-1, keepdims=True) * (1.0 / HEAD_DIM)
    kn = kv * lax.rsqrt(ms + EPS) * kg_ref[...]
    k_ref[...] = jnp.where(is_k, kn, jnp.where(lane < HEAD_DIM + 2, 1.0, 0.0)).astype(BF16)
    vt_ref[0] = jnp.concatenate([kv.T[HEAD_DIM:], _ones_row_pad(TM)], axis=0).astype(BF16)

    gate_ref[...] = _silu(proj(dsa_w)).astype(BF16)
    _store_heads_t(proj(IDX_HEADS * HEAD_DIM), iqt_ref, qg_ref, n_heads=IDX_HEADS, normalize=False, scale=1.0)

    iw = proj(128)
    ik_ref[...] = iw[:, :HEAD_DIM].astype(BF16)
    iwt_ref[...] = iw.T[HEAD_DIM:HEAD_DIM + IDX_HEADS] * iw_scale


def _even_proj(x, g, w, pool_w, pool_scale, q_gain, extra, k_gain, *, pool_width, dsa_width, q_scale, iw_scale):
    T, D = x.shape
    assert TM == ATT_CHUNK
    nb = T // QB
    lanes = DSA_HEADS * QB
    gain_b = jnp.broadcast_to(q_gain.reshape(HEAD_DIM, 1), (HEAD_DIM, QB)).astype(F32)
    kern = functools.partial(_even_proj_kernel, pool_w=pool_width, dsa_w=dsa_width, q_scale=q_scale,
                             iw_scale=iw_scale)
    row_blk = lambda width: pl.BlockSpec((TM, width), lambda i: (i, 0))
    return pl.pallas_call(
        kern,
        grid=(T // TM,),
        in_specs=[row_blk(D),
                  _const_spec((1, D)),
                  _const_spec(w.shape),
                  _const_spec(pool_w.shape),
                  _const_spec((1, pool_width)),
                  _const_spec((HEAD_DIM, QB)),
                  _const_spec(extra.shape),
                  _const_spec((1, QT_ROWS))],
        out_specs=[row_blk(pool_width),
                   pl.BlockSpec((TM // QB, QT_ROWS, lanes), lambda i: (i, 0, 0)),
                   pl.BlockSpec((TM // QB, HEAD_DIM, lanes), lambda i: (i, 0, 0)),
                   row_blk(dsa_width),
                   row_blk(QT_ROWS),
                   row_blk(HEAD_DIM),
                   pl.BlockSpec((1, VT_ROWS, ATT_CHUNK), lambda i: (i, 0, 0)),
                   pl.BlockSpec((IDX_HEADS, TM), lambda i: (0, i))],
        out_shape=[jax.ShapeDtypeStruct((T, pool_width), BF16),
                   jax.ShapeDtypeStruct((nb, QT_ROWS, lanes), BF16),
                   jax.ShapeDtypeStruct((nb, HEAD_DIM, lanes), BF16),
                   jax.ShapeDtypeStruct((T, dsa_width), BF16),
                   jax.ShapeDtypeStruct((T, QT_ROWS), BF16),
                   jax.ShapeDtypeStruct((T, HEAD_DIM), BF16),
                   jax.ShapeDtypeStruct((T // ATT_CHUNK, VT_ROWS, ATT_CHUNK), BF16),
                   jax.ShapeDtypeStruct((IDX_HEADS, T), F32)],
        scratch_shapes=[pltpu.VMEM((POOL_HALO, pool_width), F32)],
        compiler_params=_cparams(),
        name="even_proj",
    )(x, g.reshape(1, D), w, pool_w, pool_scale.reshape(1, pool_width), gain_b, extra,
      _pad_cols(k_gain.reshape(1, HEAD_DIM), QT_ROWS))


def _odd_proj_kernel(x_ref, g_ref, w_ref, qg_ref, kg_ref, qt_ref, gate_ref, k_ref, vt_ref, *, q_w, q_scale):
    xn = _rms_rows_bf16(x_ref[...], g_ref[...])
    kv_w = SWA_KV_HEADS * HEAD_DIM
    hq = jnp.dot(xn, w_ref[:, :q_w], preferred_element_type=F32)
    _store_heads_t(hq, qt_ref, qg_ref, n_heads=SWA_Q_HEADS, normalize=True, scale=q_scale)
    h = jnp.dot(xn, w_ref[:, q_w:q_w + 2 * kv_w], preferred_element_type=F32)
    for hd in range(SWA_KV_HEADS):
        k = h[:, hd * HEAD_DIM:(hd + 1) * HEAD_DIM]
        ms = jnp.mean(k * k, axis=-1, keepdims=True)
        k_ref[hd] = (k * lax.rsqrt(ms + EPS) * kg_ref[...]).astype(BF16)
    ones_rows = _ones_row_pad(TM)
    for p in range(SWA_KV_HEADS // 2):
        pair = h[:, kv_w + p * 128:kv_w + (p + 1) * 128].T
        vt_ref[2 * p] = jnp.concatenate([pair[:HEAD_DIM], ones_rows], axis=0).astype(BF16)
        vt_ref[2 * p + 1] = jnp.concatenate([pair[HEAD_DIM:], ones_rows], axis=0).astype(BF16)
    gate = jnp.dot(xn, w_ref[:, q_w + 2 * kv_w:], preferred_element_type=F32)
    gate_ref[...] = _silu(gate).astype(BF16)


def _odd_proj(x, g, w, q_gain, k_gain, *, q_width, q_scale):
    T, D = x.shape
    nb = T // QB
    lanes = SWA_Q_HEADS * QB
    gain_b = jnp.broadcast_to(q_gain.reshape(HEAD_DIM, 1), (HEAD_DIM, QB)).astype(F32)
    kern = functools.partial(_odd_proj_kernel, q_w=q_width, q_scale=q_scale)
    return pl.pallas_call(
        kern,
        grid=(T // TM,),
        in_specs=[pl.BlockSpec((TM, D), lambda i: (i, 0)),
                  _const_spec((1, D)),
                  _const_spec(w.shape),
                  _const_spec((HEAD_DIM, QB)),
                  _const_spec((1, HEAD_DIM))],
        out_specs=[pl.BlockSpec((TM // QB, HEAD_DIM, lanes), lambda i: (i, 0, 0)),
                   pl.BlockSpec((TM, q_width), lambda i: (i, 0)),
                   pl.BlockSpec((SWA_KV_HEADS, TM, HEAD_DIM), lambda i: (0, i, 0)),
                   pl.BlockSpec((SWA_KV_HEADS, VT_ROWS, TM), lambda i: (0, 0, i))],
        out_shape=[jax.ShapeDtypeStruct((nb, HEAD_DIM, lanes), BF16),
                   jax.ShapeDtypeStruct((T, q_width), BF16),
                   jax.ShapeDtypeStruct((SWA_KV_HEADS, T, HEAD_DIM), BF16),
                   jax.ShapeDtypeStruct((SWA_KV_HEADS, VT_ROWS, T), BF16)],
        compiler_params=_cparams(),
        name="odd_proj",
    )(x, g.reshape(1, D), w, gain_b, k_gain.reshape(1, HEAD_DIM))


def _out_proj_kernel(*refs):
    x_ref, w_ref, out_ref = refs[0], refs[-2], refs[-1]
    acc = x_ref[...]
    row = 0
    for a_ref in refs[1:-2]:
        width = a_ref.shape[1]
        acc = acc + jnp.dot(a_ref[...], w_ref[row:row + width, :], preferred_element_type=F32)
        row += width
    out_ref[...] = acc


def _out_proj(x, parts, w):
    T, D = x.shape
    row_blk = lambda width: pl.BlockSpec((TM_OUT, width), lambda i: (i, 0))
    return pl.pallas_call(
        _out_proj_kernel,
        grid=(T // TM_OUT,),
        in_specs=[row_blk(D)] + [row_blk(a.shape[1]) for a in parts] + [_const_spec(w.shape)],
        out_specs=row_blk(D),
        out_shape=jax.ShapeDtypeStruct((T, D), F32),
        compiler_params=_cparams(),
        name="out_proj",
    )(x, *parts, w)


def _store_gated(o_t, gate_ref, out_ref, n_heads):
    for p in range(n_heads // 2):
        pair = jnp.concatenate([o_t[:, (2 * p) * QB:(2 * p + 1) * QB],
                                o_t[:, (2 * p + 1) * QB:(2 * p + 2) * QB]], axis=0)
        g = gate_ref[:, p * 128:(p + 1) * 128].astype(F32)
        out_ref[:, p * 128:(p + 1) * 128] = (g * pair.T).astype(BF16)


def _dsa_kernel(bounded_ref, qt_ref, iqt_ref, iwt_ref, gate_ref, k_ref, ik_ref, vt_ref, bias_ref, out_ref,
                score_ref, thr_ref):
    s = pl.program_id(0)
    ns = pl.num_programs(0) - 1
    lanes = DSA_HEADS * QB
    sub = ATT_CHUNK // QB
    cur = s
    prv = jnp.maximum(s - 1, 0)
    pairs_per_chunk = IDX_CHUNK // (BLOCKS_PER_STEP * QB)
    n_cur = cur // pairs_per_chunk + 1
    n_prv = prv // pairs_per_chunk + 1
    cur_slot = cur % 2
    prv_slot = prv % 2
    both = range(BLOCKS_PER_STEP)

    @pl.when(s < ns)
    def _():
        iqt = [iqt_ref[j] for j in both]
        iwt = iwt_ref[...]
        krow = lax.broadcasted_iota(I32, (IDX_CHUNK, QB), 0)
        lane_pos = lax.broadcasted_iota(I32, (IDX_CHUNK, QB), 1)

        def idx_body(c, carry):
            off = pl.multiple_of(c * IDX_CHUNK, IDX_CHUNK)
            ik = ik_ref[pl.ds(off, IDX_CHUNK), :]
            for j in both:
                sc = jnp.dot(ik, iqt[j], preferred_element_type=F32)
                score = jnp.zeros((IDX_CHUNK, QB), F32)
                for h in range(IDX_HEADS):
                    score = score + jnp.maximum(sc[:, h * QB:(h + 1) * QB], 0.0) * iwt[h:h + 1, j * QB:(j + 1) * QB]
                causal = off + krow <= (cur * BLOCKS_PER_STEP + j) * QB + lane_pos
                score_ref[cur_slot, j, pl.ds(off, IDX_CHUNK), :] = jnp.where(causal, score, -jnp.inf)
            return carry

        lax.fori_loop(0, n_cur, idx_body, 0)

    def key_to_float(key):
        return lax.bitcast_convert_type(jnp.where(key < 0, key ^ 0x7FFFFFFF, key), F32)

    def search_unit(state):
        bases, bit, accs, ch = state
        off = pl.multiple_of(ch * IDX_CHUNK, IDX_CHUNK)
        last = ch + 1 == n_cur
        new_bases, new_accs = [], []
        for j in both:
            cand = key_to_float(bases[j] + bit)
            hit = jnp.where(score_ref[cur_slot, j, pl.ds(off, IDX_CHUNK), :] >= cand, 1, 0).astype(I32)
            acc = accs[j] + jnp.sum(hit.reshape(IDX_CHUNK // 8, 8, QB), axis=0)
            accept = jnp.sum(acc, axis=0, keepdims=True) >= DSA_TOPK
            new_bases.append(jnp.where(last & accept, bases[j] + bit, bases[j]))
            new_accs.append(jnp.where(last, 0, acc))
        bit = jnp.where(last, lax.shift_right_logical(bit, 1), bit)
        ch = jnp.where(last, 0, ch + 1)
        return tuple(new_bases), bit, tuple(new_accs), ch

    search_init = (tuple(jnp.full((1, QB), INT_MIN, I32) for _ in both), jnp.int32(INT_MIN),
                   tuple(jnp.zeros((8, QB), I32) for _ in both), jnp.int32(0))

    def store_thresholds(state):
        for j in both:
            thr = key_to_float(jnp.maximum(state[0][j], FINITE_MIN_KEY))
            thr_ref[cur_slot, j] = jnp.broadcast_to(thr, (8, QB))

    qts = [qt_ref[j] for j in both]
    thr_prv = [thr_ref[prv_slot, j][0:1] for j in both]

    def tiles(j, c, near, thr, fn):
        off = pl.multiple_of(c * ATT_CHUNK, ATT_CHUNK)
        sc = jnp.dot(k_ref[pl.ds(off, ATT_CHUNK), :], qts[j], preferred_element_type=F32)
        sel = score_ref[prv_slot, j, pl.ds(off, ATT_CHUNK), :] >= thr
        blocks = []
        for sb in range(sub):
            delta = jnp.clip(prv * BLOCKS_PER_STEP + j - (c * sub + sb), 0, FAR_DELTA)
            sel_sb = sel[sb * QB:(sb + 1) * QB]
            row = []
            for h in range(DSA_HEADS):
                t = sc[sb * QB:(sb + 1) * QB, h * QB:(h + 1) * QB]
                if near:
                    t = t + bias_ref[delta, h]
                row.append(fn(t, sel_sb))
            blocks.append(jnp.concatenate(row, axis=1))
        return jnp.concatenate(blocks, axis=0)

    def bounded_pair(j, near, pair, thr):
        pvs = []
        for half in range(ATT_PAIR):
            c = pair * ATT_PAIR + half
            p = tiles(j, c, near, thr, lambda t, sel: jnp.where(sel, jnp.exp2(t), 0.0))
            pvs.append(jnp.dot(vt_ref[c], p.astype(BF16), preferred_element_type=F32))
        return sum(pvs[1:], pvs[0])

    n_far = jnp.maximum(prv * BLOCKS_PER_STEP - (FAR_DELTA - 1), 0) // (sub * ATT_PAIR)
    acc0 = tuple(jnp.zeros((VT_ROWS, lanes), F32) for _ in both)

    def store_block(j, o_t):
        rows = slice(j * QB, (j + 1) * QB)
        for p in range(DSA_HEADS // 2):
            pair = jnp.concatenate([o_t[:, (2 * p) * QB:(2 * p + 1) * QB],
                                    o_t[:, (2 * p + 1) * QB:(2 * p + 2) * QB]], axis=0)
            g = gate_ref[rows, p * 128:(p + 1) * 128].astype(F32)
            out_ref[rows, p * 128:(p + 1) * 128] = (g * pair.T).astype(BF16)

    def finish_bounded(accs):
        for j in both:
            store_block(j, accs[j][:HEAD_DIM] / accs[j][HEAD_DIM:HEAD_DIM + 1])

    def attend(near, c, accs, thrs):
        return tuple(accs[j] + bounded_pair(j, near, c, thrs[j]) for j in both)

    def fused_body(near, c, carry):
        accs, state = carry
        thrs = [jnp.where(c < n_prv, thr_prv[j], jnp.inf) for j in both]
        accs = attend(near, jnp.minimum(c, n_prv - 1), accs, thrs)
        for _ in range(SEARCH_PASSES):
            state = search_unit(state)
        return accs, state

    def fused_path():
        carry = (acc0, search_init)
        carry = lax.fori_loop(0, n_far, functools.partial(fused_body, False), carry)
        accs, state = lax.fori_loop(n_far, n_cur, functools.partial(fused_body, True), carry)
        store_thresholds(state)
        finish_bounded(accs)
        return 0

    def general_body(j, c, carry):
        m, l, acc = carry
        t = tiles(j, c, True, thr_prv[j], lambda t, sel: jnp.where(sel, t, MASKED))
        m_new = jnp.maximum(m, jnp.max(t, axis=0, keepdims=True))
        alpha = jnp.exp2(m - m_new)
        p = jnp.exp2(t - m_new)
        l_new = alpha * l + jnp.sum(p, axis=0, keepdims=True)
        pv = jnp.dot(vt_ref[c], p.astype(BF16), preferred_element_type=F32)
        return m_new, l_new, acc * alpha + pv

    def separate_path():
        @pl.when(s < ns)
        def _():
            store_thresholds(lax.fori_loop(0, SEARCH_PASSES * n_cur, lambda _, st: search_unit(st), search_init))

        @pl.when((s >= 1) & (bounded_ref[0] > 0))
        def _():
            accs = lax.fori_loop(0, n_far, lambda c, a: attend(False, c, a, thr_prv), acc0)
            finish_bounded(lax.fori_loop(n_far, n_prv, lambda c, a: attend(True, c, a, thr_prv), accs))

        @pl.when((s >= 1) & (bounded_ref[0] <= 0))
        def _():
            for j in both:
                m0 = jnp.full((1, lanes), MASKED, F32)
                l0 = jnp.zeros((1, lanes), F32)
                _, l, acc = lax.fori_loop(0, n_prv * ATT_PAIR, functools.partial(general_body, j), (m0, l0, acc0[j]))
                store_block(j, acc[:HEAD_DIM] / l)

        return 0

    lax.cond((bounded_ref[0] > 0) & (s >= 1) & (s < ns), fused_path, separate_path)


def _dsa_attention(bounded, qt, iqt, iwt, gate, k, ik, vt, bias):
    ns = qt.shape[0] // BLOCKS_PER_STEP
    T = k.shape[0]
    width = DSA_HEADS * HEAD_DIM
    rows = BLOCKS_PER_STEP * QB

    def prv(s):
        return jnp.maximum(s - 1, 0)

    def cur(s):
        return jnp.minimum(s, ns - 1)

    return pl.pallas_call(
        _dsa_kernel,
        grid=(ns + 1,),
        in_specs=[pl.BlockSpec(memory_space=pltpu.SMEM),
                  pl.BlockSpec((BLOCKS_PER_STEP, QT_ROWS, DSA_HEADS * QB), lambda s: (prv(s), 0, 0)),
                  pl.BlockSpec((BLOCKS_PER_STEP, HEAD_DIM, IDX_HEADS * QB), lambda s: (cur(s), 0, 0)),
                  pl.BlockSpec((IDX_HEADS, rows), lambda s: (0, cur(s))),
                  pl.BlockSpec((rows, width), lambda s: (prv(s), 0)),
                  _const_spec(k.shape),
                  _const_spec(ik.shape),
                  _const_spec(vt.shape),
                  _const_spec(bias.shape)],
        out_specs=pl.BlockSpec((rows, width), lambda s: (prv(s), 0)),
        out_shape=jax.ShapeDtypeStruct((T, width), BF16),
        scratch_shapes=[pltpu.VMEM((2, BLOCKS_PER_STEP, T, QB), F32), pltpu.VMEM((2, BLOCKS_PER_STEP, 8, QB), F32)],
        compiler_params=_cparams(),
        name="dsa_attention",
    )(bounded, qt, iqt, iwt, gate, k, ik, vt, bias)


def _swa_kernel(bounded_ref, qt_ref, kp_ref, kc_ref, vp_ref, vc_ref, sink_ref, gate_ref, bias_ref, out_ref):
    i = pl.program_id(0)
    grp = SWA_Q_HEADS // SWA_KV_HEADS
    shape = (2 * QB, QB)
    c_idx = lax.broadcasted_iota(I32, shape, 0)
    dist = QB + lax.broadcasted_iota(I32, shape, 1) - c_idx
    mask = (dist >= 0) & (dist < QB) & ((c_idx >= QB) | (i > 0))

    def group(kv, bounded):
        kcat = jnp.concatenate([kp_ref[kv], kc_ref[kv]], axis=0)
        vcat = jnp.concatenate([vp_ref[kv], vc_ref[kv]], axis=1)
        qg = qt_ref[0, :, kv * grp * QB:(kv + 1) * grp * QB]
        s = jnp.dot(kcat, qg, preferred_element_type=F32)
        sink = sink_ref[:, kv * grp * QB:(kv + 1) * grp * QB]
        logits = [s[:, j * QB:(j + 1) * QB] + bias_ref[kv * grp + j] for j in range(grp)]
        if bounded:
            p = jnp.concatenate([jnp.where(mask, jnp.exp2(t), 0.0) for t in logits], axis=1)
            pv = jnp.dot(vcat, p.astype(BF16), preferred_element_type=F32)
            return pv[:HEAD_DIM] / (pv[HEAD_DIM:HEAD_DIM + 1] + jnp.exp2(sink))
        t = jnp.concatenate([jnp.where(mask, t, MASKED) for t in logits], axis=1)
        m = jnp.maximum(jnp.max(t, axis=0, keepdims=True), sink)
        p = jnp.exp2(t - m)
        l = jnp.sum(p, axis=0, keepdims=True) + jnp.exp2(sink - m)
        pv = jnp.dot(vcat, p.astype(BF16), preferred_element_type=F32)
        return pv[:HEAD_DIM] / l

    def path(bounded):
        return jnp.concatenate([group(kv, bounded) for kv in range(SWA_KV_HEADS)], axis=1)

    o_t = lax.cond(bounded_ref[0] > 0, functools.partial(path, True), functools.partial(path, False))
    _store_gated(o_t, gate_ref, out_ref, SWA_Q_HEADS)


def _swa_attention(bounded, qt, k_hm, vt_hm, sink_b, gate, bias):
    nb = qt.shape[0]
    T = k_hm.shape[1]
    width = SWA_Q_HEADS * HEAD_DIM

    def prev(i):
        return jnp.maximum(i - 1, 0)

    return pl.pallas_call(
        _swa_kernel,
        grid=(nb,),
        in_specs=[pl.BlockSpec(memory_space=pltpu.SMEM),
                  pl.BlockSpec((1, HEAD_DIM, SWA_Q_HEADS * QB), lambda i: (i, 0, 0)),
                  pl.BlockSpec((SWA_KV_HEADS, QB, HEAD_DIM), lambda i: (0, prev(i), 0)),
                  pl.BlockSpec((SWA_KV_HEADS, QB, HEAD_DIM), lambda i: (0, i, 0)),
                  pl.BlockSpec((SWA_KV_HEADS, VT_ROWS, QB), lambda i: (0, 0, prev(i))),
                  pl.BlockSpec((SWA_KV_HEADS, VT_ROWS, QB), lambda i: (0, 0, i)),
                  _const_spec((1, SWA_Q_HEADS * QB)),
                  pl.BlockSpec((QB, width), lambda i: (i, 0)),
                  _const_spec(bias.shape)],
        out_specs=pl.BlockSpec((QB, width), lambda i: (i, 0)),
        out_shape=jax.ShapeDtypeStruct((T, width), BF16),
        compiler_params=_cparams(),
        name="swa_attention",
    )(bounded, qt, k_hm, k_hm, vt_hm, vt_hm, sink_b, gate, bias)


def _pad_cols(w, width):
    return jnp.pad(w, ((0, 0), (0, width - w.shape[1])))


def _even_layer(x, norm_g, w_in, pool_w, pool_scale, q_gain, k_gain, w_out, rel_bias, dsa_bias):
    pool_width = pool_w.shape[0] * pool_w.shape[1]
    dsa_width = DSA_HEADS * HEAD_DIM
    w_all = _pad_cols(w_in, -(-w_in.shape[1] // 128) * 128).astype(BF16)

    q_scale = HEAD_DIM ** -0.5 * LOG2E
    far = rel_bias[NUM_BUCKETS - 1, :DSA_HEADS].astype(F32) * LOG2E
    far_hi = far.astype(BF16)
    far_lo = (far - far_hi.astype(F32)).astype(BF16)
    extra = jnp.zeros((QT_ROWS - HEAD_DIM, DSA_HEADS * QB), BF16)
    extra = extra.at[0].set(jnp.repeat(far_hi, QB)).at[1].set(jnp.repeat(far_lo, QB))
    q_norm = jnp.sqrt(HEAD_DIM * jnp.max(q_gain.astype(F32) ** 2) * q_scale ** 2 + jnp.max(far ** 2))
    k_norm = jnp.sqrt(HEAD_DIM * jnp.max(k_gain.astype(F32) ** 2) + 2.0)
    near_max = jnp.max(jnp.abs(rel_bias[:, :DSA_HEADS] - rel_bias[NUM_BUCKETS - 1, :DSA_HEADS])) * LOG2E
    bound = 1.02 * q_norm * k_norm + near_max
    bounded = (bound < SAFE_LOG2_RANGE).astype(I32).reshape(1)

    py, qt, iqt, gate, k, ik, vt, iwt = _even_proj(
        x, norm_g, w_all, pool_w.astype(BF16), pool_scale, q_gain, extra, k_gain, pool_width=pool_width,
        dsa_width=dsa_width, q_scale=q_scale, iw_scale=IDX_HEADS ** -0.5 * HEAD_DIM ** -0.5)
    dy = _dsa_attention(bounded, qt, iqt, iwt, gate, k, ik, vt, dsa_bias)
    return _out_proj(x, [py, dy], w_out.astype(BF16))


def _odd_layer(x, norm_g, w_in, q_gain, k_gain, sinks, w_out, rel_bias, swa_bias):
    q_width = SWA_Q_HEADS * HEAD_DIM
    w_all = w_in.astype(BF16)
    q_scale = HEAD_DIM ** -0.5 * LOG2E
    sink_b = jnp.repeat(sinks.astype(F32) * LOG2E, QB).reshape(1, SWA_Q_HEADS * QB)
    q_norm = jnp.sqrt(HEAD_DIM * jnp.max(q_gain.astype(F32) ** 2)) * q_scale
    k_norm = jnp.sqrt(HEAD_DIM * jnp.max(k_gain.astype(F32) ** 2))
    bound = jnp.maximum(1.02 * q_norm * k_norm + jnp.max(jnp.abs(rel_bias)) * LOG2E, jnp.max(jnp.abs(sink_b)))
    bounded = (bound < SAFE_LOG2_RANGE).astype(I32).reshape(1)

    qt, gate, k_hm, vt_hm = _odd_proj(x, norm_g, w_all, q_gain, k_gain, q_width=q_width, q_scale=q_scale)
    go = _swa_attention(bounded, qt, k_hm, vt_hm, sink_b, gate, swa_bias)
    return _out_proj(x, [go], w_out.astype(BF16))


def kernel(x, rel_bias, even_norm, even_w_in, even_pool_w, even_pool_scale, even_q_gain, even_k_gain,
           even_w_out, odd_norm, odd_w_in, odd_q_gain, odd_k_gain, odd_sinks, odd_w_out):
    B, T, D = x.shape
    depth = even_norm.shape[0] + odd_norm.shape[0]
    dsa_bias = _bias_tiles(rel_bias, N_BIAS_TILES, DSA_HEADS, QB, QB, 0, minus_far=True)
    swa_bias = _bias_tiles(rel_bias, 1, SWA_Q_HEADS, 2 * QB, 0, QB)[0]
    outs = []
    for b in range(B):
        h = x.reshape(T, D) if B == 1 else x[b]
        for layer in range(depth):
            j = layer // 2
            if layer % 2 == 0:
                h = _even_layer(h, even_norm[j], even_w_in[j], even_pool_w[j], even_pool_scale[j],
                                even_q_gain[j], even_k_gain[j], even_w_out[j], rel_bias, dsa_bias)
            else:
                h = _odd_layer(h, odd_norm[j], odd_w_in[j], odd_q_gain[j], odd_k_gain[j],
                               odd_sinks[j], odd_w_out[j], rel_bias, swa_bias)
        outs.append(h)
    return outs[0].reshape(B, T, D) if B == 1 else jnp.stack(outs, axis=0)
```

```python
import functools

import jax
import jax.numpy as jnp
from jax import lax
from jax.experimental import pallas as pl
from jax.experimental.pallas import tpu as pltpu

F32 = jnp.float32
BF16 = jnp.bfloat16
I32 = jnp.int32

EPS = 1e-6
HEAD_DIM = 64
QB = 128
POOL_WINDOWS = (2, 4, 8, 16)
POOL_HALO = 16
DSA_HEADS = 16
IDX_HEADS = 16
DSA_TOPK = 256
SWA_Q_HEADS = 32
SWA_KV_HEADS = 4
NUM_BUCKETS = 32
IDX_CHUNK = 512
ATT_CHUNK = 256
ATT_PAIR = IDX_CHUNK // ATT_CHUNK
INT_MIN = -(2 ** 31)
FINITE_MIN_KEY = -0x7F800000
SEARCH_PASSES = 32
BLOCKS_PER_STEP = 2
MASKED = -1e30
LOG2E = 1.4426950408889634
QT_ROWS = 128
VT_ROWS = 80
SAFE_LOG2_RANGE = 60.0
TM = 256
TM_OUT = 512
VMEM_LIMIT = 56 * 1024 * 1024


def _bucket_starts():
    max_exact = NUM_BUCKETS // 2
    starts = list(range(max_exact + 1))
    n = max_exact
    for b in range(max_exact + 1, NUM_BUCKETS):
        while n ** 16 < max_exact ** 16 * 64 ** (b - max_exact):
            n += 1
        starts.append(n)
    return tuple(starts)


BUCKET_START = _bucket_starts()
FAR_DELTA = -(-(BUCKET_START[-1] + QB - 1) // QB)
N_BIAS_TILES = FAR_DELTA + 1


def _cparams(n_grid=1):
    return pltpu.CompilerParams(dimension_semantics=("arbitrary",) * n_grid,
                                vmem_limit_bytes=VMEM_LIMIT)


def _const_spec(shape):
    return pl.BlockSpec(shape, lambda i: (0,) * len(shape), pipeline_mode=pl.Buffered(1))


def _rms_rows_bf16(x, g):
    ms = jnp.mean(x * x, axis=-1, keepdims=True)
    return (x * lax.rsqrt(ms + EPS) * g).astype(BF16)


def _silu(x):
    return x / (1.0 + jnp.exp(-x))


def _bias_tiles_kernel(rb_ref, out_ref, *, n_heads, rows, base_step, base0, minus_far):
    base = base0 + pl.program_id(0) * base_step
    span = rows + QB
    dist = base - rows + lax.broadcasted_iota(I32, (8, span), 1)
    at_least = [dist >= BUCKET_START[b] for b in range(1, NUM_BUCKETS)]
    for h in range(n_heads):
        val = jnp.full((8, span), rb_ref[0, h], F32)
        for b in range(1, NUM_BUCKETS):
            val = jnp.where(at_least[b - 1], rb_ref[b, h], val)
        if minus_far:
            val = val - rb_ref[NUM_BUCKETS - 1, h]
        wide = jnp.broadcast_to(val[0:1] * LOG2E, (rows, span))
        out_ref[0, h] = pltpu.roll(wide, 0, 1, stride=1, stride_axis=0)[:, rows:]


def _bias_tiles(rel_bias, n_tiles, n_heads, rows, base_step, base0, minus_far=False):
    kern = functools.partial(_bias_tiles_kernel, n_heads=n_heads, rows=rows, base_step=base_step, base0=base0,
                             minus_far=minus_far)
    return pl.pallas_call(
        kern,
        grid=(n_tiles,),
        in_specs=[pl.BlockSpec(memory_space=pltpu.SMEM)],
        out_specs=pl.BlockSpec((1, n_heads, rows, QB), lambda t: (t, 0, 0, 0)),
        out_shape=jax.ShapeDtypeStruct((n_tiles, n_heads, rows, QB), F32),
        compiler_params=_cparams(),
        name="bias_tiles",
    )(rel_bias)


def _store_heads_t(h, out_ref, gain_ref, *, n_heads, normalize, scale):
    for b in range(TM // QB):
        rows = h[b * QB:(b + 1) * QB]
        for p in range(n_heads // 2):
            pair = rows[:, p * 128:(p + 1) * 128].T
            for hh in range(2):
                t = pair[hh * HEAD_DIM:(hh + 1) * HEAD_DIM]
                if normalize:
                    ms = jnp.mean(t * t, axis=0, keepdims=True)
                    t = t * lax.rsqrt(ms + EPS) * gain_ref[...] * scale
                head = 2 * p + hh
                out_ref[b, :HEAD_DIM, head * QB:(head + 1) * QB] = t.astype(BF16)


def _pool_mix(i, pin, gate, pw_ref, ps_ref, out_ref, halo_ref):
    width = pin.shape[1]
    gc = width // len(POOL_WINDOWS)

    @pl.when(i == 0)
    def _():
        halo_ref[...] = jnp.zeros_like(halo_ref)

    ext = jnp.concatenate([halo_ref[...], pin], axis=0)
    halo_ref[...] = pin[TM - POOL_HALO:]
    pos = i * TM + lax.broadcasted_iota(I32, (TM, gc), 0)
    outs = []
    for grp, win in enumerate(POOL_WINDOWS):
        e = ext[:, grp * gc:(grp + 1) * gc]
        s = e
        span = 1
        while span < win:
            s = s + pltpu.roll(s, span, 0)
            span *= 2
        s = s[POOL_HALO:]
        a = pin[:, grp * gc:(grp + 1) * gc]
        cnt = jnp.minimum(pos + 1, win).astype(F32)
        pooled = s / cnt - a
        y = jnp.dot(pooled.astype(BF16), pw_ref[grp], preferred_element_type=F32)
        outs.append(y * ps_ref[:, grp * gc:(grp + 1) * gc])
    py = jnp.concatenate(outs, axis=1)
    out_ref[...] = (_silu(gate) * py).astype(BF16)


def _ones_row_pad(cols):
    row = lax.broadcasted_iota(I32, (VT_ROWS - HEAD_DIM, cols), 0)
    return jnp.where(row == 0, 1.0, 0.0)


def _even_proj_kernel(x_ref, g_ref, w_ref, wt_ref, pw_ref, ps_ref, qg_ref, extra_ref, kg_ref,
                      py_ref, qt_ref, iqt_ref, gate_ref, k_ref, ik_ref, vt_ref, iwt_ref, halo_ref,
                      *, pool_w, dsa_w, q_scale, iw_scale):
    i = pl.program_id(0)
    xn = _rms_rows_bf16(x_ref[...], g_ref[...])
    col = 0

    def proj(width):
        nonlocal col
        h = jnp.dot(xn, w_ref[:, col:col + width], preferred_element_type=F32)
        col += width
        return h

    h = proj(2 * pool_w)
    _pool_mix(i, h[:, :pool_w], h[:, pool_w:], pw_ref, ps_ref, py_ref, halo_ref)

    for b in range(TM // QB):
        qt_ref[b, HEAD_DIM:, :] = extra_ref[...]
    _store_heads_t(proj(dsa_w), qt_ref, qg_ref, n_heads=DSA_HEADS, normalize=True, scale=q_scale)

    kv = proj(2 * HEAD_DIM)
    lane = lax.broadcasted_iota(I32, kv.shape, 1)
    is_k = lane < HEAD_DIM
    ms = jnp.sum(jnp.where(is_k, kv * kv, 0.0), axis=-1, keepdims=True) * (1.0 / HEAD_DIM)
    kn = kv * lax.rsqrt(ms + EPS) * kg_ref[...]
    k_ref[...] = jnp.where(is_k, kn, jnp.where(lane < HEAD_DIM + 2, 1.0, 0.0)).astype(BF16)
    vt_ref[0] = jnp.concatenate([kv.T[HEAD_DIM:], _ones_row_pad(TM)], axis=0).astype(BF16)

    gate_ref[...] = _silu(proj(dsa_w)).astype(BF16)
    _store_heads_t(proj(IDX_HEADS * HEAD_DIM), iqt_ref, qg_ref, n_heads=IDX_HEADS, normalize=False, scale=1.0)

    iw = jnp.dot(xn, wt_ref[...], preferred_element_type=F32)
    ik_ref[...] = iw[:, :HEAD_DIM].astype(BF16)
    iwt_ref[...] = iw.T[HEAD_DIM:HEAD_DIM + IDX_HEADS] * iw_scale


def _even_proj(x, g, w, w_tail, pool_w, pool_scale, q_gain, extra, k_gain, *, pool_width, dsa_width, q_scale,
               iw_scale):
    T, D = x.shape
    assert TM == ATT_CHUNK
    nb = T // QB
    lanes = DSA_HEADS * QB
    gain_b = jnp.broadcast_to(q_gain.reshape(HEAD_DIM, 1), (HEAD_DIM, QB)).astype(F32)
    kern = functools.partial(_even_proj_kernel, pool_w=pool_width, dsa_w=dsa_width, q_scale=q_scale,
                             iw_scale=iw_scale)
    row_blk = lambda width: pl.BlockSpec((TM, width), lambda i: (i, 0))
    return pl.pallas_call(
        kern,
        grid=(T // TM,),
        in_specs=[row_blk(D),
                  _const_spec((1, D)),
                  _const_spec(w.shape),
                  _const_spec(w_tail.shape),
                  _const_spec(pool_w.shape),
                  _const_spec((1, pool_width)),
                  _const_spec((HEAD_DIM, QB)),
                  _const_spec(extra.shape),
                  _const_spec((1, QT_ROWS))],
        out_specs=[row_blk(pool_width),
                   pl.BlockSpec((TM // QB, QT_ROWS, lanes), lambda i: (i, 0, 0)),
                   pl.BlockSpec((TM // QB, HEAD_DIM, lanes), lambda i: (i, 0, 0)),
                   row_blk(dsa_width),
                   row_blk(QT_ROWS),
                   row_blk(HEAD_DIM),
                   pl.BlockSpec((1, VT_ROWS, ATT_CHUNK), lambda i: (i, 0, 0)),
                   pl.BlockSpec((IDX_HEADS, TM), lambda i: (0, i))],
        out_shape=[jax.ShapeDtypeStruct((T, pool_width), BF16),
                   jax.ShapeDtypeStruct((nb, QT_ROWS, lanes), BF16),
                   jax.ShapeDtypeStruct((nb, HEAD_DIM, lanes), BF16),
                   jax.ShapeDtypeStruct((T, dsa_width), BF16),
                   jax.ShapeDtypeStruct((T, QT_ROWS), BF16),
                   jax.ShapeDtypeStruct((T, HEAD_DIM), BF16),
                   jax.ShapeDtypeStruct((T // ATT_CHUNK, VT_ROWS, ATT_CHUNK), BF16),
                   jax.ShapeDtypeStruct((IDX_HEADS, T), F32)],
        scratch_shapes=[pltpu.VMEM((POOL_HALO, pool_width), F32)],
        compiler_params=_cparams(),
        name="even_proj",
    )(x, g.reshape(1, D), w, w_tail, pool_w, pool_scale.reshape(1, pool_width), gain_b, extra,
      _pad_cols(k_gain.reshape(1, HEAD_DIM), QT_ROWS))


def _odd_proj_kernel(x_ref, g_ref, w_ref, qg_ref, kg_ref, qt_ref, gate_ref, k_ref, vt_ref, *, q_w, q_scale):
    xn = _rms_rows_bf16(x_ref[...], g_ref[...])
    kv_w = SWA_KV_HEADS * HEAD_DIM
    hq = jnp.dot(xn, w_ref[:, :q_w], preferred_element_type=F32)
    _store_heads_t(hq, qt_ref, qg_ref, n_heads=SWA_Q_HEADS, normalize=True, scale=q_scale)
    h = jnp.dot(xn, w_ref[:, q_w:q_w + 2 * kv_w], preferred_element_type=F32)
    for hd in range(SWA_KV_HEADS):
        k = h[:, hd * HEAD_DIM:(hd + 1) * HEAD_DIM]
        ms = jnp.mean(k * k, axis=-1, keepdims=True)
        k_ref[hd] = (k * lax.rsqrt(ms + EPS) * kg_ref[...]).astype(BF16)
    ones_rows = _ones_row_pad(TM)
    for p in range(SWA_KV_HEADS // 2):
        pair = h[:, kv_w + p * 128:kv_w + (p + 1) * 128].T
        vt_ref[2 * p] = jnp.concatenate([pair[:HEAD_DIM], ones_rows], axis=0).astype(BF16)
        vt_ref[2 * p + 1] = jnp.concatenate([pair[HEAD_DIM:], ones_rows], axis=0).astype(BF16)
    gate = jnp.dot(xn, w_ref[:, q_w + 2 * kv_w:], preferred_element_type=F32)
    gate_ref[...] = _silu(gate).astype(BF16)


def _odd_proj(x, g, w, q_gain, k_gain, *, q_width, q_scale):
    T, D = x.shape
    nb = T // QB
    lanes = SWA_Q_HEADS * QB
    gain_b = jnp.broadcast_to(q_gain.reshape(HEAD_DIM, 1), (HEAD_DIM, QB)).astype(F32)
    kern = functools.partial(_odd_proj_kernel, q_w=q_width, q_scale=q_scale)
    return pl.pallas_call(
        kern,
        grid=(T // TM,),
        in_specs=[pl.BlockSpec((TM, D), lambda i: (i, 0)),
                  _const_spec((1, D)),
                  _const_spec(w.shape),
                  _const_spec((HEAD_DIM, QB)),
                  _const_spec((1, HEAD_DIM))],
        out_specs=[pl.BlockSpec((TM // QB, HEAD_DIM, lanes), lambda i: (i, 0, 0)),
                   pl.BlockSpec((TM, q_width), lambda i: (i, 0)),
                   pl.BlockSpec((SWA_KV_HEADS, TM, HEAD_DIM), lambda i: (0, i, 0)),
                   pl.BlockSpec((SWA_KV_HEADS, VT_ROWS, TM), lambda i: (0, 0, i))],
        out_shape=[jax.ShapeDtypeStruct((nb, HEAD_DIM, lanes), BF16),
                   jax.ShapeDtypeStruct((T, q_width), BF16),
                   jax.ShapeDtypeStruct((SWA_KV_HEADS, T, HEAD_DIM), BF16),
                   jax.ShapeDtypeStruct((SWA_KV_HEADS, VT_ROWS, T), BF16)],
        compiler_params=_cparams(),
        name="odd_proj",
    )(x, g.reshape(1, D), w, gain_b, k_gain.reshape(1, HEAD_DIM))


def _out_proj_kernel(*refs):
    x_ref, w_ref, out_ref = refs[0], refs[-2], refs[-1]
    acc = x_ref[...]
    row = 0
    for a_ref in refs[1:-2]:
        width = a_ref.shape[1]
        acc = acc + jnp.dot(a_ref[...], w_ref[row:row + width, :].astype(BF16), preferred_element_type=F32)
        row += width
    out_ref[...] = acc


def _out_proj(x, parts, w):
    T, D = x.shape
    row_blk = lambda width: pl.BlockSpec((TM_OUT, width), lambda i: (i, 0))
    return pl.pallas_call(
        _out_proj_kernel,
        grid=(T // TM_OUT,),
        in_specs=[row_blk(D)] + [row_blk(a.shape[1]) for a in parts] + [_const_spec(w.shape)],
        out_specs=row_blk(D),
        out_shape=jax.ShapeDtypeStruct((T, D), F32),
        compiler_params=_cparams(),
        name="out_proj",
    )(x, *parts, w)


def _store_gated(o_t, gate_ref, out_ref, n_heads):
    for p in range(n_heads // 2):
        pair = jnp.concatenate([o_t[:, (2 * p) * QB:(2 * p + 1) * QB],
                                o_t[:, (2 * p + 1) * QB:(2 * p + 2) * QB]], axis=0)
        g = gate_ref[:, p * 128:(p + 1) * 128].astype(F32)
        out_ref[:, p * 128:(p + 1) * 128] = (g * pair.T).astype(BF16)


def _dsa_kernel(bounded_ref, qt_ref, iqt_ref, iwt_ref, gate_ref, k_ref, ik_ref, vt_ref, bias_ref, out_ref,
                score_ref, thr_ref):
    s = pl.program_id(0)
    ns = pl.num_programs(0) - 1
    lanes = DSA_HEADS * QB
    sub = ATT_CHUNK // QB
    cur = s
    prv = jnp.maximum(s - 1, 0)
    pairs_per_chunk = IDX_CHUNK // (BLOCKS_PER_STEP * QB)
    n_cur = cur // pairs_per_chunk + 1
    n_prv = prv // pairs_per_chunk + 1
    cur_slot = cur % 2
    prv_slot = prv % 2
    both = range(BLOCKS_PER_STEP)

    @pl.when(s < ns)
    def _():
        iqt = [iqt_ref[j] for j in both]
        iwt = iwt_ref[...]
        krow = lax.broadcasted_iota(I32, (IDX_CHUNK, QB), 0)
        lane_pos = lax.broadcasted_iota(I32, (IDX_CHUNK, QB), 1)

        def idx_body(c, carry):
            off = pl.multiple_of(c * IDX_CHUNK, IDX_CHUNK)
            ik = ik_ref[pl.ds(off, IDX_CHUNK), :]
            for j in both:
                sc = jnp.dot(ik, iqt[j], preferred_element_type=F32)
                score = jnp.zeros((IDX_CHUNK, QB), F32)
                for h in range(IDX_HEADS):
                    score = score + jnp.maximum(sc[:, h * QB:(h + 1) * QB], 0.0) * iwt[h:h + 1, j * QB:(j + 1) * QB]
                causal = off + krow <= (cur * BLOCKS_PER_STEP + j) * QB + lane_pos
                score_ref[cur_slot, j, pl.ds(off, IDX_CHUNK), :] = jnp.where(causal, score, -jnp.inf)
            return carry

        lax.fori_loop(0, n_cur, idx_body, 0)

    def key_to_float(key):
        return lax.bitcast_convert_type(jnp.where(key < 0, key ^ 0x7FFFFFFF, key), F32)

    def search_unit(state):
        bases, bit, accs, ch = state
        off = pl.multiple_of(ch * IDX_CHUNK, IDX_CHUNK)
        last = ch + 1 == n_cur
        new_bases, new_accs = [], []
        for j in both:
            cand = key_to_float(bases[j] + bit)
            hit = jnp.where(score_ref[cur_slot, j, pl.ds(off, IDX_CHUNK), :] >= cand, 1, 0).astype(I32)
            acc = accs[j] + jnp.sum(hit.reshape(IDX_CHUNK // 8, 8, QB), axis=0)
            accept = jnp.sum(acc, axis=0, keepdims=True) >= DSA_TOPK
            new_bases.append(jnp.where(last & accept, bases[j] + bit, bases[j]))
            new_accs.append(jnp.where(last, 0, acc))
        bit = jnp.where(last, lax.shift_right_logical(bit, 1), bit)
        ch = jnp.where(last, 0, ch + 1)
        return tuple(new_bases), bit, tuple(new_accs), ch

    search_init = (tuple(jnp.full((1, QB), INT_MIN, I32) for _ in both), jnp.int32(INT_MIN),
                   tuple(jnp.zeros((8, QB), I32) for _ in both), jnp.int32(0))

    def store_thresholds(state):
        for j in both:
            thr = key_to_float(jnp.maximum(state[0][j], FINITE_MIN_KEY))
            thr_ref[cur_slot, j] = jnp.broadcast_to(thr, (8, QB))

    qts = [qt_ref[j] for j in both]
    thr_prv = [thr_ref[prv_slot, j][0:1] for j in both]

    def tiles(j, c, near, thr, fn):
        off = pl.multiple_of(c * ATT_CHUNK, ATT_CHUNK)
        sc = jnp.dot(k_ref[pl.ds(off, ATT_CHUNK), :], qts[j], preferred_element_type=F32)
        sel = score_ref[prv_slot, j, pl.ds(off, ATT_CHUNK), :] >= thr
        blocks = []
        for sb in range(sub):
            delta = jnp.clip(prv * BLOCKS_PER_STEP + j - (c * sub + sb), 0, FAR_DELTA)
            sel_sb = sel[sb * QB:(sb + 1) * QB]
            row = []
            for h in range(DSA_HEADS):
                t = sc[sb * QB:(sb + 1) * QB, h * QB:(h + 1) * QB]
                if near:
                    t = t + bias_ref[delta, h]
                row.append(fn(t, sel_sb))
            blocks.append(jnp.concatenate(row, axis=1))
        return jnp.concatenate(blocks, axis=0)

    def bounded_pair(j, near, pair, thr):
        pvs = []
        for half in range(ATT_PAIR):
            c = pair * ATT_PAIR + half
            p = tiles(j, c, near, thr, lambda t, sel: jnp.where(sel, jnp.exp2(t), 0.0))
            pvs.append(jnp.dot(vt_ref[c], p.astype(BF16), preferred_element_type=F32))
        return sum(pvs[1:], pvs[0])

    n_far = jnp.maximum(prv * BLOCKS_PER_STEP - (FAR_DELTA - 1), 0) // (sub * ATT_PAIR)
    acc0 = tuple(jnp.zeros((VT_ROWS, lanes), F32) for _ in both)

    def store_block(j, o_t):
        rows = slice(j * QB, (j + 1) * QB)
        for p in range(DSA_HEADS // 2):
            pair = jnp.concatenate([o_t[:, (2 * p) * QB:(2 * p + 1) * QB],
                                    o_t[:, (2 * p + 1) * QB:(2 * p + 2) * QB]], axis=0)
            g = gate_ref[rows, p * 128:(p + 1) * 128].astype(F32)
            out_ref[rows, p * 128:(p + 1) * 128] = (g * pair.T).astype(BF16)

    def finish_bounded(accs):
        for j in both:
            store_block(j, accs[j][:HEAD_DIM] / accs[j][HEAD_DIM:HEAD_DIM + 1])

    def attend(near, c, accs, thrs):
        return tuple(accs[j] + bounded_pair(j, near, c, thrs[j]) for j in both)

    def fused_body(near, c, carry):
        accs, state = carry
        thrs = [jnp.where(c < n_prv, thr_prv[j], jnp.inf) for j in both]
        accs = attend(near, jnp.minimum(c, n_prv - 1), accs, thrs)
        for _ in range(SEARCH_PASSES):
            state = search_unit(state)
        return accs, state

    def fused_path():
        carry = (acc0, search_init)
        carry = lax.fori_loop(0, n_far, functools.partial(fused_body, False), carry)
        accs, state = lax.fori_loop(n_far, n_cur, functools.partial(fused_body, True), carry)
        store_thresholds(state)
        finish_bounded(accs)
        return 0

    def general_body(j, c, carry):
        m, l, acc = carry
        t = tiles(j, c, True, thr_prv[j], lambda t, sel: jnp.where(sel, t, MASKED))
        m_new = jnp.maximum(m, jnp.max(t, axis=0, keepdims=True))
        alpha = jnp.exp2(m - m_new)
        p = jnp.exp2(t - m_new)
        l_new = alpha * l + jnp.sum(p, axis=0, keepdims=True)
        pv = jnp.dot(vt_ref[c], p.astype(BF16), preferred_element_type=F32)
        return m_new, l_new, acc * alpha + pv

    def separate_path():
        @pl.when(s < ns)
        def _():
            store_thresholds(lax.fori_loop(0, SEARCH_PASSES * n_cur, lambda _, st: search_unit(st), search_init))

        @pl.when((s >= 1) & (bounded_ref[0] > 0))
        def _():
            accs = lax.fori_loop(0, n_far, lambda c, a: attend(False, c, a, thr_prv), acc0)
            finish_bounded(lax.fori_loop(n_far, n_prv, lambda c, a: attend(True, c, a, thr_prv), accs))

        @pl.when((s >= 1) & (bounded_ref[0] <= 0))
        def _():
            for j in both:
                m0 = jnp.full((1, lanes), MASKED, F32)
                l0 = jnp.zeros((1, lanes), F32)
                _, l, acc = lax.fori_loop(0, n_prv * ATT_PAIR, functools.partial(general_body, j), (m0, l0, acc0[j]))
                store_block(j, acc[:HEAD_DIM] / l)

        return 0

    lax.cond((bounded_ref[0] > 0) & (s >= 1) & (s < ns), fused_path, separate_path)


def _dsa_attention(bounded, qt, iqt, iwt, gate, k, ik, vt, bias):
    ns = qt.shape[0] // BLOCKS_PER_STEP
    T = k.shape[0]
    width = DSA_HEADS * HEAD_DIM
    rows = BLOCKS_PER_STEP * QB

    def prv(s):
        return jnp.maximum(s - 1, 0)

    def cur(s):
        return jnp.minimum(s, ns - 1)

    return pl.pallas_call(
        _dsa_kernel,
        grid=(ns + 1,),
        in_specs=[pl.BlockSpec(memory_space=pltpu.SMEM),
                  pl.BlockSpec((BLOCKS_PER_STEP, QT_ROWS, DSA_HEADS * QB), lambda s: (prv(s), 0, 0)),
                  pl.BlockSpec((BLOCKS_PER_STEP, HEAD_DIM, IDX_HEADS * QB), lambda s: (cur(s), 0, 0)),
                  pl.BlockSpec((IDX_HEADS, rows), lambda s: (0, cur(s))),
                  pl.BlockSpec((rows, width), lambda s: (prv(s), 0)),
                  _const_spec(k.shape),
                  _const_spec(ik.shape),
                  _const_spec(vt.shape),
                  _const_spec(bias.shape)],
        out_specs=pl.BlockSpec((rows, width), lambda s: (prv(s), 0)),
        out_shape=jax.ShapeDtypeStruct((T, width), BF16),
        scratch_shapes=[pltpu.VMEM((2, BLOCKS_PER_STEP, T, QB), F32), pltpu.VMEM((2, BLOCKS_PER_STEP, 8, QB), F32)],
        compiler_params=_cparams(),
        name="dsa_attention",
    )(bounded, qt, iqt, iwt, gate, k, ik, vt, bias)


def _swa_kernel(bounded_ref, qt_ref, kp_ref, kc_ref, vp_ref, vc_ref, sink_ref, gate_ref, bias_ref, out_ref):
    i = pl.program_id(0)
    grp = SWA_Q_HEADS // SWA_KV_HEADS
    shape = (2 * QB, QB)
    c_idx = lax.broadcasted_iota(I32, shape, 0)
    dist = QB + lax.broadcasted_iota(I32, shape, 1) - c_idx
    mask = (dist >= 0) & (dist < QB) & ((c_idx >= QB) | (i > 0))

    def group(kv, bounded):
        kcat = jnp.concatenate([kp_ref[kv], kc_ref[kv]], axis=0)
        vcat = jnp.concatenate([vp_ref[kv], vc_ref[kv]], axis=1)
        qg = qt_ref[0, :, kv * grp * QB:(kv + 1) * grp * QB]
        s = jnp.dot(kcat, qg, preferred_element_type=F32)
        sink = sink_ref[:, kv * grp * QB:(kv + 1) * grp * QB]
        logits = [s[:, j * QB:(j + 1) * QB] + bias_ref[kv * grp + j] for j in range(grp)]
        if bounded:
            p = jnp.concatenate([jnp.where(mask, jnp.exp2(t), 0.0) for t in logits], axis=1)
            pv = jnp.dot(vcat, p.astype(BF16), preferred_element_type=F32)
            return pv[:HEAD_DIM] / (pv[HEAD_DIM:HEAD_DIM + 1] + jnp.exp2(sink))
        t = jnp.concatenate([jnp.where(mask, t, MASKED) for t in logits], axis=1)
        m = jnp.maximum(jnp.max(t, axis=0, keepdims=True), sink)
        p = jnp.exp2(t - m)
        l = jnp.sum(p, axis=0, keepdims=True) + jnp.exp2(sink - m)
        pv = jnp.dot(vcat, p.astype(BF16), preferred_element_type=F32)
        return pv[:HEAD_DIM] / l

    def path(bounded):
        return jnp.concatenate([group(kv, bounded) for kv in range(SWA_KV_HEADS)], axis=1)

    o_t = lax.cond(bounded_ref[0] > 0, functools.partial(path, True), functools.partial(path, False))
    _store_gated(o_t, gate_ref, out_ref, SWA_Q_HEADS)


def _swa_attention(bounded, qt, k_hm, vt_hm, sink_b, gate, bias):
    nb = qt.shape[0]
    T = k_hm.shape[1]
    width = SWA_Q_HEADS * HEAD_DIM

    def prev(i):
        return jnp.maximum(i - 1, 0)

    return pl.pallas_call(
        _swa_kernel,
        grid=(nb,),
        in_specs=[pl.BlockSpec(memory_space=pltpu.SMEM),
                  pl.BlockSpec((1, HEAD_DIM, SWA_Q_HEADS * QB), lambda i: (i, 0, 0)),
                  pl.BlockSpec((SWA_KV_HEADS, QB, HEAD_DIM), lambda i: (0, prev(i), 0)),
                  pl.BlockSpec((SWA_KV_HEADS, QB, HEAD_DIM), lambda i: (0, i, 0)),
                  pl.BlockSpec((SWA_KV_HEADS, VT_ROWS, QB), lambda i: (0, 0, prev(i))),
                  pl.BlockSpec((SWA_KV_HEADS, VT_ROWS, QB), lambda i: (0, 0, i)),
                  _const_spec((1, SWA_Q_HEADS * QB)),
                  pl.BlockSpec((QB, width), lambda i: (i, 0)),
                  _const_spec(bias.shape)],
        out_specs=pl.BlockSpec((QB, width), lambda i: (i, 0)),
        out_shape=jax.ShapeDtypeStruct((T, width), BF16),
        compiler_params=_cparams(),
        name="swa_attention",
    )(bounded, qt, k_hm, k_hm, vt_hm, vt_hm, sink_b, gate, bias)


def _pad_cols(w, width):
    return jnp.pad(w, ((0, 0), (0, width - w.shape[1])))


def _even_layer(x, norm_g, w_in, pool_w, pool_scale, q_gain, k_gain, w_out, rel_bias, dsa_bias):
    pool_width = pool_w.shape[0] * pool_w.shape[1]
    dsa_width = DSA_HEADS * HEAD_DIM
    w_all = w_in.astype(BF16)
    w_tail = _pad_cols(w_in[:, w_in.shape[1] // 128 * 128:], 128).astype(BF16)

    q_scale = HEAD_DIM ** -0.5 * LOG2E
    far = rel_bias[NUM_BUCKETS - 1, :DSA_HEADS].astype(F32) * LOG2E
    far_hi = far.astype(BF16)
    far_lo = (far - far_hi.astype(F32)).astype(BF16)
    extra = jnp.zeros((QT_ROWS - HEAD_DIM, DSA_HEADS * QB), BF16)
    extra = extra.at[0].set(jnp.repeat(far_hi, QB)).at[1].set(jnp.repeat(far_lo, QB))
    q_norm = jnp.sqrt(HEAD_DIM * jnp.max(q_gain.astype(F32) ** 2) * q_scale ** 2 + jnp.max(far ** 2))
    k_norm = jnp.sqrt(HEAD_DIM * jnp.max(k_gain.astype(F32) ** 2) + 2.0)
    near_max = jnp.max(jnp.abs(rel_bias[:, :DSA_HEADS] - rel_bias[NUM_BUCKETS - 1, :DSA_HEADS])) * LOG2E
    bound = 1.02 * q_norm * k_norm + near_max
    bounded = (bound < SAFE_LOG2_RANGE).astype(I32).reshape(1)

    py, qt, iqt, gate, k, ik, vt, iwt = _even_proj(
        x, norm_g, w_all, w_tail, pool_w.astype(BF16), pool_scale, q_gain, extra, k_gain, pool_width=pool_width,
        dsa_width=dsa_width, q_scale=q_scale, iw_scale=IDX_HEADS ** -0.5 * HEAD_DIM ** -0.5)
    dy = _dsa_attention(bounded, qt, iqt, iwt, gate, k, ik, vt, dsa_bias)
    return _out_proj(x, [py, dy], w_out)


def _odd_layer(x, norm_g, w_in, q_gain, k_gain, sinks, w_out, rel_bias, swa_bias):
    q_width = SWA_Q_HEADS * HEAD_DIM
    w_all = w_in.astype(BF16)
    q_scale = HEAD_DIM ** -0.5 * LOG2E
    sink_b = jnp.repeat(sinks.astype(F32) * LOG2E, QB).reshape(1, SWA_Q_HEADS * QB)
    q_norm = jnp.sqrt(HEAD_DIM * jnp.max(q_gain.astype(F32) ** 2)) * q_scale
    k_norm = jnp.sqrt(HEAD_DIM * jnp.max(k_gain.astype(F32) ** 2))
    bound = jnp.maximum(1.02 * q_norm * k_norm + jnp.max(jnp.abs(rel_bias)) * LOG2E, jnp.max(jnp.abs(sink_b)))
    bounded = (bound < SAFE_LOG2_RANGE).astype(I32).reshape(1)

    qt, gate, k_hm, vt_hm = _odd_proj(x, norm_g, w_all, q_gain, k_gain, q_width=q_width, q_scale=q_scale)
    go = _swa_attention(bounded, qt, k_hm, vt_hm, sink_b, gate, swa_bias)
    return _out_proj(x, [go], w_out)


def kernel(x, rel_bias, even_norm, even_w_in, even_pool_w, even_pool_scale, even_q_gain, even_k_gain,
           even_w_out, odd_norm, odd_w_in, odd_q_gain, odd_k_gain, odd_sinks, odd_w_out):
    B, T, D = x.shape
    depth = even_norm.shape[0] + odd_norm.shape[0]
    dsa_bias = _bias_tiles(rel_bias, N_BIAS_TILES, DSA_HEADS, QB, QB, 0, minus_far=True)
    swa_bias = _bias_tiles(rel_bias, 1, SWA_Q_HEADS, 2 * QB, 0, QB)[0]
    outs = []
    for b in range(B):
        h = x.reshape(T, D) if B == 1 else x[b]
        for layer in range(depth):
            j = layer // 2
            if layer % 2 == 0:
                h = _even_layer(h, even_norm[j], even_w_in[j], even_pool_w[j], even_pool_scale[j],
                                even_q_gain[j], even_k_gain[j], even_w_out[j], rel_bias, dsa_bias)
            else:
                h = _odd_layer(h, odd_norm[j], odd_w_in[j], odd_q_gain[j], odd_k_gain[j],
                               odd_sinks[j], odd_w_out[j], rel_bias, swa_bias)
        outs.append(h)
    return outs[0].reshape(B, T, D) if B == 1 else jnp.stack(outs, axis=0)
```

```python
import functools

import jax
import jax.numpy as jnp
from jax import lax
from jax.experimental import pallas as pl
from jax.experimental.pallas import tpu as pltpu

F32 = jnp.float32
BF16 = jnp.bfloat16
I32 = jnp.int32

EPS = 1e-6
HEAD_DIM = 64
QB = 128
POOL_WINDOWS = (2, 4, 8, 16)
POOL_HALO = 16
DSA_HEADS = 16
IDX_HEADS = 16
DSA_TOPK = 256
SWA_Q_HEADS = 32
SWA_KV_HEADS = 4
NUM_BUCKETS = 32
IDX_CHUNK = 512
ATT_CHUNK = 256
ATT_PAIR = IDX_CHUNK // ATT_CHUNK
INT_MIN = -(2 ** 31)
FINITE_MIN_KEY = -0x7F800000
SEARCH_PASSES = 32
BLOCKS_PER_STEP = 2
MASKED = -1e30
LOG2E = 1.4426950408889634
QT_ROWS = 128
VT_ROWS = 80
SAFE_LOG2_RANGE = 60.0
TM = 256
TM_OUT = 512
VMEM_LIMIT = 56 * 1024 * 1024


def _bucket_starts():
    max_exact = NUM_BUCKETS // 2
    starts = list(range(max_exact + 1))
    n = max_exact
    for b in range(max_exact + 1, NUM_BUCKETS):
        while n ** 16 < max_exact ** 16 * 64 ** (b - max_exact):
            n += 1
        starts.append(n)
    return tuple(starts)


BUCKET_START = _bucket_starts()
FAR_DELTA = -(-(BUCKET_START[-1] + QB - 1) // QB)
N_BIAS_TILES = FAR_DELTA + 1


def _cparams(n_grid=1):
    return pltpu.CompilerParams(dimension_semantics=("arbitrary",) * n_grid,
                                vmem_limit_bytes=VMEM_LIMIT)


def _const_spec(shape):
    return pl.BlockSpec(shape, lambda i: (0,) * len(shape), pipeline_mode=pl.Buffered(1))


def _rms_rows_bf16(x, g):
    ms = jnp.mean(x * x, axis=-1, keepdims=True)
    return (x * lax.rsqrt(ms + EPS) * g).astype(BF16)


def _silu(x):
    return x / (1.0 + jnp.exp(-x))


def _bias_tiles_kernel(rb_ref, out_ref, *, n_heads, rows, base_step, base0, minus_far):
    base = base0 + pl.program_id(0) * base_step
    span = rows + QB
    dist = base - rows + lax.broadcasted_iota(I32, (8, span), 1)
    at_least = [dist >= BUCKET_START[b] for b in range(1, NUM_BUCKETS)]
    for h in range(n_heads):
        val = jnp.full((8, span), rb_ref[0, h], F32)
        for b in range(1, NUM_BUCKETS):
            val = jnp.where(at_least[b - 1], rb_ref[b, h], val)
        if minus_far:
            val = val - rb_ref[NUM_BUCKETS - 1, h]
        wide = jnp.broadcast_to(val[0:1] * LOG2E, (rows, span))
        out_ref[0, h] = pltpu.roll(wide, 0, 1, stride=1, stride_axis=0)[:, rows:]


def _bias_tiles(rel_bias, n_tiles, n_heads, rows, base_step, base0, minus_far=False):
    kern = functools.partial(_bias_tiles_kernel, n_heads=n_heads, rows=rows, base_step=base_step, base0=base0,
                             minus_far=minus_far)
    return pl.pallas_call(
        kern,
        grid=(n_tiles,),
        in_specs=[pl.BlockSpec(memory_space=pltpu.SMEM)],
        out_specs=pl.BlockSpec((1, n_heads, rows, QB), lambda t: (t, 0, 0, 0)),
        out_shape=jax.ShapeDtypeStruct((n_tiles, n_heads, rows, QB), F32),
        compiler_params=_cparams(),
        name="bias_tiles",
    )(rel_bias)


def _store_heads_t(h, out_ref, gain_ref, *, n_heads, normalize, scale):
    for b in range(TM // QB):
        rows = h[b * QB:(b + 1) * QB]
        for p in range(n_heads // 2):
            pair = rows[:, p * 128:(p + 1) * 128].T
            for hh in range(2):
                t = pair[hh * HEAD_DIM:(hh + 1) * HEAD_DIM]
                if normalize:
                    ms = jnp.mean(t * t, axis=0, keepdims=True)
                    t = t * lax.rsqrt(ms + EPS) * gain_ref[...] * scale
                head = 2 * p + hh
                out_ref[b, :HEAD_DIM, head * QB:(head + 1) * QB] = t.astype(BF16)


def _pool_mix(i, pin, gate, pw_ref, ps_ref, out_ref, halo_ref):
    width = pin.shape[1]
    gc = width // len(POOL_WINDOWS)

    @pl.when(i == 0)
    def _():
        halo_ref[...] = jnp.zeros_like(halo_ref)

    ext = jnp.concatenate([halo_ref[...], pin], axis=0)
    halo_ref[...] = pin[TM - POOL_HALO:]
    pos = i * TM + lax.broadcasted_iota(I32, (TM, gc), 0)
    outs = []
    for grp, win in enumerate(POOL_WINDOWS):
        e = ext[:, grp * gc:(grp + 1) * gc]
        s = e
        span = 1
        while span < win:
            s = s + pltpu.roll(s, span, 0)
            span *= 2
        s = s[POOL_HALO:]
        a = pin[:, grp * gc:(grp + 1) * gc]
        cnt = jnp.minimum(pos + 1, win).astype(F32)
        pooled = s / cnt - a
        y = jnp.dot(pooled.astype(BF16), pw_ref[grp], preferred_element_type=F32)
        outs.append(y * ps_ref[:, grp * gc:(grp + 1) * gc])
    py = jnp.concatenate(outs, axis=1)
    out_ref[...] = (_silu(gate) * py).astype(BF16)


def _ones_row_pad(cols):
    row = lax.broadcasted_iota(I32, (VT_ROWS - HEAD_DIM, cols), 0)
    return jnp.where(row == 0, 1.0, 0.0)


def _even_proj_kernel(x_ref, g_ref, w_ref, wt_ref, pw_ref, ps_ref, qg_ref, extra_ref, kg_ref,
                      py_ref, qt_ref, iqt_ref, gate_ref, k_ref, ik_ref, vt_ref, iwt_ref, halo_ref,
                      *, pool_w, dsa_w, q_scale, iw_scale):
    i = pl.program_id(0)
    xn = _rms_rows_bf16(x_ref[...], g_ref[...])
    col = 0

    def proj(width):
        nonlocal col
        h = jnp.dot(xn, w_ref[:, col:col + width], preferred_element_type=F32)
        col += width
        return h

    h = proj(2 * pool_w)
    _pool_mix(i, h[:, :pool_w], h[:, pool_w:], pw_ref, ps_ref, py_ref, halo_ref)

    for b in range(TM // QB):
        qt_ref[b, HEAD_DIM:, :] = extra_ref[...]
    _store_heads_t(proj(dsa_w), qt_ref, qg_ref, n_heads=DSA_HEADS, normalize=True, scale=q_scale)

    kv = proj(2 * HEAD_DIM)
    lane = lax.broadcasted_iota(I32, kv.shape, 1)
    is_k = lane < HEAD_DIM
    ms = jnp.sum(jnp.where(is_k, kv * kv, 0.0), axis=-1, keepdims=True) * (1.0 / HEAD_DIM)
    kn = kv * lax.rsqrt(ms + EPS) * kg_ref[...]
    k_ref[...] = jnp.where(is_k, kn, jnp.where(lane < HEAD_DIM + 2, 1.0, 0.0)).astype(BF16)
    vt_ref[0] = jnp.concatenate([kv.T[HEAD_DIM:], _ones_row_pad(TM)], axis=0).astype(BF16)

    gate_ref[...] = _silu(proj(dsa_w)).astype(BF16)
    _store_heads_t(proj(IDX_HEADS * HEAD_DIM), iqt_ref, qg_ref, n_heads=IDX_HEADS, normalize=False, scale=1.0)

    iw = jnp.dot(xn, wt_ref[...], preferred_element_type=F32)
    ik_ref[...] = iw[:, :HEAD_DIM].astype(BF16)
    iwt_ref[...] = iw.T[HEAD_DIM:HEAD_DIM + IDX_HEADS] * iw_scale


def _even_proj(x, g, w, w_tail, pool_w, pool_scale, q_gain, extra, k_gain, *, pool_width, dsa_width, q_scale,
               iw_scale):
    T, D = x.shape
    assert TM == ATT_CHUNK
    nb = T // QB
    lanes = DSA_HEADS * QB
    gain_b = jnp.broadcast_to(q_gain.reshape(HEAD_DIM, 1), (HEAD_DIM, QB)).astype(F32)
    kern = functools.partial(_even_proj_kernel, pool_w=pool_width, dsa_w=dsa_width, q_scale=q_scale,
                             iw_scale=iw_scale)
    row_blk = lambda width: pl.BlockSpec((TM, width), lambda i: (i, 0))
    return pl.pallas_call(
        kern,
        grid=(T // TM,),
        in_specs=[row_blk(D),
                  _const_spec((1, D)),
                  _const_spec(w.shape),
                  _const_spec(w_tail.shape),
                  _const_spec(pool_w.shape),
                  _const_spec((1, pool_width)),
                  _const_spec((HEAD_DIM, QB)),
                  _const_spec(extra.shape),
                  _const_spec((1, QT_ROWS))],
        out_specs=[row_blk(pool_width),
                   pl.BlockSpec((TM // QB, QT_ROWS, lanes), lambda i: (i, 0, 0)),
                   pl.BlockSpec((TM // QB, HEAD_DIM, lanes), lambda i: (i, 0, 0)),
                   row_blk(dsa_width),
                   row_blk(QT_ROWS),
                   row_blk(HEAD_DIM),
                   pl.BlockSpec((1, VT_ROWS, ATT_CHUNK), lambda i: (i, 0, 0)),
                   pl.BlockSpec((IDX_HEADS, TM), lambda i: (0, i))],
        out_shape=[jax.ShapeDtypeStruct((T, pool_width), BF16),
                   jax.ShapeDtypeStruct((nb, QT_ROWS, lanes), BF16),
                   jax.ShapeDtypeStruct((nb, HEAD_DIM, lanes), BF16),
                   jax.ShapeDtypeStruct((T, dsa_width), BF16),
                   jax.ShapeDtypeStruct((T, QT_ROWS), BF16),
                   jax.ShapeDtypeStruct((T, HEAD_DIM), BF16),
                   jax.ShapeDtypeStruct((T // ATT_CHUNK, VT_ROWS, ATT_CHUNK), BF16),
                   jax.ShapeDtypeStruct((IDX_HEADS, T), F32)],
        scratch_shapes=[pltpu.VMEM((POOL_HALO, pool_width), F32)],
        compiler_params=_cparams(),
        name="even_proj",
    )(x, g.reshape(1, D), w, w_tail, pool_w, pool_scale.reshape(1, pool_width), gain_b, extra,
      _pad_cols(k_gain.reshape(1, HEAD_DIM), QT_ROWS))


def _odd_proj_kernel(x_ref, g_ref, w_ref, qg_ref, kg_ref, qt_ref, gate_ref, k_ref, vt_ref, *, q_w, q_scale):
    xn = _rms_rows_bf16(x_ref[...], g_ref[...])
    kv_w = SWA_KV_HEADS * HEAD_DIM
    hq = jnp.dot(xn, w_ref[:, :q_w], preferred_element_type=F32)
    _store_heads_t(hq, qt_ref, qg_ref, n_heads=SWA_Q_HEADS, normalize=True, scale=q_scale)
    h = jnp.dot(xn, w_ref[:, q_w:q_w + 2 * kv_w], preferred_element_type=F32)
    for hd in range(SWA_KV_HEADS):
        k = h[:, hd * HEAD_DIM:(hd + 1) * HEAD_DIM]
        ms = jnp.mean(k * k, axis=-1, keepdims=True)
        k_ref[hd] = (k * lax.rsqrt(ms + EPS) * kg_ref[...]).astype(BF16)
    ones_rows = _ones_row_pad(TM)
    for p in range(SWA_KV_HEADS // 2):
        pair = h[:, kv_w + p * 128:kv_w + (p + 1) * 128].T
        vt_ref[2 * p] = jnp.concatenate([pair[:HEAD_DIM], ones_rows], axis=0).astype(BF16)
        vt_ref[2 * p + 1] = jnp.concatenate([pair[HEAD_DIM:], ones_rows], axis=0).astype(BF16)
    gate = jnp.dot(xn, w_ref[:, q_w + 2 * kv_w:], preferred_element_type=F32)
    gate_ref[...] = _silu(gate).astype(BF16)


def _odd_proj(x, g, w, q_gain, k_gain, *, q_width, q_scale):
    T, D = x.shape
    nb = T // QB
    lanes = SWA_Q_HEADS * QB
    gain_b = jnp.broadcast_to(q_gain.reshape(HEAD_DIM, 1), (HEAD_DIM, QB)).astype(F32)
    kern = functools.partial(_odd_proj_kernel, q_w=q_width, q_scale=q_scale)
    return pl.pallas_call(
        kern,
        grid=(T // TM,),
        in_specs=[pl.BlockSpec((TM, D), lambda i: (i, 0)),
                  _const_spec((1, D)),
                  _const_spec(w.shape),
                  _const_spec((HEAD_DIM, QB)),
                  _const_spec((1, HEAD_DIM))],
        out_specs=[pl.BlockSpec((TM // QB, HEAD_DIM, lanes), lambda i: (i, 0, 0)),
                   pl.BlockSpec((TM, q_width), lambda i: (i, 0)),
                   pl.BlockSpec((SWA_KV_HEADS, TM, HEAD_DIM), lambda i: (0, i, 0)),
                   pl.BlockSpec((SWA_KV_HEADS, VT_ROWS, TM), lambda i: (0, 0, i))],
        out_shape=[jax.ShapeDtypeStruct((nb, HEAD_DIM, lanes), BF16),
                   jax.ShapeDtypeStruct((T, q_width), BF16),
                   jax.ShapeDtypeStruct((SWA_KV_HEADS, T, HEAD_DIM), BF16),
                   jax.ShapeDtypeStruct((SWA_KV_HEADS, VT_ROWS, T), BF16)],
        compiler_params=_cparams(),
        name="odd_proj",
    )(x, g.reshape(1, D), w, gain_b, k_gain.reshape(1, HEAD_DIM))


def _out_proj_kernel(*refs):
    x_ref, w_ref, out_ref = refs[0], refs[-2], refs[-1]
    acc = x_ref[...]
    row = 0
    for a_ref in refs[1:-2]:
        width = a_ref.shape[1]
        acc = acc + jnp.dot(a_ref[...], w_ref[row:row + width, :].astype(BF16), preferred_element_type=F32)
        row += width
    out_ref[...] = acc


def _out_proj(x, parts, w):
    T, D = x.shape
    row_blk = lambda width: pl.BlockSpec((TM_OUT, width), lambda i: (i, 0))
    return pl.pallas_call(
        _out_proj_kernel,
        grid=(T // TM_OUT,),
        in_specs=[row_blk(D)] + [row_blk(a.shape[1]) for a in parts] + [_const_spec(w.shape)],
        out_specs=row_blk(D),
        out_shape=jax.ShapeDtypeStruct((T, D), F32),
        compiler_params=_cparams(),
        name="out_proj",
    )(x, *parts, w)


def _store_gated(o_t, gate_ref, out_ref, n_heads):
    for p in range(n_heads // 2):
        pair = jnp.concatenate([o_t[:, (2 * p) * QB:(2 * p + 1) * QB],
                                o_t[:, (2 * p + 1) * QB:(2 * p + 2) * QB]], axis=0)
        g = gate_ref[:, p * 128:(p + 1) * 128].astype(F32)
        out_ref[:, p * 128:(p + 1) * 128] = (g * pair.T).astype(BF16)


def _dsa_kernel(bounded_ref, qt_ref, iqt_ref, iwt_ref, gate_ref, k_ref, ik_ref, vt_ref, bias_ref, out_ref,
                score_ref, thr_ref):
    s = pl.program_id(0)
    ns = pl.num_programs(0) - 1
    lanes = DSA_HEADS * QB
    sub = ATT_CHUNK // QB
    cur = s
    prv = jnp.maximum(s - 1, 0)
    pairs_per_chunk = IDX_CHUNK // (BLOCKS_PER_STEP * QB)
    n_cur = cur // pairs_per_chunk + 1
    n_prv = prv // pairs_per_chunk + 1
    cur_slot = cur % 2
    prv_slot = prv % 2
    both = range(BLOCKS_PER_STEP)

    @pl.when(s < ns)
    def _():
        iqt = [iqt_ref[j] for j in both]
        iwt = iwt_ref[...]
        krow = lax.broadcasted_iota(I32, (IDX_CHUNK, QB), 0)
        lane_pos = lax.broadcasted_iota(I32, (IDX_CHUNK, QB), 1)

        def idx_body(c, carry):
            off = pl.multiple_of(c * IDX_CHUNK, IDX_CHUNK)
            ik = ik_ref[pl.ds(off, IDX_CHUNK), :]
            for j in both:
                sc = jnp.dot(ik, iqt[j], preferred_element_type=F32)
                score = jnp.zeros((IDX_CHUNK, QB), F32)
                for h in range(IDX_HEADS):
                    score = score + jnp.maximum(sc[:, h * QB:(h + 1) * QB], 0.0) * iwt[h:h + 1, j * QB:(j + 1) * QB]
                causal = off + krow <= (cur * BLOCKS_PER_STEP + j) * QB + lane_pos
                score_ref[cur_slot, j, pl.ds(off, IDX_CHUNK), :] = jnp.where(causal, score, -jnp.inf)
            return carry

        lax.fori_loop(0, n_cur, idx_body, 0)

    def key_to_float(key):
        return lax.bitcast_convert_type(jnp.where(key < 0, key ^ 0x7FFFFFFF, key), F32)

    def search_unit(state):
        bases, bit, accs, ch, kept = state
        off = pl.multiple_of(ch * IDX_CHUNK, IDX_CHUNK)
        last = ch + 1 == n_cur
        new_bases, new_accs, new_kept = [], [], []
        for j in both:
            cand = key_to_float(bases[j] + bit)
            hit = jnp.where(score_ref[cur_slot, j, pl.ds(off, IDX_CHUNK), :] >= cand, 1, 0).astype(I32)
            acc = accs[j] + jnp.sum(hit.reshape(IDX_CHUNK // 8, 8, QB), axis=0)
            total = jnp.sum(acc, axis=0, keepdims=True)
            take = last & (total >= DSA_TOPK)
            new_bases.append(jnp.where(take, bases[j] + bit, bases[j]))
            new_kept.append(jnp.where(take, total, kept[j]))
            new_accs.append(jnp.where(last, 0, acc))
        bit = jnp.where(last, lax.shift_right_logical(bit, 1), bit)
        ch = jnp.where(last, 0, ch + 1)
        return tuple(new_bases), bit, tuple(new_accs), ch, tuple(new_kept)

    search_init = (tuple(jnp.full((1, QB), INT_MIN, I32) for _ in both), jnp.int32(INT_MIN),
                   tuple(jnp.zeros((8, QB), I32) for _ in both), jnp.int32(0),
                   tuple(jnp.full((1, QB), jnp.iinfo(jnp.int32).max, I32) for _ in both))

    def count_rows(j, pred):
        rows = lax.broadcasted_iota(I32, (IDX_CHUNK, QB), 0)

        def body(c, acc):
            off = pl.multiple_of(c * IDX_CHUNK, IDX_CHUNK)
            hit = jnp.where(pred(score_ref[cur_slot, j, pl.ds(off, IDX_CHUNK), :], off + rows), 1, 0).astype(I32)
            return acc + jnp.sum(hit.reshape(IDX_CHUNK // 8, 8, QB), axis=0)

        return jnp.sum(lax.fori_loop(0, n_cur, body, jnp.zeros((8, QB), I32)), axis=0, keepdims=True)

    def settle_ties(state):
        bases, _, _, _, kept = state
        tied = [(kept[j] > DSA_TOPK) & (bases[j] >= FINITE_MIN_KEY) for j in both]

        @pl.when(jnp.max(jnp.where(tied[0] | tied[1], 1, 0)) > 0)
        def _():
            rows = lax.broadcasted_iota(I32, (IDX_CHUNK, QB), 0)
            index_bits = score_ref.shape[2].bit_length()
            for j in both:
                thr = key_to_float(bases[j])
                wanted = DSA_TOPK - count_rows(j, lambda sc, r: sc > thr)

                def index_bit(b, first_out):
                    cand = first_out + jnp.left_shift(jnp.int32(1), index_bits - 1 - b)
                    below = count_rows(j, lambda sc, r: (sc == thr) & (r < cand)) < wanted
                    return jnp.where(below, cand, first_out)

                cut = lax.fori_loop(0, index_bits, index_bit, jnp.zeros((1, QB), I32))

                def nudge(c, carry):
                    off = pl.multiple_of(c * IDX_CHUNK, IDX_CHUNK)
                    sc = score_ref[cur_slot, j, pl.ds(off, IDX_CHUNK), :]
                    surplus = tied[j] & (sc == thr) & (off + rows > cut)
                    score_ref[cur_slot, j, pl.ds(off, IDX_CHUNK), :] = jnp.where(surplus, -jnp.inf, sc)
                    return carry

                lax.fori_loop(0, n_cur, nudge, 0)

    def store_thresholds(state):
        for j in both:
            thr = key_to_float(jnp.maximum(state[0][j], FINITE_MIN_KEY))
            thr_ref[cur_slot, j] = jnp.broadcast_to(thr, (8, QB))

    qts = [qt_ref[j] for j in both]
    thr_prv = [thr_ref[prv_slot, j][0:1] for j in both]

    def tiles(j, c, near, thr, fn):
        off = pl.multiple_of(c * ATT_CHUNK, ATT_CHUNK)
        sc = jnp.dot(k_ref[pl.ds(off, ATT_CHUNK), :], qts[j], preferred_element_type=F32)
        sel = score_ref[prv_slot, j, pl.ds(off, ATT_CHUNK), :] >= thr
        blocks = []
        for sb in range(sub):
            delta = jnp.clip(prv * BLOCKS_PER_STEP + j - (c * sub + sb), 0, FAR_DELTA)
            sel_sb = sel[sb * QB:(sb + 1) * QB]
            row = []
            for h in range(DSA_HEADS):
                t = sc[sb * QB:(sb + 1) * QB, h * QB:(h + 1) * QB]
                if near:
                    t = t + bias_ref[delta, h]
                row.append(fn(t, sel_sb))
            blocks.append(jnp.concatenate(row, axis=1))
        return jnp.concatenate(blocks, axis=0)

    def bounded_pair(j, near, pair, thr):
        pvs = []
        for half in range(ATT_PAIR):
            c = pair * ATT_PAIR + half
            p = tiles(j, c, near, thr, lambda t, sel: jnp.where(sel, jnp.exp2(t), 0.0))
            pvs.append(jnp.dot(vt_ref[c], p.astype(BF16), preferred_element_type=F32))
        return sum(pvs[1:], pvs[0])

    n_far = jnp.maximum(prv * BLOCKS_PER_STEP - (FAR_DELTA - 1), 0) // (sub * ATT_PAIR)
    acc0 = tuple(jnp.zeros((VT_ROWS, lanes), F32) for _ in both)

    def store_block(j, o_t):
        rows = slice(j * QB, (j + 1) * QB)
        for p in range(DSA_HEADS // 2):
            pair = jnp.concatenate([o_t[:, (2 * p) * QB:(2 * p + 1) * QB],
                                    o_t[:, (2 * p + 1) * QB:(2 * p + 2) * QB]], axis=0)
            g = gate_ref[rows, p * 128:(p + 1) * 128].astype(F32)
            out_ref[rows, p * 128:(p + 1) * 128] = (g * pair.T).astype(BF16)

    def finish_bounded(accs):
        for j in both:
            store_block(j, accs[j][:HEAD_DIM] / accs[j][HEAD_DIM:HEAD_DIM + 1])

    def attend(near, c, accs, thrs):
        return tuple(accs[j] + bounded_pair(j, near, c, thrs[j]) for j in both)

    def fused_body(near, c, carry):
        accs, state = carry
        thrs = [jnp.where(c < n_prv, thr_prv[j], jnp.inf) for j in both]
        accs = attend(near, jnp.minimum(c, n_prv - 1), accs, thrs)
        for _ in range(SEARCH_PASSES):
            state = search_unit(state)
        return accs, state

    def fused_path():
        carry = (acc0, search_init)
        carry = lax.fori_loop(0, n_far, functools.partial(fused_body, False), carry)
        accs, state = lax.fori_loop(n_far, n_cur, functools.partial(fused_body, True), carry)
        settle_ties(state)
        store_thresholds(state)
        finish_bounded(accs)
        return 0

    def general_body(j, c, carry):
        m, l, acc = carry
        t = tiles(j, c, True, thr_prv[j], lambda t, sel: jnp.where(sel, t, MASKED))
        m_new = jnp.maximum(m, jnp.max(t, axis=0, keepdims=True))
        alpha = jnp.exp2(m - m_new)
        p = jnp.exp2(t - m_new)
        l_new = alpha * l + jnp.sum(p, axis=0, keepdims=True)
        pv = jnp.dot(vt_ref[c], p.astype(BF16), preferred_element_type=F32)
        return m_new, l_new, acc * alpha + pv

    def separate_path():
        @pl.when(s < ns)
        def _():
            state = lax.fori_loop(0, SEARCH_PASSES * n_cur, lambda _, st: search_unit(st), search_init)
            settle_ties(state)
            store_thresholds(state)

        @pl.when((s >= 1) & (bounded_ref[0] > 0))
        def _():
            accs = lax.fori_loop(0, n_far, lambda c, a: attend(False, c, a, thr_prv), acc0)
            finish_bounded(lax.fori_loop(n_far, n_prv, lambda c, a: attend(True, c, a, thr_prv), accs))

        @pl.when((s >= 1) & (bounded_ref[0] <= 0))
        def _():
            for j in both:
                m0 = jnp.full((1, lanes), MASKED, F32)
                l0 = jnp.zeros((1, lanes), F32)
                _, l, acc = lax.fori_loop(0, n_prv * ATT_PAIR, functools.partial(general_body, j), (m0, l0, acc0[j]))
                store_block(j, acc[:HEAD_DIM] / l)

        return 0

    lax.cond((bounded_ref[0] > 0) & (s >= 1) & (s < ns), fused_path, separate_path)


def _dsa_attention(bounded, qt, iqt, iwt, gate, k, ik, vt, bias):
    ns = qt.shape[0] // BLOCKS_PER_STEP
    T = k.shape[0]
    width = DSA_HEADS * HEAD_DIM
    rows = BLOCKS_PER_STEP * QB

    def prv(s):
        return jnp.maximum(s - 1, 0)

    def cur(s):
        return jnp.minimum(s, ns - 1)

    return pl.pallas_call(
        _dsa_kernel,
        grid=(ns + 1,),
        in_specs=[pl.BlockSpec(memory_space=pltpu.SMEM),
                  pl.BlockSpec((BLOCKS_PER_STEP, QT_ROWS, DSA_HEADS * QB), lambda s: (prv(s), 0, 0)),
                  pl.BlockSpec((BLOCKS_PER_STEP, HEAD_DIM, IDX_HEADS * QB), lambda s: (cur(s), 0, 0)),
                  pl.BlockSpec((IDX_HEADS, rows), lambda s: (0, cur(s))),
                  pl.BlockSpec((rows, width), lambda s: (prv(s), 0)),
                  _const_spec(k.shape),
                  _const_spec(ik.shape),
                  _const_spec(vt.shape),
                  _const_spec(bias.shape)],
        out_specs=pl.BlockSpec((rows, width), lambda s: (prv(s), 0)),
        out_shape=jax.ShapeDtypeStruct((T, width), BF16),
        scratch_shapes=[pltpu.VMEM((2, BLOCKS_PER_STEP, T, QB), F32), pltpu.VMEM((2, BLOCKS_PER_STEP, 8, QB), F32)],
        compiler_params=_cparams(),
        name="dsa_attention",
    )(bounded, qt, iqt, iwt, gate, k, ik, vt, bias)


def _swa_kernel(bounded_ref, qt_ref, kp_ref, kc_ref, vp_ref, vc_ref, sink_ref, gate_ref, bias_ref, out_ref):
    i = pl.program_id(0)
    grp = SWA_Q_HEADS // SWA_KV_HEADS
    shape = (2 * QB, QB)
    c_idx = lax.broadcasted_iota(I32, shape, 0)
    dist = QB + lax.broadcasted_iota(I32, shape, 1) - c_idx
    mask = (dist >= 0) & (dist < QB) & ((c_idx >= QB) | (i > 0))

    def group(kv, bounded):
        kcat = jnp.concatenate([kp_ref[kv], kc_ref[kv]], axis=0)
        vcat = jnp.concatenate([vp_ref[kv], vc_ref[kv]], axis=1)
        qg = qt_ref[0, :, kv * grp * QB:(kv + 1) * grp * QB]
        s = jnp.dot(kcat, qg, preferred_element_type=F32)
        sink = sink_ref[:, kv * grp * QB:(kv + 1) * grp * QB]
        logits = [s[:, j * QB:(j + 1) * QB] + bias_ref[kv * grp + j] for j in range(grp)]
        if bounded:
            p = jnp.concatenate([jnp.where(mask, jnp.exp2(t), 0.0) for t in logits], axis=1)
            pv = jnp.dot(vcat, p.astype(BF16), preferred_element_type=F32)
            return pv[:HEAD_DIM] / (pv[HEAD_DIM:HEAD_DIM + 1] + jnp.exp2(sink))
        t = jnp.concatenate([jnp.where(mask, t, MASKED) for t in logits], axis=1)
        m = jnp.maximum(jnp.max(t, axis=0, keepdims=True), sink)
        p = jnp.exp2(t - m)
        l = jnp.sum(p, axis=0, keepdims=True) + jnp.exp2(sink - m)
        pv = jnp.dot(vcat, p.astype(BF16), preferred_element_type=F32)
        return pv[:HEAD_DIM] / l

    def path(bounded):
        return jnp.concatenate([group(kv, bounded) for kv in range(SWA_KV_HEADS)], axis=1)

    o_t = lax.cond(bounded_ref[0] > 0, functools.partial(path, True), functools.partial(path, False))
    _store_gated(o_t, gate_ref, out_ref, SWA_Q_HEADS)


def _swa_attention(bounded, qt, k_hm, vt_hm, sink_b, gate, bias):
    nb = qt.shape[0]
    T = k_hm.shape[1]
    width = SWA_Q_HEADS * HEAD_DIM

    def prev(i):
        return jnp.maximum(i - 1, 0)

    return pl.pallas_call(
        _swa_kernel,
        grid=(nb,),
        in_specs=[pl.BlockSpec(memory_space=pltpu.SMEM),
                  pl.BlockSpec((1, HEAD_DIM, SWA_Q_HEADS * QB), lambda i: (i, 0, 0)),
                  pl.BlockSpec((SWA_KV_HEADS, QB, HEAD_DIM), lambda i: (0, prev(i), 0)),
                  pl.BlockSpec((SWA_KV_HEADS, QB, HEAD_DIM), lambda i: (0, i, 0)),
                  pl.BlockSpec((SWA_KV_HEADS, VT_ROWS, QB), lambda i: (0, 0, prev(i))),
                  pl.BlockSpec((SWA_KV_HEADS, VT_ROWS, QB), lambda i: (0, 0, i)),
                  _const_spec((1, SWA_Q_HEADS * QB)),
                  pl.BlockSpec((QB, width), lambda i: (i, 0)),
                  _const_spec(bias.shape)],
        out_specs=pl.BlockSpec((QB, width), lambda i: (i, 0)),
        out_shape=jax.ShapeDtypeStruct((T, width), BF16),
        compiler_params=_cparams(),
        name="swa_attention",
    )(bounded, qt, k_hm, k_hm, vt_hm, vt_hm, sink_b, gate, bias)


def _pad_cols(w, width):
    return jnp.pad(w, ((0, 0), (0, width - w.shape[1])))


def _even_layer(x, norm_g, w_in, pool_w, pool_scale, q_gain, k_gain, w_out, rel_bias, dsa_bias):
    pool_width = pool_w.shape[0] * pool_w.shape[1]
    dsa_width = DSA_HEADS * HEAD_DIM
    w_all = w_in.astype(BF16)
    w_tail = _pad_cols(w_in[:, w_in.shape[1] // 128 * 128:], 128).astype(BF16)

    q_scale = HEAD_DIM ** -0.5 * LOG2E
    far = rel_bias[NUM_BUCKETS - 1, :DSA_HEADS].astype(F32) * LOG2E
    far_hi = far.astype(BF16)
    far_lo = (far - far_hi.astype(F32)).astype(BF16)
    extra = jnp.zeros((QT_ROWS - HEAD_DIM, DSA_HEADS * QB), BF16)
    extra = extra.at[0].set(jnp.repeat(far_hi, QB)).at[1].set(jnp.repeat(far_lo, QB))
    q_norm = jnp.sqrt(HEAD_DIM * jnp.max(q_gain.astype(F32) ** 2) * q_scale ** 2 + jnp.max(far ** 2))
    k_norm = jnp.sqrt(HEAD_DIM * jnp.max(k_gain.astype(F32) ** 2) + 2.0)
    near_max = jnp.max(jnp.abs(rel_bias[:, :DSA_HEADS] - rel_bias[NUM_BUCKETS - 1, :DSA_HEADS])) * LOG2E
    bound = 1.02 * q_norm * k_norm + near_max
    bounded = (bound < SAFE_LOG2_RANGE).astype(I32).reshape(1)

    py, qt, iqt, gate, k, ik, vt, iwt = _even_proj(
        x, norm_g, w_all, w_tail, pool_w.astype(BF16), pool_scale, q_gain, extra, k_gain, pool_width=pool_width,
        dsa_width=dsa_width, q_scale=q_scale, iw_scale=IDX_HEADS ** -0.5 * HEAD_DIM ** -0.5)
    dy = _dsa_attention(bounded, qt, iqt, iwt, gate, k, ik, vt, dsa_bias)
    return _out_proj(x, [py, dy], w_out)


def _odd_layer(x, norm_g, w_in, q_gain, k_gain, sinks, w_out, rel_bias, swa_bias):
    q_width = SWA_Q_HEADS * HEAD_DIM
    w_all = w_in.astype(BF16)
    q_scale = HEAD_DIM ** -0.5 * LOG2E
    sink_b = jnp.repeat(sinks.astype(F32) * LOG2E, QB).reshape(1, SWA_Q_HEADS * QB)
    q_norm = jnp.sqrt(HEAD_DIM * jnp.max(q_gain.astype(F32) ** 2)) * q_scale
    k_norm = jnp.sqrt(HEAD_DIM * jnp.max(k_gain.astype(F32) ** 2))
    bound = jnp.maximum(1.02 * q_norm * k_norm + jnp.max(jnp.abs(rel_bias)) * LOG2E, jnp.max(jnp.abs(sink_b)))
    bounded = (bound < SAFE_LOG2_RANGE).astype(I32).reshape(1)

    qt, gate, k_hm, vt_hm = _odd_proj(x, norm_g, w_all, q_gain, k_gain, q_width=q_width, q_scale=q_scale)
    go = _swa_attention(bounded, qt, k_hm, vt_hm, sink_b, gate, swa_bias)
    return _out_proj(x, [go], w_out)


def kernel(x, rel_bias, even_norm, even_w_in, even_pool_w, even_pool_scale, even_q_gain, even_k_gain,
           even_w_out, odd_norm, odd_w_in, odd_q_gain, odd_k_gain, odd_sinks, odd_w_out):
    B, T, D = x.shape
    depth = even_norm.shape[0] + odd_norm.shape[0]
    dsa_bias = _bias_tiles(rel_bias, N_BIAS_TILES, DSA_HEADS, QB, QB, 0, minus_far=True)
    swa_bias = _bias_tiles(rel_bias, 1, SWA_Q_HEADS, 2 * QB, 0, QB)[0]
    outs = []
    for b in range(B):
        h = x.reshape(T, D) if B == 1 else x[b]
        for layer in range(depth):
            j = layer // 2
            if layer % 2 == 0:
                h = _even_layer(h, even_norm[j], even_w_in[j], even_pool_w[j], even_pool_scale[j],
                                even_q_gain[j], even_k_gain[j], even_w_out[j], rel_bias, dsa_bias)
            else:
                h = _odd_layer(h, odd_norm[j], odd_w_in[j], odd_q_gain[j], odd_k_gain[j],
                               odd_sinks[j], odd_w_out[j], rel_bias, swa_bias)
        outs.append(h)
    return outs[0].reshape(B, T, D) if B == 1 else jnp.stack(outs, axis=0)
```
